```python
import math
import jax, jax.numpy as jnp
from jax import lax
import numpy as np

D_MODEL = 2048
BATCH = 2
SEQ = 4096
DEPTH = 4
DEC_BATCH = 8
DEC_SEQ = 64
PAST_LEN = 2048

CHUNK = 64
Q_BLOCK = 128
D_CONV = D_MODEL // 2
CONV_WIDTH = 31
N_HEADS = 16
Q_LORA_RANK = D_MODEL // 4
KV_LORA_RANK = D_MODEL // 4
QK_NOPE_DIM = 128
ROPE_DIM = 64
QK_HEAD_DIM = QK_NOPE_DIM + ROPE_DIM
V_HEAD_DIM = D_MODEL // N_HEADS
ROPE_THETA = 10000.0
N_EXPERTS = 16
N_EXPERT_GROUPS = 4
EXPERTS_PER_GROUP = N_EXPERTS // N_EXPERT_GROUPS
TOP_K = 2
D_EXPERT = 512
LN_EPS = 1e-5
RMS_EPS = 1e-6
ALPHA = (2.0 * DEPTH) ** 0.25
BETA = (8.0 * DEPTH) ** -0.25
D_IN = 2 * D_CONV + Q_LORA_RANK + KV_LORA_RANK + ROPE_DIM + 2 * D_MODEL
SPLITS = (2 * D_CONV,
          2 * D_CONV + Q_LORA_RANK,
          2 * D_CONV + Q_LORA_RANK + KV_LORA_RANK,
          2 * D_CONV + Q_LORA_RANK + KV_LORA_RANK + ROPE_DIM)

kernel_name = "streaming_conformer_mla_grouped_moe"


def layer_norm(x, g, b):
    xf = x.astype(jnp.float32)
    mu = jnp.mean(xf, -1, keepdims=True)
    var = jnp.mean(jnp.square(xf - mu), -1, keepdims=True)
    return ((xf - mu) * lax.rsqrt(var + LN_EPS) * g.astype(jnp.float32) + b.astype(jnp.float32)).astype(x.dtype)


def rms_norm(x, g):
    xf = x.astype(jnp.float32)
    return (xf * lax.rsqrt(jnp.mean(jnp.square(xf), -1, keepdims=True) + RMS_EPS) * g.astype(jnp.float32)).astype(x.dtype)


def rope(x, pos):
    half = ROPE_DIM // 2
    inv = jnp.float32(ROPE_THETA) ** (-jnp.arange(half, dtype=jnp.float32) / half)
    ang = pos.astype(jnp.float32)[:, None] * inv[None, :]
    shape = (1, pos.shape[0]) + (1,) * (x.ndim - 3) + (half,)
    cos = jnp.cos(ang).reshape(shape)
    sin = jnp.sin(ang).reshape(shape)
    xf = x.astype(jnp.float32)
    x1, x2 = xf[..., :half], xf[..., half:]
    return jnp.concatenate([x1 * cos - x2 * sin, x2 * cos + x1 * sin], -1).astype(x.dtype)


def conv_branch(u_glu, hist, conv_w, conv_b, conv_ln_g, conv_ln_b, w_conv_pw):
    a, g = u_glu[..., :D_CONV], u_glu[..., D_CONV:]
    u = a * jax.nn.sigmoid(g)
    ext = jnp.concatenate([hist, u], axis=1)
    y = lax.conv_general_dilated(ext, conv_w[:, None, :], (1,), 'VALID',
                                 dimension_numbers=('NWC', 'WIO', 'NWC'),
                                 feature_group_count=D_CONV) + conv_b
    y = jax.nn.silu(layer_norm(y, conv_ln_g, conv_ln_b))
    return y @ w_conv_pw, ext[:, -(CONV_WIDTH - 1):]


def block_causal_attention(q_nope, q_pe, k_nope, k_pe, v, q_pos, k_pos):
    B, S, H, _ = q_nope.shape
    qb = Q_BLOCK if S % Q_BLOCK == 0 else S
    nb = S // qb
    scale = 1.0 / math.sqrt(QK_HEAD_DIM)
    k_chunk = k_pos // CHUNK

    def blk(args):
        qn, qp, qc = args
        s = (jnp.einsum('bqhd,bkhd->bhqk', qn, k_nope)
             + jnp.einsum('bqhr,bkr->bhqk', qp, k_pe)).astype(jnp.float32) * scale
        mask = k_chunk[None, :] <= qc[:, None]
        s = jnp.where(mask[None, None], s, -jnp.inf)
        p = jax.nn.softmax(s, axis=-1).astype(v.dtype)
        return jnp.einsum('bhqk,bkhd->bqhd', p, v)

    qn = q_nope.reshape(B, nb, qb, H, QK_NOPE_DIM).swapaxes(0, 1)
    qp = q_pe.reshape(B, nb, qb, H, ROPE_DIM).swapaxes(0, 1)
    qc = (q_pos // CHUNK).reshape(nb, qb)
    out = lax.map(blk, (qn, qp, qc))
    return out.swapaxes(0, 1).reshape(B, S, H * V_HEAD_DIM)


def grouped_moe(x, w_router, b_router, w_gate, w_up, w_down):
    B, S, _ = x.shape
    scores = jax.nn.sigmoid((x @ w_router).astype(jnp.float32))
    sel = (scores + b_router.astype(jnp.float32)).reshape(B, S, N_EXPERT_GROUPS, EXPERTS_PER_GROUP)
    grp_score = lax.top_k(sel, TOP_K)[0].sum(-1)
    g_idx = jnp.argmax(grp_score, axis=-1)
    in_grp = jnp.take_along_axis(sel, g_idx[..., None, None], axis=-2)[..., 0, :]
    _, local = lax.top_k(in_grp, TOP_K)
    e_idx = g_idx[..., None] * EXPERTS_PER_GROUP + local
    w = jnp.take_along_axis(scores, e_idx, axis=-1)
    w = w / jnp.sum(w, -1, keepdims=True)
    gate = jnp.sum(jax.nn.one_hot(e_idx, N_EXPERTS, dtype=jnp.float32) * w[..., None], axis=-2).astype(x.dtype)
    h = jax.nn.silu(jnp.einsum('bsd,edf->bsef', x, w_gate)) * jnp.einsum('bsd,edf->bsef', x, w_up)
    return jnp.einsum('bsef,efd->bsd', h * gate[..., None], w_down)


def trunk_layer(x, pos, conv_hist, past_lat, past_pe, past_pos,
                w_in, q_norm_g, w_q_b, kv_norm_g, w_kv_b,
                conv_w, conv_b, conv_ln_g, conv_ln_b, w_conv_pw, w_o, ln1_g, ln1_b,
                w_gate, w_up, w_down, ln2_g, ln2_b, w_router, b_router):
    B, S, _ = x.shape
    h = x @ w_in
    u_glu, q_a, c_raw, k_pe_raw, g_raw = jnp.split(h, SPLITS, axis=-1)

    a_out, new_hist = conv_branch(u_glu, conv_hist, conv_w, conv_b, conv_ln_g, conv_ln_b, w_conv_pw)

    q = (rms_norm(q_a, q_norm_g) @ w_q_b).reshape(B, S, N_HEADS, QK_HEAD_DIM)
    q_nope, q_pe = q[..., :QK_NOPE_DIM], rope(q[..., QK_NOPE_DIM:], pos)
    c_kv = rms_norm(c_raw, kv_norm_g)
    k_pe = rope(k_pe_raw, pos)
    lat = jnp.concatenate([past_lat, c_kv], axis=1)
    kpe_all = jnp.concatenate([past_pe, k_pe], axis=1)
    k_pos = jnp.concatenate([past_pos, pos])
    K = lat.shape[1]
    kv = (lat @ w_kv_b).reshape(K and B, K, N_HEADS, QK_NOPE_DIM + V_HEAD_DIM)
    k_nope, v = kv[..., :QK_NOPE_DIM], kv[..., QK_NOPE_DIM:]
    b_out = block_causal_attention(q_nope, q_pe, k_nope, kpe_all, v, pos, k_pos)

    gates = jax.nn.sigmoid(g_raw.astype(jnp.float32)).astype(x.dtype).reshape(B, S, 2, D_MODEL)
    mixed = (gates[:, :, 0] * a_out + gates[:, :, 1] * b_out) @ w_o
    x = layer_norm(ALPHA * x + mixed, ln1_g, ln1_b)
    x = layer_norm(ALPHA * x + grouped_moe(x, w_router, b_router, w_gate, w_up, w_down), ln2_g, ln2_b)
    return x, c_kv, k_pe, new_hist


def setup_inputs(seed: int = 0) -> dict:
    key = jax.random.key(seed)
    ks = jax.random.split(key, 32)
    f32 = jnp.float32
    L = DEPTH

    def nrm(k, shape, scale):
        return jax.random.normal(k, shape, f32) * scale

    def gain(k, shape):
        return 1.0 + 0.05 * jax.random.normal(k, shape, f32)

    w_k = nrm(ks[9], (L, KV_LORA_RANK, N_HEADS, QK_NOPE_DIM), KV_LORA_RANK ** -0.5)
    w_v = nrm(ks[10], (L, KV_LORA_RANK, N_HEADS, V_HEAD_DIM), BETA * KV_LORA_RANK ** -0.5)
    w_kv_b = jnp.concatenate([w_k, w_v], -1).reshape(L, KV_LORA_RANK, N_HEADS * (QK_NOPE_DIM + V_HEAD_DIM))
    return {
        "x_prompt": nrm(ks[0], (BATCH, SEQ, D_MODEL), 1.0),
        "x_sample": nrm(ks[1], (DEC_BATCH, DEC_SEQ, D_MODEL), 1.0),
        "cache_kv_latent": nrm(ks[2], (L, DEC_BATCH, PAST_LEN, KV_LORA_RANK), 1.0),
        "cache_k_rope": nrm(ks[3], (L, DEC_BATCH, PAST_LEN, ROPE_DIM), 1.0),
        "state_conv": nrm(ks[4], (L, DEC_BATCH, CONV_WIDTH - 1, D_CONV), 0.5),
        "w_in": nrm(ks[5], (L, D_MODEL, D_IN), D_MODEL ** -0.5),
        "q_norm_g": gain(ks[6], (L, Q_LORA_RANK)),
        "w_q_b": nrm(ks[7], (L, Q_LORA_RANK, N_HEADS * QK_HEAD_DIM), Q_LORA_RANK ** -0.5),
        "kv_norm_g": gain(ks[8], (L, KV_LORA_RANK)),
        "w_kv_b": w_kv_b,
        "conv_w": nrm(ks[11], (L, CONV_WIDTH, D_CONV), CONV_WIDTH ** -0.5),
        "conv_b": nrm(ks[12], (L, D_CONV), 0.02),
        "conv_ln_g": gain(ks[13], (L, D_CONV)),
        "conv_ln_b": nrm(ks[14], (L, D_CONV), 0.02),
        "w_conv_pw": nrm(ks[15], (L, D_CONV, D_MODEL), BETA * D_CONV ** -0.5),
        "w_o": nrm(ks[16], (L, D_MODEL, D_MODEL), BETA * D_MODEL ** -0.5),
        "ln1_g": gain(ks[17], (L, D_MODEL)),
        "ln1_b": nrm(ks[18], (L, D_MODEL), 0.02),
        "w_gate": nrm(ks[19], (L, N_EXPERTS, D_MODEL, D_EXPERT), D_MODEL ** -0.5),
        "w_up": nrm(ks[20], (L, N_EXPERTS, D_MODEL, D_EXPERT), BETA * D_MODEL ** -0.5),
        "w_down": nrm(ks[21], (L, N_EXPERTS, D_EXPERT, D_MODEL), BETA * D_EXPERT ** -0.5),
        "ln2_g": gain(ks[22], (L, D_MODEL)),
        "ln2_b": nrm(ks[23], (L, D_MODEL), 0.02),
        "w_router": nrm(ks[24], (D_MODEL, N_EXPERTS), D_MODEL ** -0.5),
        "b_router": nrm(ks[25], (N_EXPERTS,), 0.01),
    }


def reference(x_prompt, x_sample, cache_kv_latent, cache_k_rope, state_conv,
              w_in, q_norm_g, w_q_b, kv_norm_g, w_kv_b,
              conv_w, conv_b, conv_ln_g, conv_ln_b, w_conv_pw, w_o, ln1_g, ln1_b,
              w_gate, w_up, w_down, ln2_g, ln2_b, w_router, b_router):
    B, S, _ = x_prompt.shape
    DS = x_sample.shape[1]
    P = cache_kv_latent.shape[2]
    dt = x_prompt.dtype
    pos_p = jnp.arange(S, dtype=jnp.int32)
    pos_s = P + jnp.arange(DS, dtype=jnp.int32)
    past_pos = jnp.arange(P, dtype=jnp.int32)
    no_pos = jnp.zeros((0,), jnp.int32)
    zero_hist = jnp.zeros((B, CONV_WIDTH - 1, D_CONV), dt)
    no_lat = jnp.zeros((B, 0, KV_LORA_RANK), dt)
    no_pe = jnp.zeros((B, 0, ROPE_DIM), dt)

    hp, hs = x_prompt, x_sample
    p_lat, p_pe, p_conv, s_lat, s_pe, s_conv = [], [], [], [], [], []
    for l in range(DEPTH):
        lw = (w_in[l], q_norm_g[l], w_q_b[l], kv_norm_g[l], w_kv_b[l],
              conv_w[l], conv_b[l], conv_ln_g[l], conv_ln_b[l], w_conv_pw[l], w_o[l], ln1_g[l], ln1_b[l],
              w_gate[l], w_up[l], w_down[l], ln2_g[l], ln2_b[l], w_router, b_router)
        hp, c_p, k_p, hist_p = trunk_layer(hp, pos_p, zero_hist, no_lat, no_pe, no_pos, *lw)
        hs, c_s, k_s, hist_s = trunk_layer(hs, pos_s, state_conv[l], cache_kv_latent[l], cache_k_rope[l], past_pos, *lw)
        p_lat.append(c_p); p_pe.append(k_p); p_conv.append(hist_p)
        s_lat.append(c_s); s_pe.append(k_s); s_conv.append(hist_s)

    prompt_kv_latent = jnp.stack(p_lat, 0)
    prompt_k_rope = jnp.stack(p_pe, 0)
    prompt_conv_state = jnp.stack(p_conv, 0)
    sample_kv_latent = jnp.stack(s_lat, 0)
    sample_k_rope = jnp.stack(s_pe, 0)
    sample_conv_state = jnp.stack(s_conv, 0)
    return (hp, hs, prompt_kv_latent, prompt_k_rope, prompt_conv_state,
            sample_kv_latent, sample_k_rope, sample_conv_state)
```

```python
import functools
import math

import jax
import jax.numpy as jnp
from jax import lax
from jax.experimental import pallas as pl
from jax.experimental.pallas import tpu as pltpu

F32 = jnp.float32
BF16 = jnp.bfloat16

CHUNK = 64
CONV_WIDTH = 31
N_HEADS = 16
QK_NOPE_DIM = 128
ROPE_DIM = 64
V_HEAD_DIM = 128
ROPE_THETA = 10000.0
N_EXPERTS = 16
N_EXPERT_GROUPS = 4
EXPERTS_PER_GROUP = N_EXPERTS // N_EXPERT_GROUPS
LN_EPS = 1e-5
RMS_EPS = 1e-6

LANES = 128
HEAD_PAD = 256
CONV_HALO = 32
CONV_OFF = CONV_HALO - (CONV_WIDTH - 1)
NEG_BIG = -1e30
MIB = 1024 * 1024


def _cparams(sem, vmem_mib):
    return pltpu.CompilerParams(dimension_semantics=sem, vmem_limit_bytes=vmem_mib * MIB)


def _dot(a, b):
    return jnp.dot(a, b, preferred_element_type=F32)


def _dot_nt(a, b):
    return lax.dot_general(a, b, (((1,), (1,)), ((), ())), preferred_element_type=F32)


def _layer_norm(y, g, b):
    mu = jnp.mean(y, axis=-1, keepdims=True)
    d = y - mu
    var = jnp.mean(d * d, axis=-1, keepdims=True)
    return d * lax.rsqrt(var + LN_EPS) * g + b


def _rms_norm(y, g):
    return y * lax.rsqrt(jnp.mean(y * y, axis=-1, keepdims=True) + RMS_EPS) * g


def _rope128(v, c, s1, s2):
    return v * c + pltpu.roll(v, 96, axis=1) * s1 + pltpu.roll(v, 32, axis=1) * s2


def _glu_kernel(x_ref, wa_ref, wg_ref, u_ref, wa_bf, wg_bf):
    @pl.when(pl.program_id(1) == 0)
    def _():
        wa_bf[...] = wa_ref[...].astype(BF16)
        wg_bf[...] = wg_ref[...].astype(BF16)

    x = x_ref[...]
    a = _dot(x, wa_bf[...])
    g = _dot(x, wg_bf[...])
    u_ref[...] = a * jax.nn.sigmoid(g)


def _glu_call(x_bf, w_in, l, d_conv, tm, tn):
    t, d = x_bf.shape
    nj = d_conv // tn
    return pl.pallas_call(
        _glu_kernel,
        grid=(nj, t // tm),
        in_specs=[pl.BlockSpec((tm, d), lambda j, i: (i, 0)),
                  pl.BlockSpec((None, d, tn), lambda j, i: (l, 0, j)),
                  pl.BlockSpec((None, d, tn), lambda j, i: (l, 0, j + nj))],
        out_specs=pl.BlockSpec((tm, tn), lambda j, i: (i, j)),
        out_shape=jax.ShapeDtypeStruct((t, d_conv), F32),
        scratch_shapes=[pltpu.VMEM((d, tn), BF16), pltpu.VMEM((d, tn), BF16)],
        compiler_params=_cparams(("arbitrary", "arbitrary"), 48),
        name="glu",
    )(x_bf, w_in, w_in)


def _latent_kernel(x_ref, wq_ref, wc_ref, wk_ref, qg_ref, cg_ref, c_ref, s1_ref, s2_ref,
                   qn_ref, ckv_ref, kpe_ref, kpad_ref, wq_bf, wc_bf, wk_bf):
    @pl.when(pl.program_id(0) == 0)
    def _():
        wq_bf[...] = wq_ref[...].astype(BF16)
        wc_bf[...] = wc_ref[...].astype(BF16)
        wk_bf[...] = wk_ref[...].astype(BF16)

    x = x_ref[...]
    qn_ref[...] = _rms_norm(_dot(x, wq_bf[...]), qg_ref[...]).astype(BF16)
    ckv_ref[...] = _rms_norm(_dot(x, wc_bf[...]), cg_ref[...])
    kr = _dot(x, wk_bf[...])
    k = _rope128(kr, c_ref[...], s1_ref[...], s2_ref[...])
    lane = lax.broadcasted_iota(jnp.int32, k.shape, 1)
    k = jnp.where(lane < ROPE_DIM, k, 0.0)
    kpe_ref[...] = k[:, :ROPE_DIM]
    kpad_ref[...] = k.astype(BF16)


def _latent_call(x_bf, w_in, q_norm_g, kv_norm_g, tabs, l, splits, rq, rkv, tm):
    t, d = x_bf.shape
    assert splits[0] % rq == 0 and splits[1] % rkv == 0 and splits[2] % LANES == 0
    row = lambda i: (i, 0)
    return pl.pallas_call(
        _latent_kernel,
        grid=(t // tm,),
        in_specs=[pl.BlockSpec((tm, d), row),
                  pl.BlockSpec((None, d, rq), lambda i: (l, 0, splits[0] // rq)),
                  pl.BlockSpec((None, d, rkv), lambda i: (l, 0, splits[1] // rkv)),
                  pl.BlockSpec((None, d, LANES), lambda i: (l, 0, splits[2] // LANES)),
                  pl.BlockSpec((None, 1, rq), lambda i: (l, 0, 0)),
                  pl.BlockSpec((None, 1, rkv), lambda i: (l, 0, 0)),
                  pl.BlockSpec((tm, LANES), row),
                  pl.BlockSpec((tm, LANES), row),
                  pl.BlockSpec((tm, LANES), row)],
        out_specs=[pl.BlockSpec((tm, rq), row),
                   pl.BlockSpec((tm, rkv), row),
                   pl.BlockSpec((tm, ROPE_DIM), row),
                   pl.BlockSpec((tm, LANES), row)],
        out_shape=[jax.ShapeDtypeStruct((t, rq), BF16),
                   jax.ShapeDtypeStruct((t, rkv), F32),
                   jax.ShapeDtypeStruct((t, ROPE_DIM), F32),
                   jax.ShapeDtypeStruct((t, LANES), BF16)],
        scratch_shapes=[pltpu.VMEM((d, rq), BF16), pltpu.VMEM((d, rkv), BF16), pltpu.VMEM((d, LANES), BF16)],
        compiler_params=_cparams(("arbitrary",), 48),
        name="latent",
    )(x_bf, w_in, w_in, w_in, q_norm_g, kv_norm_g, *tabs)


def _gates_kernel(x_ref, w_ref, o_ref):
    o_ref[...] = jax.nn.sigmoid(_dot(x_ref[...], w_ref[...])).astype(BF16)


def _gates_call(x_bf, w_g, l, tm, tn):
    t, d = x_bf.shape
    n = w_g.shape[-1]
    return pl.pallas_call(
        _gates_kernel,
        grid=(n // tn, t // tm),
        in_specs=[pl.BlockSpec((tm, d), lambda j, i: (i, 0)),
                  pl.BlockSpec((None, d, tn), lambda j, i: (l, 0, j))],
        out_specs=pl.BlockSpec((tm, tn), lambda j, i: (i, j)),
        out_shape=jax.ShapeDtypeStruct((t, n), BF16),
        compiler_params=_cparams(("arbitrary", "arbitrary"), 48),
        name="gates",
    )(x_bf, w_g)


def _conv_tail(ext_ref, y_ref, cw_ref, cb_ref, lg_ref, lb_ref, wpw_bf, gate_ref, ga_ref, ts, rows):
    c_dim = y_ref.shape[1]
    for c in range(c_dim // LANES):
        cs = slice(c * LANES, (c + 1) * LANES)

        def rbody(r, carry, cs=cs):
            r0 = pl.multiple_of(r * rows, rows)
            win = ext_ref[pl.ds(r0, rows + CONV_HALO), cs]
            acc = jnp.zeros((rows, LANES), F32)
            for k in range(CONV_WIDTH):
                o = k + CONV_OFF
                acc = acc + win[o:o + rows, :] * cw_ref[k:k + 1, cs]
            y_ref[pl.ds(r0, rows), cs] = acc
            return carry

        lax.fori_loop(0, ts // rows, rbody, 0)

    y = _layer_norm(y_ref[...] + cb_ref[...], lg_ref[...], lb_ref[...])
    z = (y * jax.nn.sigmoid(y)).astype(BF16)
    a_out = _dot(z, wpw_bf[...])
    ga_ref[...] = (gate_ref[...].astype(F32) * a_out).astype(BF16)


def _conv_prompt_kernel(u_ref, halo_ref, cw_ref, cb_ref, lg_ref, lb_ref, wpw_ref, gate_ref, ga_ref,
                        ext_ref, y_ref, wpw_bf, *, ts, rows):
    b, i = pl.program_id(0), pl.program_id(1)

    @pl.when((b == 0) & (i == 0))
    def _():
        wpw_bf[...] = wpw_ref[...].astype(BF16)

    @pl.when(i == 0)
    def _():
        ext_ref[0:CONV_HALO, :] = jnp.zeros((CONV_HALO, ext_ref.shape[1]), F32)

    @pl.when(i > 0)
    def _():
        ext_ref[0:CONV_HALO, :] = halo_ref[...]

    ext_ref[CONV_HALO:CONV_HALO + ts, :] = u_ref[...]
    _conv_tail(ext_ref, y_ref, cw_ref, cb_ref, lg_ref, lb_ref, wpw_bf, gate_ref, ga_ref, ts, rows)


def _conv_sample_kernel(u_ref, hist_ref, cw_ref, cb_ref, lg_ref, lb_ref, wpw_ref, gate_ref, ga_ref,
                        ext_ref, y_ref, wpw_bf, *, ts, rows):
    @pl.when(pl.program_id(0) == 0)
    def _():
        wpw_bf[...] = wpw_ref[...].astype(BF16)

    ext_ref[0:CONV_HALO, :] = jnp.zeros((CONV_HALO, ext_ref.shape[1]), F32)
    ext_ref[CONV_OFF:CONV_HALO, :] = hist_ref[...]
    ext_ref[CONV_HALO:CONV_HALO + ts, :] = u_ref[...]
    _conv_tail(ext_ref, y_ref, cw_ref, cb_ref, lg_ref, lb_ref, wpw_bf, gate_ref, ga_ref, ts, rows)


def _conv_weight_specs(l, c_dim, d, nargs):
    z = (lambda *a: (l, 0, 0))
    return [pl.BlockSpec((None, CONV_WIDTH, c_dim), z),
            pl.BlockSpec((None, 1, c_dim), z),
            pl.BlockSpec((None, 1, c_dim), z),
            pl.BlockSpec((None, 1, c_dim), z),
            pl.BlockSpec((None, c_dim, d), z)]


def _conv_prompt_call(u, gates, conv_w, conv_b, ln_g, ln_b, w_pw, l, nb, seq, ts, rows):
    c_dim = u.shape[1]
    d = w_pw.shape[-1]
    ns = seq // ts
    hb = ts // CONV_HALO
    tile = lambda b, i: (b * ns + i, 0)
    return pl.pallas_call(
        functools.partial(_conv_prompt_kernel, ts=ts, rows=rows),
        grid=(nb, ns),
        in_specs=[pl.BlockSpec((ts, c_dim), tile),
                  pl.BlockSpec((CONV_HALO, c_dim), lambda b, i: (jnp.maximum((b * ns + i) * hb - 1, 0), 0))]
        + _conv_weight_specs(l, c_dim, d, 2)
        + [pl.BlockSpec((ts, d), tile)],
        out_specs=pl.BlockSpec((ts, d), tile),
        out_shape=jax.ShapeDtypeStruct((nb * seq, d), BF16),
        scratch_shapes=[pltpu.VMEM((CONV_HALO + ts, c_dim), F32), pltpu.VMEM((ts, c_dim), F32),
                        pltpu.VMEM((c_dim, d), BF16)],
        compiler_params=_cparams(("arbitrary", "arbitrary"), 56),
        name="conv_prompt",
    )(u, u, conv_w, conv_b, ln_g, ln_b, w_pw, gates)


def _conv_sample_call(u, gates, state_conv, conv_w, conv_b, ln_g, ln_b, w_pw, l, row0, nb, ts, rows):
    c_dim = u.shape[1]
    d = w_pw.shape[-1]
    t0 = row0 // ts
    tile = lambda b: (t0 + b, 0)
    return pl.pallas_call(
        functools.partial(_conv_sample_kernel, ts=ts, rows=rows),
        grid=(nb,),
        in_specs=[pl.BlockSpec((ts, c_dim), tile),
                  pl.BlockSpec((None, None, CONV_WIDTH - 1, c_dim), lambda b: (l, b, 0, 0))]
        + _conv_weight_specs(l, c_dim, d, 1)
        + [pl.BlockSpec((ts, d), tile)],
        out_specs=pl.BlockSpec((ts, d), lambda b: (b, 0)),
        out_shape=jax.ShapeDtypeStruct((nb * ts, d), BF16),
        scratch_shapes=[pltpu.VMEM((CONV_HALO + ts, c_dim), F32), pltpu.VMEM((ts, c_dim), F32),
                        pltpu.VMEM((c_dim, d), BF16)],
        compiler_params=_cparams(("arbitrary",), 56),
        name="conv_sample",
    )(u, state_conv, conv_w, conv_b, ln_g, ln_b, w_pw, gates)


def _q_kernel(qn_ref, w_ref, c_ref, s1_ref, s2_ref, q_ref, *, scale):
    qn = qn_ref[...]
    c, s1, s2 = c_ref[...], s1_ref[...], s2_ref[...]
    for h in range(N_HEADS):
        qh = _dot(qn, w_ref[:, h * HEAD_PAD:(h + 1) * HEAD_PAD])
        q_ref[:, h * HEAD_PAD:h * HEAD_PAD + LANES] = (qh[:, :LANES] * scale).astype(BF16)
        q_ref[:, h * HEAD_PAD + LANES:(h + 1) * HEAD_PAD] = (_rope128(qh[:, LANES:], c, s1, s2) * scale).astype(BF16)


def _q_call(qn, w_q, tabs, l, tm, scale):
    t, r = qn.shape
    n = w_q.shape[-1]
    row = lambda i: (i, 0)
    return pl.pallas_call(
        functools.partial(_q_kernel, scale=scale),
        grid=(t // tm,),
        in_specs=[pl.BlockSpec((tm, r), row),
                  pl.BlockSpec((None, r, n), lambda i: (l, 0, 0)),
                  pl.BlockSpec((tm, LANES), row), pl.BlockSpec((tm, LANES), row), pl.BlockSpec((tm, LANES), row)],
        out_specs=pl.BlockSpec((tm, n), row),
        out_shape=jax.ShapeDtypeStruct((t, n), BF16),
        compiler_params=_cparams(("arbitrary",), 48),
        name="q_proj",
    )(qn, w_q, *tabs)


def _kv_kernel(lat_ref, kpad_ref, w_ref, k_ref, vt_ref, w_bf):
    @pl.when(pl.program_id(0) == 0)
    def _():
        w_bf[...] = w_ref[...].astype(BF16)

    lat = lat_ref[...].astype(BF16)
    kpad = kpad_ref[...]
    for h in range(N_HEADS):
        kv = _dot(lat, w_bf[:, h * HEAD_PAD:(h + 1) * HEAD_PAD])
        k_ref[:, h * HEAD_PAD:h * HEAD_PAD + LANES] = kv[:, :LANES].astype(BF16)
        k_ref[:, h * HEAD_PAD + LANES:(h + 1) * HEAD_PAD] = kpad
        vt_ref[h * V_HEAD_DIM:(h + 1) * V_HEAD_DIM, :] = kv[:, LANES:].T.astype(BF16)


def _kv_call(ckv, kpad, w_kv_b, l, rows, tm):
    r = ckv.shape[1]
    n = w_kv_b.shape[-1]
    row = lambda i: (i, 0)
    return pl.pallas_call(
        _kv_kernel,
        grid=(rows // tm,),
        in_specs=[pl.BlockSpec((tm, r), row),
                  pl.BlockSpec((tm, LANES), row),
                  pl.BlockSpec((None, r, n), lambda i: (l, 0, 0))],
        out_specs=[pl.BlockSpec((tm, N_HEADS * HEAD_PAD), row),
                   pl.BlockSpec((N_HEADS * V_HEAD_DIM, tm), lambda i: (0, i))],
        out_shape=[jax.ShapeDtypeStruct((rows, N_HEADS * HEAD_PAD), BF16),
                   jax.ShapeDtypeStruct((N_HEADS * V_HEAD_DIM, rows), BF16)],
        scratch_shapes=[pltpu.VMEM((r, n), BF16)],
        compiler_params=_cparams(("arbitrary",), 48),
        name="kv_proj",
    )(ckv, kpad, w_kv_b)


def _attn_prompt_kernel(q_ref, k_ref, vt_ref, ga_ref, gb_ref, o_ref, acc_ref, m_ref, l_ref, *, seq, tq):
    def step(q, q0, k0, masked):
        kt = k_ref[pl.ds(k0, tq), :]
        s = _dot_nt(kt, q)
        if masked:
            kk = lax.broadcasted_iota(jnp.int32, s.shape, 0) // CHUNK
            qq = lax.broadcasted_iota(jnp.int32, s.shape, 1) // CHUNK
            s = jnp.where(kk <= qq, s, NEG_BIG)
        m_old = m_ref[...]
        m_new = jnp.maximum(m_old, jnp.max(s, axis=0, keepdims=True))
        alpha = jnp.exp(m_old - m_new)
        p = jnp.exp(s - m_new)
        l_ref[...] = alpha * l_ref[...] + jnp.sum(p, axis=0, keepdims=True)
        pv = _dot(vt_ref[:, pl.ds(k0, tq)], p.astype(BF16))
        acc_ref[...] = alpha * acc_ref[...] + pv
        m_ref[...] = m_new

    def q_body(qi, carry):
        q0 = pl.multiple_of(qi * tq, tq)
        q = q_ref[pl.ds(q0, tq), :]
        m_ref[...] = jnp.full(m_ref.shape, NEG_BIG, F32)
        l_ref[...] = jnp.zeros(l_ref.shape, F32)
        acc_ref[...] = jnp.zeros(acc_ref.shape, F32)

        def k_body(j, c):
            step(q, q0, pl.multiple_of(j * tq, tq), False)
            return c

        lax.fori_loop(0, qi, k_body, 0)
        step(q, q0, q0, True)
        o = (acc_ref[...] / l_ref[...]).T
        rows = pl.ds(q0, tq)
        o_ref[rows, :] = (ga_ref[rows, :].astype(F32) + gb_ref[rows, :].astype(F32) * o).astype(BF16)
        return carry

    lax.fori_loop(0, seq // tq, q_body, 0)


def _attn_prompt_call(q, k, vt, ga, gates, nb, seq, tq):
    d = N_HEADS * V_HEAD_DIM
    bh = lambda b, h: (b, h)
    return pl.pallas_call(
        functools.partial(_attn_prompt_kernel, seq=seq, tq=tq),
        grid=(nb, N_HEADS),
        in_specs=[pl.BlockSpec((seq, HEAD_PAD), bh),
                  pl.BlockSpec((seq, HEAD_PAD), bh),
                  pl.BlockSpec((V_HEAD_DIM, seq), lambda b, h: (h, b)),
                  pl.BlockSpec((seq, V_HEAD_DIM), bh),
                  pl.BlockSpec((seq, V_HEAD_DIM), lambda b, h: (b, N_HEADS + h))],
        out_specs=pl.BlockSpec((seq, V_HEAD_DIM), bh),
        out_shape=jax.ShapeDtypeStruct((nb * seq, d), BF16),
        scratch_shapes=[pltpu.VMEM((V_HEAD_DIM, tq), F32), pltpu.VMEM((1, tq), F32), pltpu.VMEM((1, tq), F32)],
        compiler_params=_cparams(("arbitrary", "arbitrary"), 48),
        name="attn_prompt",
    )(q, k, vt, ga, gates)


def _attn_sample_kernel(q_ref, latp_ref, kpep_ref, latn_ref, kpen_ref, w_ref, ga_ref, gb_ref, o_ref,
                        w_bf, qlat_ref, qpe_ref, kpp_ref, kpn_ref, olat_ref, *, ds):
    @pl.when(pl.program_id(0) == 0)
    def _():
        w_bf[...] = w_ref[...].astype(BF16)
        kpp_ref[...] = jnp.zeros(kpp_ref.shape, BF16)
        kpn_ref[...] = jnp.zeros(kpn_ref.shape, BF16)

    kpp_ref[:, :ROPE_DIM] = kpep_ref[...].astype(BF16)
    kpn_ref[:, :ROPE_DIM] = kpen_ref[...].astype(BF16)
    for h in range(N_HEADS):
        rows = slice(h * ds, (h + 1) * ds)
        qn = q_ref[:, h * HEAD_PAD:h * HEAD_PAD + LANES]
        qlat_ref[rows, :] = _dot_nt(qn, w_bf[:, h * HEAD_PAD:h * HEAD_PAD + LANES]).astype(BF16)
        qpe_ref[rows, :] = q_ref[:, h * HEAD_PAD + LANES:(h + 1) * HEAD_PAD]

    lat_p = latp_ref[...].astype(BF16)
    lat_n = latn_ref[...].astype(BF16)
    qlat, qpe = qlat_ref[...], qpe_ref[...]
    s_p = _dot_nt(qlat, lat_p) + _dot_nt(qpe, kpp_ref[...])
    s_n = _dot_nt(qlat, lat_n) + _dot_nt(qpe, kpn_ref[...])
    m = jnp.maximum(jnp.max(s_p, axis=1, keepdims=True), jnp.max(s_n, axis=1, keepdims=True))
    p_p = jnp.exp(s_p - m)
    p_n = jnp.exp(s_n - m)
    den = jnp.sum(p_p, axis=1, keepdims=True) + jnp.sum(p_n, axis=1, keepdims=True)
    o_lat = _dot(p_p.astype(BF16), lat_p) + _dot(p_n.astype(BF16), lat_n)
    olat_ref[...] = (o_lat / den).astype(BF16)
    for h in range(N_HEADS):
        cols = slice(h * V_HEAD_DIM, (h + 1) * V_HEAD_DIM)
        o = _dot(olat_ref[h * ds:(h + 1) * ds, :], w_bf[:, h * HEAD_PAD + LANES:(h + 1) * HEAD_PAD])
        o_ref[:, cols] = (ga_ref[:, cols].astype(F32) + gb_ref[:, cols].astype(F32) * o).astype(BF16)


def _attn_sample_call(q, cache_lat, cache_pe, ckv, kpe, w_kv_b, ga_s, gates, l, row0, nb, ds):
    past, r = cache_lat.shape[2], cache_lat.shape[3]
    d = N_HEADS * V_HEAD_DIM
    t0 = row0 // ds
    tile = lambda b: (t0 + b, 0)
    return pl.pallas_call(
        functools.partial(_attn_sample_kernel, ds=ds),
        grid=(nb,),
        in_specs=[pl.BlockSpec((ds, N_HEADS * HEAD_PAD), tile),
                  pl.BlockSpec((None, None, past, r), lambda b: (l, b, 0, 0)),
                  pl.BlockSpec((None, None, past, ROPE_DIM), lambda b: (l, b, 0, 0)),
                  pl.BlockSpec((ds, r), tile),
                  pl.BlockSpec((ds, ROPE_DIM), tile),
                  pl.BlockSpec((None, r, N_HEADS * HEAD_PAD), lambda b: (l, 0, 0)),
                  pl.BlockSpec((ds, d), lambda b: (b, 0)),
                  pl.BlockSpec((ds, d), lambda b: (t0 + b, 1))],
        out_specs=pl.BlockSpec((ds, d), lambda b: (b, 0)),
        out_shape=jax.ShapeDtypeStruct((nb * ds, d), BF16),
        scratch_shapes=[pltpu.VMEM((r, N_HEADS * HEAD_PAD), BF16),
                        pltpu.VMEM((N_HEADS * ds, r), BF16),
                        pltpu.VMEM((N_HEADS * ds, LANES), BF16),
                        pltpu.VMEM((past, LANES), BF16),
                        pltpu.VMEM((ds, LANES), BF16),
                        pltpu.VMEM((N_HEADS * ds, r), BF16)],
        compiler_params=_cparams(("arbitrary",), 56),
        name="attn_sample",
    )(q, cache_lat, cache_pe, ckv, kpe, w_kv_b, ga_s, gates)


def _route(scores, bias):
    sel = [s + b for s, b in zip(scores, bias)]
    n = EXPERTS_PER_GROUP
    grp = []
    for g in range(N_EXPERT_GROUPS):
        v = sel[g * n:(g + 1) * n]
        best = None
        for i in range(n):
            for j in range(i + 1, n):
                pair = v[i] + v[j]
                best = pair if best is None else jnp.maximum(best, pair)
        grp.append(best)
    g_idx = jnp.zeros_like(grp[0], dtype=jnp.int32)
    g_best = grp[0]
    for g in range(1, N_EXPERT_GROUPS):
        better = grp[g] > g_best
        g_idx = jnp.where(better, g, g_idx)
        g_best = jnp.where(better, grp[g], g_best)

    def pick(rows_by_group):
        out = rows_by_group[0]
        for g in range(1, N_EXPERT_GROUPS):
            out = jnp.where(g_idx == g, rows_by_group[g], out)
        return out

    in_sel = [pick([sel[g * n + j] for g in range(N_EXPERT_GROUPS)]) for j in range(n)]
    in_sc = [pick([scores[g * n + j] for g in range(N_EXPERT_GROUPS)]) for j in range(n)]

    def argmax_first(vals, excluded):
        idx = None
        best = None
        for j in range(n):
            v = vals[j] if excluded is None else jnp.where(excluded == j, -jnp.inf, vals[j])
            if best is None:
                best, idx = v, jnp.zeros_like(g_idx)
            else:
                better = v > best
                idx = jnp.where(better, j, idx)
                best = jnp.where(better, v, best)
        return idx

    l0 = argmax_first(in_sel, None)
    l1 = argmax_first(in_sel, l0)

    def take(vals, idx):
        out = vals[0]
        for j in range(1, n):
            out = jnp.where(idx == j, vals[j], out)
        return out

    w0, w1 = take(in_sc, l0), take(in_sc, l1)
    tot = w0 + w1
    return g_idx * n + l0, g_idx * n + l1, w0 / tot, w1 / tot


def _wo_kernel(mix_ref, x_ref, w_ref, g_ref, b_ref, wr_ref, br_ref, x1_ref, x1bf_ref, gate_ref, w_bf, *, alpha):
    @pl.when(pl.program_id(0) == 0)
    def _():
        w_bf[...] = w_ref[...].astype(BF16)

    y = alpha * x_ref[...] + _dot(mix_ref[...], w_bf[...])
    x1 = _layer_norm(y, g_ref[...], b_ref[...])
    x1_ref[...] = x1
    x1bf_ref[...] = x1.astype(BF16)
    logits = lax.dot_general(wr_ref[...], x1, (((1,), (1,)), ((), ())), preferred_element_type=F32,
                             precision=lax.Precision.HIGHEST)
    sc = jax.nn.sigmoid(logits)
    br = br_ref[...]
    e0, e1, g0, g1 = _route([sc[e:e + 1, :] for e in range(N_EXPERTS)],
                            [br[e:e + 1, :] for e in range(N_EXPERTS)])
    rows = lax.broadcasted_iota(jnp.int32, (LANES, sc.shape[1]), 0)
    dense = jnp.where(rows == e0, g0, 0.0) + jnp.where(rows == e1, g1, 0.0)
    gate_ref[...] = dense.T


def _wo_call(mixed, x, w_o, ln_g, ln_b, wr_t, br, l, tm, alpha):
    t, d = x.shape
    row = lambda i: (i, 0)
    vec = pl.BlockSpec((None, 1, d), lambda i: (l, 0, 0))
    return pl.pallas_call(
        functools.partial(_wo_kernel, alpha=alpha),
        grid=(t // tm,),
        in_specs=[pl.BlockSpec((tm, d), row),
                  pl.BlockSpec((tm, d), row),
                  pl.BlockSpec((None, d, d), lambda i: (l, 0, 0), pipeline_mode=pl.Buffered(1)),
                  vec, vec,
                  pl.BlockSpec((N_EXPERTS, d), lambda i: (0, 0)),
                  pl.BlockSpec((N_EXPERTS, 1), lambda i: (0, 0))],
        out_specs=[pl.BlockSpec((tm, d), row), pl.BlockSpec((tm, d), row), pl.BlockSpec((tm, LANES), row)],
        out_shape=[jax.ShapeDtypeStruct((t, d), F32), jax.ShapeDtypeStruct((t, d), BF16),
                   jax.ShapeDtypeStruct((t, LANES), F32)],
        scratch_shapes=[pltpu.VMEM((d, d), BF16)],
        compiler_params=_cparams(("arbitrary",), 56),
        name="wo_ln_router",
    )(mixed, x, w_o, ln_g, ln_b, wr_t, br)


def _moe_kernel(xbf_ref, x1_ref, gate_ref, wg_ref, wu_ref, wd_ref, g_ref, b_ref, x2_ref, x2bf_ref, acc_ref, *, alpha):
    e = pl.program_id(1)

    @pl.when(e == 0)
    def _():
        acc_ref[...] = jnp.zeros(acc_ref.shape, F32)

    x = xbf_ref[...]
    hg = _dot(x, wg_ref[...])
    hu = _dot(x, wu_ref[...])
    gate = gate_ref[...]
    lane = lax.broadcasted_iota(jnp.int32, gate.shape, 1)
    ge = jnp.sum(jnp.where(lane == e, gate, 0.0), axis=1, keepdims=True)
    h = hg * jax.nn.sigmoid(hg) * hu * ge
    acc_ref[...] += _dot(h.astype(BF16), wd_ref[...])

    @pl.when(e == pl.num_programs(1) - 1)
    def _():
        x2 = _layer_norm(alpha * x1_ref[...] + acc_ref[...], g_ref[...], b_ref[...])
        x2_ref[...] = x2
        x2bf_ref[...] = x2.astype(BF16)


def _moe_call(x1_bf, x1, gate, w_gate, w_up, w_down, ln_g, ln_b, l, tm, alpha):
    t, d = x1.shape
    f = w_gate.shape[-1]
    row = lambda i, e: (i, 0)
    vec = pl.BlockSpec((None, 1, d), lambda i, e: (l, 0, 0))
    return pl.pallas_call(
        functools.partial(_moe_kernel, alpha=alpha),
        grid=(t // tm, N_EXPERTS),
        in_specs=[pl.BlockSpec((tm, d), row), pl.BlockSpec((tm, d), row), pl.BlockSpec((tm, LANES), row),
                  pl.BlockSpec((None, None, d, f), lambda i, e: (l, e, 0, 0)),
                  pl.BlockSpec((None, None, d, f), lambda i, e: (l, e, 0, 0)),
                  pl.BlockSpec((None, None, f, d), lambda i, e: (l, e, 0, 0)),
                  vec, vec],
        out_specs=[pl.BlockSpec((tm, d), row), pl.BlockSpec((tm, d), row)],
        out_shape=[jax.ShapeDtypeStruct((t, d), F32), jax.ShapeDtypeStruct((t, d), BF16)],
        scratch_shapes=[pltpu.VMEM((tm, d), F32)],
        compiler_params=_cparams(("arbitrary", "arbitrary"), 56),
        name="moe_ln",
    )(x1_bf, x1, gate, w_gate, w_up, w_down, ln_g, ln_b)


def _rope_tables(pos):
    half = ROPE_DIM // 2
    inv = jnp.float32(ROPE_THETA) ** (-jnp.arange(half, dtype=F32) / half)
    ang = pos.astype(F32)[:, None] * inv[None, :]
    cos, sin = jnp.cos(ang), jnp.sin(ang)
    z = jnp.zeros_like(cos)
    return (jnp.concatenate([cos, cos, z, z], 1),
            jnp.concatenate([-sin, z, z, z], 1),
            jnp.concatenate([z, sin, z, z], 1))


def kernel(x_prompt, x_sample, cache_kv_latent, cache_k_rope, state_conv, w_in, q_norm_g, w_q_b, kv_norm_g, w_kv_b,
           conv_w, conv_b, conv_ln_g, conv_ln_b, w_conv_pw, w_o, ln1_g, ln1_b, w_gate, w_up, w_down, ln2_g, ln2_b,
           w_router, b_router):
    nb, seq, d = x_prompt.shape
    db, ds, _ = x_sample.shape
    depth = w_in.shape[0]
    past = cache_kv_latent.shape[2]
    d_conv = conv_w.shape[-1]
    rq, rkv = q_norm_g.shape[-1], kv_norm_g.shape[-1]
    tp, tsmp = nb * seq, db * ds
    t_all = tp + tsmp
    splits = (2 * d_conv, 2 * d_conv + rq, 2 * d_conv + rq + rkv, 2 * d_conv + rq + rkv + ROPE_DIM)
    alpha = (2.0 * depth) ** 0.25
    scale = 1.0 / math.sqrt(QK_NOPE_DIM + ROPE_DIM)
    tm = 512
    tq = 512
    tm_moe = t_all // 16
    assert t_all == 16 * tm_moe and tm_moe % 16 == 0
    assert d == N_HEADS * V_HEAD_DIM and w_kv_b.shape[-1] == N_HEADS * HEAD_PAD
    assert tp % tm == 0 and tsmp % tm == 0 and seq % tq == 0 and tq % CHUNK == 0
    assert past % CHUNK == 0 and ds <= CHUNK and ds >= CONV_WIDTH - 1

    pos = jnp.concatenate([jnp.tile(jnp.arange(seq, dtype=jnp.int32), nb),
                           jnp.tile(past + jnp.arange(ds, dtype=jnp.int32), db)])
    tabs = _rope_tables(pos)

    w_g = w_in[:, :, splits[3]:].astype(BF16)
    wq = w_q_b.reshape(depth, rq, N_HEADS, QK_NOPE_DIM + ROPE_DIM)
    wq = jnp.pad(wq, ((0, 0), (0, 0), (0, 0), (0, HEAD_PAD - QK_NOPE_DIM - ROPE_DIM)))
    wq = wq.reshape(depth, rq, N_HEADS * HEAD_PAD).astype(BF16)
    w_gate_bf, w_up_bf, w_down_bf = w_gate.astype(BF16), w_up.astype(BF16), w_down.astype(BF16)
    vec3 = lambda a: a.reshape(depth, 1, a.shape[-1])
    q_norm_g3, kv_norm_g3 = vec3(q_norm_g), vec3(kv_norm_g)
    conv_b3, conv_ln_g3, conv_ln_b3 = vec3(conv_b), vec3(conv_ln_g), vec3(conv_ln_b)
    ln1_g3, ln1_b3, ln2_g3, ln2_b3 = vec3(ln1_g), vec3(ln1_b), vec3(ln2_g), vec3(ln2_b)
    wr_t = w_router.T
    br = b_router.reshape(N_EXPERTS, 1)

    x = jnp.concatenate([x_prompt.reshape(tp, d), x_sample.reshape(tsmp, d)], axis=0)
    x_bf = x.astype(BF16)

    lat_out, pe_out, u_out = [], [], []
    for l in range(depth):
        u = _glu_call(x_bf, w_in, l, d_conv, tm, 512)
        qn, ckv, kpe, kpad = _latent_call(x_bf, w_in, q_norm_g3, kv_norm_g3, tabs, l, splits, rq, rkv, tm)
        gates = _gates_call(x_bf, w_g, l, tm, 1024)
        ga_p = _conv_prompt_call(u, gates, conv_w, conv_b3, conv_ln_g3, conv_ln_b3, w_conv_pw, l, nb, seq, 512, 64)
        ga_s = _conv_sample_call(u, gates, state_conv, conv_w, conv_b3, conv_ln_g3, conv_ln_b3, w_conv_pw,
                                 l, tp, db, ds, ds)
        q = _q_call(qn, wq, tabs, l, tm, scale)
        k, vt = _kv_call(ckv, kpad, w_kv_b, l, tp, tm)
        mixed_p = _attn_prompt_call(q, k, vt, ga_p, gates, nb, seq, tq)
        mixed_s = _attn_sample_call(q, cache_kv_latent, cache_k_rope, ckv, kpe, w_kv_b, ga_s, gates, l, tp, db, ds)
        mixed = jnp.concatenate([mixed_p, mixed_s], axis=0)
        x1, x1_bf, gate = _wo_call(mixed, x, w_o, ln1_g3, ln1_b3, wr_t, br, l, tm, alpha)
        x, x_bf = _moe_call(x1_bf, x1, gate, w_gate_bf, w_up_bf, w_down_bf, ln2_g3, ln2_b3, l, tm_moe, alpha)
        lat_out.append(ckv)
        pe_out.append(kpe)
        u_out.append(u)

    keep = CONV_WIDTH - 1
    lat = jnp.stack(lat_out)
    pe = jnp.stack(pe_out)
    uu = jnp.stack(u_out)
    return (x[:tp].reshape(nb, seq, d),
            x[tp:].reshape(db, ds, d),
            lat[:, :tp].reshape(depth, nb, seq, rkv),
            pe[:, :tp].reshape(depth, nb, seq, ROPE_DIM),
            uu[:, :tp].reshape(depth, nb, seq, d_conv)[:, :, seq - keep:],
            lat[:, tp:].reshape(depth, db, ds, rkv),
            pe[:, tp:].reshape(depth, db, ds, ROPE_DIM),
            uu[:, tp:].reshape(depth, db, ds, d_conv)[:, :, ds - keep:])
```

```python
import functools
import math

import jax
import jax.numpy as jnp
from jax import lax
from jax.experimental import pallas as pl
from jax.experimental.pallas import tpu as pltpu

F32 = jnp.float32
BF16 = jnp.bfloat16

CHUNK = 64
CONV_WIDTH = 31
N_HEADS = 16
QK_NOPE_DIM = 128
ROPE_DIM = 64
V_HEAD_DIM = 128
ROPE_THETA = 10000.0
N_EXPERTS = 16
N_EXPERT_GROUPS = 4
EXPERTS_PER_GROUP = N_EXPERTS // N_EXPERT_GROUPS
LN_EPS = 1e-5
RMS_EPS = 1e-6

LANES = 128
SUBLANES = 8
MOE_TILE = 256
PREFIX_CHUNK = 512
HEAD_PAD = 256
CONV_HALO = 32
CONV_OFF = CONV_HALO - (CONV_WIDTH - 1)
NEG_BIG = -1e30
MIB = 1024 * 1024


def _cparams(sem, vmem_mib):
    return pltpu.CompilerParams(dimension_semantics=sem, vmem_limit_bytes=vmem_mib * MIB)


def _dot(a, b):
    return jnp.dot(a, b, preferred_element_type=F32)


def _dot_nt(a, b):
    return lax.dot_general(a, b, (((1,), (1,)), ((), ())), preferred_element_type=F32)


def _layer_norm(y, g, b):
    mu = jnp.mean(y, axis=-1, keepdims=True)
    d = y - mu
    var = jnp.mean(d * d, axis=-1, keepdims=True)
    return d * lax.rsqrt(var + LN_EPS) * g + b


def _rms_norm(y, g):
    return y * lax.rsqrt(jnp.mean(y * y, axis=-1, keepdims=True) + RMS_EPS) * g


def _rope128(v, c, s1, s2):
    return v * c + pltpu.roll(v, 96, axis=1) * s1 + pltpu.roll(v, 32, axis=1) * s2


def _glu_kernel(x_ref, wa_ref, wg_ref, u_ref, wa_bf, wg_bf):
    @pl.when(pl.program_id(1) == 0)
    def _():
        wa_bf[...] = wa_ref[...].astype(BF16)
        wg_bf[...] = wg_ref[...].astype(BF16)

    x = x_ref[...]
    a = _dot_nt(x, wa_bf[...])
    g = _dot_nt(x, wg_bf[...])
    u_ref[...] = a * jax.nn.sigmoid(g)


def _glu_call(x_bf, w_in_t, l, d_conv, tm, tn):
    t, d = x_bf.shape
    nj = d_conv // tn
    return pl.pallas_call(
        _glu_kernel,
        grid=(nj, t // tm),
        in_specs=[pl.BlockSpec((tm, d), lambda j, i: (i, 0)),
                  pl.BlockSpec((None, tn, d), lambda j, i: (l, j, 0)),
                  pl.BlockSpec((None, tn, d), lambda j, i: (l, j + nj, 0))],
        out_specs=pl.BlockSpec((tm, tn), lambda j, i: (i, j)),
        out_shape=jax.ShapeDtypeStruct((t, d_conv), F32),
        scratch_shapes=[pltpu.VMEM((tn, d), BF16), pltpu.VMEM((tn, d), BF16)],
        compiler_params=_cparams(("arbitrary", "arbitrary"), 48),
        name="glu",
    )(x_bf, w_in_t, w_in_t)


def _latent_kernel(x_ref, wq_ref, wc_ref, wk_ref, qg_ref, cg_ref, c_ref, s1_ref, s2_ref,
                   qn_ref, ckv_ref, kpe_ref, kpad_ref, wq_bf, wc_bf, wk_bf):
    @pl.when(pl.program_id(0) == 0)
    def _():
        wq_bf[...] = wq_ref[...].astype(BF16)
        wc_bf[...] = wc_ref[...].astype(BF16)
        wk_bf[...] = wk_ref[...].astype(BF16)

    x = x_ref[...]
    qn_ref[...] = _rms_norm(_dot_nt(x, wq_bf[...]), qg_ref[...]).astype(BF16)
    ckv_ref[...] = _rms_norm(_dot_nt(x, wc_bf[...]), cg_ref[...])
    kr = _dot_nt(x, wk_bf[...])
    k = _rope128(kr, c_ref[...], s1_ref[...], s2_ref[...])
    lane = lax.broadcasted_iota(jnp.int32, k.shape, 1)
    k = jnp.where(lane < ROPE_DIM, k, 0.0)
    kpe_ref[...] = k[:, :ROPE_DIM]
    kpad_ref[...] = k.astype(BF16)


def _latent_call(x_bf, w_in_t, q_norm_g, kv_norm_g, tabs, l, splits, rq, rkv, tm):
    t, d = x_bf.shape
    assert splits[0] % rq == 0 and splits[1] % rkv == 0 and splits[2] % LANES == 0
    row = lambda i: (i, 0)
    return pl.pallas_call(
        _latent_kernel,
        grid=(t // tm,),
        in_specs=[pl.BlockSpec((tm, d), row),
                  pl.BlockSpec((None, rq, d), lambda i: (l, splits[0] // rq, 0)),
                  pl.BlockSpec((None, rkv, d), lambda i: (l, splits[1] // rkv, 0)),
                  pl.BlockSpec((None, LANES, d), lambda i: (l, splits[2] // LANES, 0)),
                  pl.BlockSpec((None, 1, rq), lambda i: (l, 0, 0)),
                  pl.BlockSpec((None, 1, rkv), lambda i: (l, 0, 0)),
                  pl.BlockSpec((tm, LANES), row),
                  pl.BlockSpec((tm, LANES), row),
                  pl.BlockSpec((tm, LANES), row)],
        out_specs=[pl.BlockSpec((tm, rq), row),
                   pl.BlockSpec((tm, rkv), row),
                   pl.BlockSpec((tm, ROPE_DIM), row),
                   pl.BlockSpec((tm, LANES), row)],
        out_shape=[jax.ShapeDtypeStruct((t, rq), BF16),
                   jax.ShapeDtypeStruct((t, rkv), F32),
                   jax.ShapeDtypeStruct((t, ROPE_DIM), F32),
                   jax.ShapeDtypeStruct((t, LANES), BF16)],
        scratch_shapes=[pltpu.VMEM((rq, d), BF16), pltpu.VMEM((rkv, d), BF16), pltpu.VMEM((LANES, d), BF16)],
        compiler_params=_cparams(("arbitrary",), 48),
        name="latent",
    )(x_bf, w_in_t, w_in_t, w_in_t, q_norm_g, kv_norm_g, *tabs)


def _gates_kernel(x_ref, w_ref, o_ref):
    o_ref[...] = jax.nn.sigmoid(_dot_nt(x_ref[...], w_ref[...])).astype(BF16)


def _gates_call(x_bf, w_g_t, l, tm, tn):
    t, d = x_bf.shape
    n = w_g_t.shape[1]
    return pl.pallas_call(
        _gates_kernel,
        grid=(n // tn, t // tm),
        in_specs=[pl.BlockSpec((tm, d), lambda j, i: (i, 0)),
                  pl.BlockSpec((None, tn, d), lambda j, i: (l, j, 0))],
        out_specs=pl.BlockSpec((tm, tn), lambda j, i: (i, j)),
        out_shape=jax.ShapeDtypeStruct((t, n), BF16),
        compiler_params=_cparams(("arbitrary", "arbitrary"), 48),
        name="gates",
    )(x_bf, w_g_t)


def _conv_tail(ext_ref, y_ref, cw_ref, cb_ref, lg_ref, lb_ref, wpw_bf, gate_ref, ga_ref, ts, rows):
    c_dim = y_ref.shape[1]
    for c in range(c_dim // LANES):
        cs = slice(c * LANES, (c + 1) * LANES)

        def rbody(r, carry, cs=cs):
            r0 = pl.multiple_of(r * rows, rows)
            win = ext_ref[pl.ds(r0, rows + CONV_HALO), cs]
            acc = jnp.zeros((rows, LANES), F32)
            for k in range(CONV_WIDTH):
                o = k + CONV_OFF
                acc = acc + win[o:o + rows, :] * cw_ref[k:k + 1, cs]
            y_ref[pl.ds(r0, rows), cs] = acc
            return carry

        lax.fori_loop(0, ts // rows, rbody, 0)

    y = _layer_norm(y_ref[...] + cb_ref[...], lg_ref[...], lb_ref[...])
    z = (y * jax.nn.sigmoid(y)).astype(BF16)
    a_out = _dot(z, wpw_bf[...])
    ga_ref[...] = (gate_ref[...].astype(F32) * a_out).astype(BF16)


def _conv_prompt_kernel(u_ref, halo_ref, cw_ref, cb_ref, lg_ref, lb_ref, wpw_ref, gate_ref, ga_ref,
                        ext_ref, y_ref, wpw_bf, *, ts, rows):
    b, i = pl.program_id(0), pl.program_id(1)

    @pl.when((b == 0) & (i == 0))
    def _():
        wpw_bf[...] = wpw_ref[...].astype(BF16)

    @pl.when(i == 0)
    def _():
        ext_ref[0:CONV_HALO, :] = jnp.zeros((CONV_HALO, ext_ref.shape[1]), F32)

    @pl.when(i > 0)
    def _():
        ext_ref[0:CONV_HALO, :] = halo_ref[...]

    ext_ref[CONV_HALO:CONV_HALO + ts, :] = u_ref[...]
    _conv_tail(ext_ref, y_ref, cw_ref, cb_ref, lg_ref, lb_ref, wpw_bf, gate_ref, ga_ref, ts, rows)


def _conv_sample_kernel(u_ref, hist_ref, cw_ref, cb_ref, lg_ref, lb_ref, wpw_ref, gate_ref, ga_ref,
                        ext_ref, y_ref, wpw_bf, *, ts, rows):
    @pl.when(pl.program_id(0) == 0)
    def _():
        wpw_bf[...] = wpw_ref[...].astype(BF16)

    ext_ref[0:CONV_HALO, :] = jnp.zeros((CONV_HALO, ext_ref.shape[1]), F32)
    ext_ref[CONV_OFF:CONV_HALO, :] = hist_ref[...]
    ext_ref[CONV_HALO:CONV_HALO + ts, :] = u_ref[...]
    _conv_tail(ext_ref, y_ref, cw_ref, cb_ref, lg_ref, lb_ref, wpw_bf, gate_ref, ga_ref, ts, rows)


def _conv_weight_specs(l, c_dim, d, nargs):
    z = (lambda *a: (l, 0, 0))
    return [pl.BlockSpec((None, CONV_WIDTH, c_dim), z),
            pl.BlockSpec((None, 1, c_dim), z),
            pl.BlockSpec((None, 1, c_dim), z),
            pl.BlockSpec((None, 1, c_dim), z),
            pl.BlockSpec((None, c_dim, d), z)]


def _conv_prompt_call(u, gates, conv_w, conv_b, ln_g, ln_b, w_pw, l, nb, seq, ts, rows):
    c_dim = u.shape[1]
    d = w_pw.shape[-1]
    ns = seq // ts
    hb = ts // CONV_HALO
    tile = lambda b, i: (b * ns + i, 0)
    return pl.pallas_call(
        functools.partial(_conv_prompt_kernel, ts=ts, rows=rows),
        grid=(nb, ns),
        in_specs=[pl.BlockSpec((ts, c_dim), tile),
                  pl.BlockSpec((CONV_HALO, c_dim), lambda b, i: (jnp.maximum((b * ns + i) * hb - 1, 0), 0))]
        + _conv_weight_specs(l, c_dim, d, 2)
        + [pl.BlockSpec((ts, d), tile)],
        out_specs=pl.BlockSpec((ts, d), tile),
        out_shape=jax.ShapeDtypeStruct((nb * seq, d), BF16),
        scratch_shapes=[pltpu.VMEM((CONV_HALO + ts, c_dim), F32), pltpu.VMEM((ts, c_dim), F32),
                        pltpu.VMEM((c_dim, d), BF16)],
        compiler_params=_cparams(("arbitrary", "arbitrary"), 56),
        name="conv_prompt",
    )(u, u, conv_w, conv_b, ln_g, ln_b, w_pw, gates)


def _conv_sample_call(u, gates, state_conv, conv_w, conv_b, ln_g, ln_b, w_pw, l, row0, nb, ts, rows):
    c_dim = u.shape[1]
    d = w_pw.shape[-1]
    t0 = row0 // ts
    tile = lambda b: (t0 + b, 0)
    return pl.pallas_call(
        functools.partial(_conv_sample_kernel, ts=ts, rows=rows),
        grid=(nb,),
        in_specs=[pl.BlockSpec((ts, c_dim), tile),
                  pl.BlockSpec((None, None, CONV_WIDTH - 1, c_dim), lambda b: (l, b, 0, 0))]
        + _conv_weight_specs(l, c_dim, d, 1)
        + [pl.BlockSpec((ts, d), tile)],
        out_specs=pl.BlockSpec((ts, d), lambda b: (b, 0)),
        out_shape=jax.ShapeDtypeStruct((nb * ts, d), BF16),
        scratch_shapes=[pltpu.VMEM((CONV_HALO + ts, c_dim), F32), pltpu.VMEM((ts, c_dim), F32),
                        pltpu.VMEM((c_dim, d), BF16)],
        compiler_params=_cparams(("arbitrary",), 56),
        name="conv_sample",
    )(u, state_conv, conv_w, conv_b, ln_g, ln_b, w_pw, gates)


def _q_kernel(qn_ref, w_ref, c_ref, s1_ref, s2_ref, q_ref, *, scale):
    qn = qn_ref[...]
    c, s1, s2 = c_ref[...], s1_ref[...], s2_ref[...]
    for h in range(N_HEADS):
        qh = _dot(qn, w_ref[:, h * HEAD_PAD:(h + 1) * HEAD_PAD])
        q_ref[:, h * HEAD_PAD:h * HEAD_PAD + LANES] = (qh[:, :LANES] * scale).astype(BF16)
        q_ref[:, h * HEAD_PAD + LANES:(h + 1) * HEAD_PAD] = (_rope128(qh[:, LANES:], c, s1, s2) * scale).astype(BF16)


def _q_call(qn, w_q, tabs, l, tm, scale):
    t, r = qn.shape
    n = w_q.shape[-1]
    row = lambda i: (i, 0)
    return pl.pallas_call(
        functools.partial(_q_kernel, scale=scale),
        grid=(t // tm,),
        in_specs=[pl.BlockSpec((tm, r), row),
                  pl.BlockSpec((None, r, n), lambda i: (l, 0, 0)),
                  pl.BlockSpec((tm, LANES), row), pl.BlockSpec((tm, LANES), row), pl.BlockSpec((tm, LANES), row)],
        out_specs=pl.BlockSpec((tm, n), row),
        out_shape=jax.ShapeDtypeStruct((t, n), BF16),
        compiler_params=_cparams(("arbitrary",), 48),
        name="q_proj",
    )(qn, w_q, *tabs)


def _kv_kernel(lat_ref, kpad_ref, w_ref, k_ref, vt_ref, w_bf):
    @pl.when(pl.program_id(0) == 0)
    def _():
        w_bf[...] = w_ref[...].astype(BF16)

    lat = lat_ref[...].astype(BF16)
    kpad = kpad_ref[...]
    for h in range(N_HEADS):
        kv = _dot(lat, w_bf[:, h * HEAD_PAD:(h + 1) * HEAD_PAD])
        k_ref[:, h * HEAD_PAD:h * HEAD_PAD + LANES] = kv[:, :LANES].astype(BF16)
        k_ref[:, h * HEAD_PAD + LANES:(h + 1) * HEAD_PAD] = kpad
        vt_ref[h * V_HEAD_DIM:(h + 1) * V_HEAD_DIM, :] = kv[:, LANES:].T.astype(BF16)


def _kv_call(ckv, kpad, w_kv_b, l, rows, tm):
    r = ckv.shape[1]
    n = w_kv_b.shape[-1]
    row = lambda i: (i, 0)
    return pl.pallas_call(
        _kv_kernel,
        grid=(rows // tm,),
        in_specs=[pl.BlockSpec((tm, r), row),
                  pl.BlockSpec((tm, LANES), row),
                  pl.BlockSpec((None, r, n), lambda i: (l, 0, 0))],
        out_specs=[pl.BlockSpec((tm, N_HEADS * HEAD_PAD), row),
                   pl.BlockSpec((N_HEADS * V_HEAD_DIM, tm), lambda i: (0, i))],
        out_shape=[jax.ShapeDtypeStruct((rows, N_HEADS * HEAD_PAD), BF16),
                   jax.ShapeDtypeStruct((N_HEADS * V_HEAD_DIM, rows), BF16)],
        scratch_shapes=[pltpu.VMEM((r, n), BF16)],
        compiler_params=_cparams(("arbitrary",), 48),
        name="kv_proj",
    )(ckv, kpad, w_kv_b)


def _attn_prompt_kernel(q_ref, k_ref, vt_ref, ga_ref, gb_ref, o_ref, acc_ref, m_ref, l_ref, *, seq, tq):
    def step(q, q0, k0, masked):
        kt = k_ref[pl.ds(k0, tq), :]
        s = _dot_nt(kt, q)
        if masked:
            kk = lax.broadcasted_iota(jnp.int32, s.shape, 0) // CHUNK
            qq = lax.broadcasted_iota(jnp.int32, s.shape, 1) // CHUNK
            s = jnp.where(kk <= qq, s, NEG_BIG)
        m_old = m_ref[...]
        m_new = jnp.maximum(m_old, jnp.max(s, axis=0, keepdims=True))
        alpha = jnp.exp(m_old - m_new)
        p = jnp.exp(s - m_new)
        l_ref[...] = alpha * l_ref[...] + jnp.sum(p, axis=0, keepdims=True)
        pv = _dot(vt_ref[:, pl.ds(k0, tq)], p.astype(BF16))
        acc_ref[...] = alpha * acc_ref[...] + pv
        m_ref[...] = m_new

    def q_body(qi, carry):
        q0 = pl.multiple_of(qi * tq, tq)
        q = q_ref[pl.ds(q0, tq), :]
        m_ref[...] = jnp.full(m_ref.shape, NEG_BIG, F32)
        l_ref[...] = jnp.zeros(l_ref.shape, F32)
        acc_ref[...] = jnp.zeros(acc_ref.shape, F32)

        def k_body(j, c):
            step(q, q0, pl.multiple_of(j * tq, tq), False)
            return c

        lax.fori_loop(0, qi, k_body, 0)
        step(q, q0, q0, True)
        o = (acc_ref[...] / l_ref[...]).T
        rows = pl.ds(q0, tq)
        o_ref[rows, :] = (ga_ref[rows, :].astype(F32) + gb_ref[rows, :].astype(F32) * o).astype(BF16)
        return carry

    lax.fori_loop(0, seq // tq, q_body, 0)


def _attn_prompt_call(q, k, vt, ga, gates, nb, seq, tq):
    d = N_HEADS * V_HEAD_DIM
    bh = lambda b, h: (b, h)
    return pl.pallas_call(
        functools.partial(_attn_prompt_kernel, seq=seq, tq=tq),
        grid=(nb, N_HEADS),
        in_specs=[pl.BlockSpec((seq, HEAD_PAD), bh),
                  pl.BlockSpec((seq, HEAD_PAD), bh),
                  pl.BlockSpec((V_HEAD_DIM, seq), lambda b, h: (h, b)),
                  pl.BlockSpec((seq, V_HEAD_DIM), bh),
                  pl.BlockSpec((seq, V_HEAD_DIM), lambda b, h: (b, N_HEADS + h))],
        out_specs=pl.BlockSpec((seq, V_HEAD_DIM), bh),
        out_shape=jax.ShapeDtypeStruct((nb * seq, d), BF16),
        scratch_shapes=[pltpu.VMEM((V_HEAD_DIM, tq), F32), pltpu.VMEM((1, tq), F32), pltpu.VMEM((1, tq), F32)],
        compiler_params=_cparams(("arbitrary", "arbitrary"), 48),
        name="attn_prompt",
    )(q, k, vt, ga, gates)


def _attn_sample_kernel(q_ref, latp_ref, kpep_ref, latn_ref, kpen_ref, w_ref, ga_ref, gb_ref, o_ref,
                        w_bf, qlat_ref, qpe_ref, kpp_ref, kpn_ref, olat_ref, *, ds):
    @pl.when(pl.program_id(0) == 0)
    def _():
        w_bf[...] = w_ref[...].astype(BF16)
        kpp_ref[...] = jnp.zeros(kpp_ref.shape, BF16)
        kpn_ref[...] = jnp.zeros(kpn_ref.shape, BF16)

    kpp_ref[:ROPE_DIM, :] = kpep_ref[...].astype(BF16)
    kpn_ref[:, :ROPE_DIM] = kpen_ref[...].astype(BF16)
    for h in range(N_HEADS):
        rows = slice(h * ds, (h + 1) * ds)
        qn = q_ref[:, h * HEAD_PAD:h * HEAD_PAD + LANES]
        qlat_ref[rows, :] = _dot_nt(qn, w_bf[:, h * HEAD_PAD:h * HEAD_PAD + LANES]).astype(BF16)
        qpe_ref[rows, :] = q_ref[:, h * HEAD_PAD + LANES:(h + 1) * HEAD_PAD]

    lat_p = latp_ref[...].astype(BF16)
    lat_n = latn_ref[...].astype(BF16)
    qlat, qpe = qlat_ref[...], qpe_ref[...]
    s_p = _dot_nt(qlat, lat_p) + _dot(qpe, kpp_ref[...])
    s_n = _dot_nt(qlat, lat_n) + _dot_nt(qpe, kpn_ref[...])
    m = jnp.maximum(jnp.max(s_p, axis=1, keepdims=True), jnp.max(s_n, axis=1, keepdims=True))
    p_p = jnp.exp(s_p - m)
    p_n = jnp.exp(s_n - m)
    den = jnp.sum(p_p, axis=1, keepdims=True) + jnp.sum(p_n, axis=1, keepdims=True)
    o_lat = _dot(p_p.astype(BF16), lat_p) + _dot(p_n.astype(BF16), lat_n)
    olat_ref[...] = (o_lat / den).astype(BF16)
    for h in range(N_HEADS):
        cols = slice(h * V_HEAD_DIM, (h + 1) * V_HEAD_DIM)
        o = _dot(olat_ref[h * ds:(h + 1) * ds, :], w_bf[:, h * HEAD_PAD + LANES:(h + 1) * HEAD_PAD])
        o_ref[:, cols] = (ga_ref[:, cols].astype(F32) + gb_ref[:, cols].astype(F32) * o).astype(BF16)


def _attn_sample_call(q, cache_lat, cache_pe, ckv, kpe, w_kv_b, ga_s, gates, l, row0, nb, ds):
    past, r = cache_lat.shape[2], cache_lat.shape[3]
    d = N_HEADS * V_HEAD_DIM
    t0 = row0 // ds
    tile = lambda b: (t0 + b, 0)
    return pl.pallas_call(
        functools.partial(_attn_sample_kernel, ds=ds),
        grid=(nb,),
        in_specs=[pl.BlockSpec((ds, N_HEADS * HEAD_PAD), tile),
                  pl.BlockSpec((None, None, past, r), lambda b: (l, b, 0, 0)),
                  pl.BlockSpec((None, None, ROPE_DIM, past), lambda b: (l, b, 0, 0)),
                  pl.BlockSpec((ds, r), tile),
                  pl.BlockSpec((ds, ROPE_DIM), tile),
                  pl.BlockSpec((None, r, N_HEADS * HEAD_PAD), lambda b: (l, 0, 0)),
                  pl.BlockSpec((ds, d), lambda b: (b, 0)),
                  pl.BlockSpec((ds, d), lambda b: (t0 + b, 1))],
        out_specs=pl.BlockSpec((ds, d), lambda b: (b, 0)),
        out_shape=jax.ShapeDtypeStruct((nb * ds, d), BF16),
        scratch_shapes=[pltpu.VMEM((r, N_HEADS * HEAD_PAD), BF16),
                        pltpu.VMEM((N_HEADS * ds, r), BF16),
                        pltpu.VMEM((N_HEADS * ds, LANES), BF16),
                        pltpu.VMEM((LANES, past), BF16),
                        pltpu.VMEM((ds, LANES), BF16),
                        pltpu.VMEM((N_HEADS * ds, r), BF16)],
        compiler_params=_cparams(("arbitrary",), 56),
        name="attn_sample",
    )(q, cache_lat, cache_pe, ckv, kpe, w_kv_b, ga_s, gates)


def _route(scores, bias):
    sel = [s + b for s, b in zip(scores, bias)]
    n = EXPERTS_PER_GROUP
    grp = []
    for g in range(N_EXPERT_GROUPS):
        v = sel[g * n:(g + 1) * n]
        best = None
        for i in range(n):
            for j in range(i + 1, n):
                pair = v[i] + v[j]
                best = pair if best is None else jnp.maximum(best, pair)
        grp.append(best)
    g_idx = jnp.zeros_like(grp[0], dtype=jnp.int32)
    g_best = grp[0]
    for g in range(1, N_EXPERT_GROUPS):
        better = grp[g] > g_best
        g_idx = jnp.where(better, g, g_idx)
        g_best = jnp.where(better, grp[g], g_best)

    def pick(rows_by_group):
        out = rows_by_group[0]
        for g in range(1, N_EXPERT_GROUPS):
            out = jnp.where(g_idx == g, rows_by_group[g], out)
        return out

    in_sel = [pick([sel[g * n + j] for g in range(N_EXPERT_GROUPS)]) for j in range(n)]
    in_sc = [pick([scores[g * n + j] for g in range(N_EXPERT_GROUPS)]) for j in range(n)]

    def argmax_first(vals, excluded):
        idx = None
        best = None
        for j in range(n):
            v = vals[j] if excluded is None else jnp.where(excluded == j, -jnp.inf, vals[j])
            if best is None:
                best, idx = v, jnp.zeros_like(g_idx)
            else:
                better = v > best
                idx = jnp.where(better, j, idx)
                best = jnp.where(better, v, best)
        return idx

    l0 = argmax_first(in_sel, None)
    l1 = argmax_first(in_sel, l0)

    def take(vals, idx):
        out = vals[0]
        for j in range(1, n):
            out = jnp.where(idx == j, vals[j], out)
        return out

    w0, w1 = take(in_sc, l0), take(in_sc, l1)
    tot = w0 + w1
    return g_idx * n + l0, g_idx * n + l1, w0 / tot, w1 / tot


def _wo_kernel(mixp_ref, mixs_ref, x_ref, w_ref, g_ref, b_ref, wr_ref, br_ref, x1_ref, gate_ref, eidx_ref, w_bf,
               *, alpha, n_prompt_tiles):
    @pl.when(pl.program_id(0) == 0)
    def _():
        w_bf[...] = w_ref[...].astype(BF16)

    mix = jnp.where(pl.program_id(0) < n_prompt_tiles, mixp_ref[...], mixs_ref[...])
    y = alpha * x_ref[...] + _dot(mix, w_bf[...])
    x1 = _layer_norm(y, g_ref[...], b_ref[...])
    x1_ref[...] = x1
    logits =lax.dot_general(wr_ref[...], x1, (((1,), (1,)), ((), ())), preferred_element_type=F32,
                             precision=lax.Precision.HIGHEST)
    sc = jax.nn.sigmoid(logits)
    br = br_ref[...]
    e0, e1, g0, g1 = _route([sc[e:e + 1, :] for e in range(N_EXPERTS)],
                            [br[e:e + 1, :] for e in range(N_EXPERTS)])
    rows = lax.broadcasted_iota(jnp.int32, (LANES, sc.shape[1]), 0)
    gate_ref[...] = jnp.where(rows == 0, g0, jnp.where(rows == 1, g1, 0.0)).T
    rows8 = lax.broadcasted_iota(jnp.int32, eidx_ref.shape, 0)
    eidx_ref[...] = jnp.where(rows8 == 0, e0, jnp.where(rows8 == 1, e1, 0))


def _wo_call(mixed_p, mixed_s, x, w_o, ln_g, ln_b, wr_t, br, l, tm, alpha):
    t, d = x.shape
    n_p = mixed_p.shape[0] // tm
    row = lambda i: (i, 0)
    vec = pl.BlockSpec((None, 1, d), lambda i: (l, 0, 0))
    return pl.pallas_call(
        functools.partial(_wo_kernel, alpha=alpha, n_prompt_tiles=n_p),
        grid=(t // tm,),
        in_specs=[pl.BlockSpec((tm, d), lambda i: (jnp.minimum(i, n_p - 1), 0)),
                  pl.BlockSpec((tm, d), lambda i: (jnp.maximum(i - n_p, 0), 0)),
                  pl.BlockSpec((tm, d), row),
                  pl.BlockSpec((None, d, d), lambda i: (l, 0, 0), pipeline_mode=pl.Buffered(1)),
                  vec, vec,
                  pl.BlockSpec((N_EXPERTS, d), lambda i: (0, 0)),
                  pl.BlockSpec((N_EXPERTS, 1), lambda i: (0, 0))],
        out_specs=[pl.BlockSpec((tm, d), row), pl.BlockSpec((tm, LANES), row),
                   pl.BlockSpec((SUBLANES, tm), lambda i: (0, i))],
        out_shape=[jax.ShapeDtypeStruct((t, d), F32), jax.ShapeDtypeStruct((t, LANES), F32),
                   jax.ShapeDtypeStruct((SUBLANES, t), jnp.int32)],
        scratch_shapes=[pltpu.VMEM((d, d), BF16)],
        compiler_params=_cparams(("arbitrary",), 56),
        name="wo_ln_router",
    )(mixed_p, mixed_s, x, w_o, ln_g, ln_b, wr_t, br)


def _dispatch_kernel(eidx_ref, pos_ref, meta_ref):
    t = eidx_ref.shape[1]
    e0, e1 = eidx_ref[0:1, :], eidx_ref[1:2, :]
    rows = lax.broadcasted_iota(jnp.int32, (N_EXPERTS, t), 0)
    hit0, hit1 = rows == e0, rows == e1
    oh = jnp.where(hit0 | hit1, 1.0, 0.0)
    cnt = jnp.sum(oh, axis=1, keepdims=True)
    padded = jnp.floor((cnt + (MOE_TILE - 1)) * (1.0 / MOE_TILE)) * MOE_TILE
    erow = lax.broadcasted_iota(jnp.int32, (N_EXPERTS, 1), 0)
    off = jnp.zeros((N_EXPERTS, 1), F32)
    run = jnp.zeros((1, 1), F32)
    for e in range(N_EXPERTS):
        off = jnp.where(erow == e, run, off)
        run = run + padded[e:e + 1, :]
    c = PREFIX_CHUNK
    before = jnp.where(lax.broadcasted_iota(jnp.int32, (c, c), 0) < lax.broadcasted_iota(jnp.int32, (c, c), 1),
                       1.0, 0.0).astype(BF16)
    pos_ref[...] = jnp.zeros(pos_ref.shape, jnp.int32)
    carry = off
    for j in range(t // c):
        cs = slice(j * c, (j + 1) * c)
        ohc = oh[:, cs]
        slot = _dot(ohc.astype(BF16), before) + carry
        pos_ref[0:1, cs] = jnp.sum(jnp.where(hit0[:, cs], slot, 0.0), axis=0, keepdims=True).astype(jnp.int32)
        pos_ref[1:2, cs] = jnp.sum(jnp.where(hit1[:, cs], slot, 0.0), axis=0, keepdims=True).astype(jnp.int32)
        carry = carry + jnp.sum(ohc, axis=1, keepdims=True)
    start = lax.broadcasted_iota(jnp.int32, (N_EXPERTS, LANES), 1).astype(F32) * MOE_TILE
    te = jnp.minimum(jnp.sum(jnp.where(off + padded <= start, 1.0, 0.0), axis=0, keepdims=True), N_EXPERTS - 1.0)
    mine = lax.broadcasted_iota(jnp.int32, (N_EXPERTS, LANES), 0).astype(F32) == te
    end_valid = jnp.sum(jnp.where(mine, off + cnt, 0.0), axis=0, keepdims=True)
    nvalid = jnp.clip(end_valid - start[0:1, :], 0.0, MOE_TILE)
    r8 = lax.broadcasted_iota(jnp.int32, meta_ref.shape, 0)
    meta = jnp.where(r8 == 0, te, jnp.where(r8 == 1, nvalid, jnp.where(r8 == 2, run * (1.0 / MOE_TILE), 0.0)))
    meta_ref[...] = meta.astype(jnp.int32)


def _dispatch_call(eidx):
    t = eidx.shape[1]
    assert t % PREFIX_CHUNK == 0
    return pl.pallas_call(
        _dispatch_kernel,
        out_shape=[jax.ShapeDtypeStruct((SUBLANES, t), jnp.int32), jax.ShapeDtypeStruct((SUBLANES, LANES), jnp.int32)],
        compiler_params=pltpu.CompilerParams(vmem_limit_bytes=32 * MIB),
        name="moe_dispatch",
    )(eidx)


def _row_copy(src, i, dst, j, sem):
    return pltpu.make_async_copy(src.at[pl.ds(i, 1), :], dst.at[pl.ds(j, 1), :], sem)


def _scatter_kernel(p0_ref, p1_ref, x_hbm, init_hbm, xs_hbm, sem, *, rows):
    del init_hbm
    base = pl.program_id(0) * rows

    def body(r, carry):
        t = base + r
        _row_copy(x_hbm, t, xs_hbm, p0_ref[t], sem).start()
        _row_copy(x_hbm, t, xs_hbm, p1_ref[t], sem).start()
        return carry

    lax.fori_loop(0, rows, body, 0, unroll=8)
    pltpu.make_async_copy(x_hbm.at[pl.ds(0, 2 * rows), :], xs_hbm.at[pl.ds(0, 2 * rows), :], sem).wait()


def _scatter_call(pos0, pos1, x1, init, rows):
    t, d = x1.shape
    return pl.pallas_call(
        functools.partial(_scatter_kernel, rows=rows),
        grid_spec=pltpu.PrefetchScalarGridSpec(
            num_scalar_prefetch=2, grid=(t // rows,),
            in_specs=[pl.BlockSpec(memory_space=pl.ANY), pl.BlockSpec(memory_space=pl.ANY)],
            out_specs=pl.BlockSpec(memory_space=pl.ANY),
            scratch_shapes=[pltpu.SemaphoreType.DMA(())]),
        out_shape=jax.ShapeDtypeStruct(init.shape, F32),
        input_output_aliases={3: 0},
        compiler_params=pltpu.CompilerParams(dimension_semantics=("arbitrary",), disable_bounds_checks=True),
        name="moe_scatter",
    )(pos0, pos1, x1, init)


def _experts_kernel(te_ref, nv_ref, nu_ref, xs_ref, wg_ref, wu_ref, wd_ref, y_ref, wg_bf, wu_bf, wd_bf):
    i = pl.program_id(0)
    nv = nv_ref[i]
    fresh = (i == 0) | (te_ref[i] != te_ref[jnp.maximum(i - 1, 0)])

    @pl.when(fresh & (nv > 0))
    def _():
        wg_bf[...] = wg_ref[...].astype(BF16)
        wu_bf[...] = wu_ref[...].astype(BF16)
        wd_bf[...] = wd_ref[...].astype(BF16)

    @pl.when(nv > 0)
    def _():
        row = lax.broadcasted_iota(jnp.int32, xs_ref.shape, 0)
        x = jnp.where(row < nv, xs_ref[...], 0.0).astype(BF16)
        hg = _dot(x, wg_bf[...])
        hu = _dot(x, wu_bf[...])
        h = hg * jax.nn.sigmoid(hg) * hu
        y_ref[...] = _dot(h.astype(BF16), wd_bf[...])

    @pl.when(nv == 0)
    def _():
        y_ref[...] = jnp.zeros(y_ref.shape, F32)


def _experts_call(te, nv, nu, xs, w_gate, w_up, w_down, l):
    n_slots, d = xs.shape
    f = w_gate.shape[-1]
    n_tiles = n_slots // MOE_TILE
    tile = lambda i, te, nv, nu: (jnp.minimum(i, nu[0] - 1), 0)
    return pl.pallas_call(
        _experts_kernel,
        grid_spec=pltpu.PrefetchScalarGridSpec(
            num_scalar_prefetch=3, grid=(n_tiles,),
            in_specs=[pl.BlockSpec((MOE_TILE, d), tile),
                      pl.BlockSpec((None, None, d, f), lambda i, te, nv, nu: (l, te[i], 0, 0)),
                      pl.BlockSpec((None, None, d, f), lambda i, te, nv, nu: (l, te[i], 0, 0)),
                      pl.BlockSpec((None, None, f, d), lambda i, te, nv, nu: (l, te[i], 0, 0))],
            out_specs=pl.BlockSpec((MOE_TILE, d), lambda i, te, nv, nu: (i, 0)),
            scratch_shapes=[pltpu.VMEM((d, f), BF16), pltpu.VMEM((d, f), BF16), pltpu.VMEM((f, d), BF16)]),
        out_shape=jax.ShapeDtypeStruct((n_slots, d), F32),
        compiler_params=_cparams(("arbitrary",), 56),
        name="moe_experts",
    )(te, nv, nu, xs, w_gate, w_up, w_down)


def _combine_kernel(p0_ref, p1_ref, y_hbm, x1_ref, gate_ref, g_ref, b_ref, x2_ref, x2bf_ref, buf, sems, *, tm, alpha):
    i = pl.program_id(0)
    slot = i % 2

    def issue(tile, s):
        base = tile * tm

        def body(r, carry):
            t = base + r
            _row_copy(y_hbm, p0_ref[t], buf.at[s, 0], r, sems.at[s]).start()
            _row_copy(y_hbm, p1_ref[t], buf.at[s, 1], r, sems.at[s]).start()
            return carry

        lax.fori_loop(0, tm, body, 0, unroll=8)

    @pl.when(i == 0)
    def _():
        issue(0, 0)

    @pl.when(i + 1 < pl.num_programs(0))
    def _():
        issue(i + 1, 1 - slot)

    for k in range(2):
        pltpu.make_async_copy(y_hbm.at[pl.ds(0, tm), :], buf.at[slot, k], sems.at[slot]).wait()
    gate = gate_ref[...]
    moe = gate[:, 0:1] * buf[slot, 0] + gate[:, 1:2] * buf[slot, 1]
    x2 = _layer_norm(alpha * x1_ref[...] + moe, g_ref[...], b_ref[...])
    x2_ref[...] = x2
    x2bf_ref[...] = x2.astype(BF16)


def _combine_call(pos0, pos1, y, x1, gate, ln_g, ln_b, l, tm, alpha):
    t, d = x1.shape
    row = lambda i, p0, p1: (i, 0)
    vec = pl.BlockSpec((None, 1, d), lambda i, p0, p1: (l, 0, 0))
    return pl.pallas_call(
        functools.partial(_combine_kernel, tm=tm, alpha=alpha),
        grid_spec=pltpu.PrefetchScalarGridSpec(
            num_scalar_prefetch=2, grid=(t // tm,),
            in_specs=[pl.BlockSpec(memory_space=pl.ANY),
                      pl.BlockSpec((tm, d), row), pl.BlockSpec((tm, LANES), row), vec, vec],
            out_specs=[pl.BlockSpec((tm, d), row), pl.BlockSpec((tm, d), row)],
            scratch_shapes=[pltpu.VMEM((2, 2, tm, d), F32), pltpu.SemaphoreType.DMA((2,))]),
        out_shape=[jax.ShapeDtypeStruct((t, d), F32), jax.ShapeDtypeStruct((t, d), BF16)],
        compiler_params=pltpu.CompilerParams(dimension_semantics=("arbitrary",), vmem_limit_bytes=48 * MIB,
                                             disable_bounds_checks=True),
        name="moe_combine_ln",
    )(pos0, pos1, y, x1, gate, ln_g, ln_b)


def _rope_tables(pos):
    half = ROPE_DIM // 2
    inv = jnp.float32(ROPE_THETA) ** (-jnp.arange(half, dtype=F32) / half)
    ang = pos.astype(F32)[:, None] * inv[None, :]
    cos, sin = jnp.cos(ang), jnp.sin(ang)
    z = jnp.zeros_like(cos)
    return (jnp.concatenate([cos, cos, z, z], 1),
            jnp.concatenate([-sin, z, z, z], 1),
            jnp.concatenate([z, sin, z, z], 1))


def kernel(x_prompt, x_sample, cache_kv_latent, cache_k_rope, state_conv, w_in, q_norm_g, w_q_b, kv_norm_g, w_kv_b,
           conv_w, conv_b, conv_ln_g, conv_ln_b, w_conv_pw, w_o, ln1_g, ln1_b, w_gate, w_up, w_down, ln2_g, ln2_b,
           w_router, b_router):
    nb, seq, d = x_prompt.shape
    db, ds, _ = x_sample.shape
    depth = w_in.shape[0]
    past = cache_kv_latent.shape[2]
    d_conv = conv_w.shape[-1]
    rq, rkv = q_norm_g.shape[-1], kv_norm_g.shape[-1]
    tp, tsmp = nb * seq, db * ds
    t_all = tp + tsmp
    splits = (2 * d_conv, 2 * d_conv + rq, 2 * d_conv + rq + rkv, 2 * d_conv + rq + rkv + ROPE_DIM)
    alpha = (2.0 * depth) ** 0.25
    scale = 1.0 / math.sqrt(QK_NOPE_DIM + ROPE_DIM)
    tm = 512
    tq = 512
    n_tiles = 2 * t_all // MOE_TILE + N_EXPERTS
    assert t_all % MOE_TILE == 0 and n_tiles <= LANES
    assert d == N_HEADS * V_HEAD_DIM and w_kv_b.shape[-1] == N_HEADS * HEAD_PAD
    assert tp % tm == 0 and tsmp % tm == 0 and seq % tq == 0 and tq % CHUNK == 0
    assert past % CHUNK == 0 and ds <= CHUNK and ds >= CONV_WIDTH - 1

    pos = jnp.concatenate([jnp.tile(jnp.arange(seq, dtype=jnp.int32), nb),
                           jnp.tile(past + jnp.arange(ds, dtype=jnp.int32), db)])
    tabs = _rope_tables(pos)

    w_in_t = jnp.swapaxes(w_in, 1, 2)
    w_g_t = w_in_t[:, splits[3]:, :].astype(BF16)
    cache_pe_t = jnp.swapaxes(cache_k_rope, 2, 3)
    wq = w_q_b.reshape(depth, rq, N_HEADS, QK_NOPE_DIM + ROPE_DIM)
    wq = jnp.pad(wq, ((0, 0), (0, 0), (0, 0), (0, HEAD_PAD - QK_NOPE_DIM - ROPE_DIM)))
    wq = wq.reshape(depth, rq, N_HEADS * HEAD_PAD).astype(BF16)
    vec3 = lambda a: a.reshape(depth, 1, a.shape[-1])
    q_norm_g3, kv_norm_g3 = vec3(q_norm_g), vec3(kv_norm_g)
    conv_b3, conv_ln_g3, conv_ln_b3 = vec3(conv_b), vec3(conv_ln_g), vec3(conv_ln_b)
    ln1_g3, ln1_b3, ln2_g3, ln2_b3 = vec3(ln1_g), vec3(ln1_b), vec3(ln2_g), vec3(ln2_b)
    wr_t = w_router.T
    br = b_router.reshape(N_EXPERTS, 1)

    x = jnp.concatenate([x_prompt.reshape(tp, d), x_sample.reshape(tsmp, d)], axis=0)
    x_bf = x.astype(BF16)

    spare = jnp.zeros((n_tiles * MOE_TILE, d), F32)
    keep = CONV_WIDTH - 1
    outs = [[] for _ in range(6)]
    for l in range(depth):
        u = _glu_call(x_bf, w_in_t, l, d_conv, tm, 512)
        qn, ckv, kpe, kpad = _latent_call(x_bf, w_in_t, q_norm_g3, kv_norm_g3, tabs, l, splits, rq, rkv, tm)
        gates = _gates_call(x_bf, w_g_t, l, tm, 1024)
        ga_p = _conv_prompt_call(u, gates, conv_w, conv_b3, conv_ln_g3, conv_ln_b3, w_conv_pw, l, nb, seq, 512, 64)
        ga_s = _conv_sample_call(u, gates, state_conv, conv_w, conv_b3, conv_ln_g3, conv_ln_b3, w_conv_pw,
                                 l, tp, db, ds, ds)
        q = _q_call(qn, wq, tabs, l, tm, scale)
        k, vt = _kv_call(ckv, kpad, w_kv_b, l, tp, tm)
        mixed_p = _attn_prompt_call(q, k, vt, ga_p, gates, nb, seq, tq)
        mixed_s = _attn_sample_call(q, cache_kv_latent, cache_pe_t, ckv, kpe, w_kv_b, ga_s, gates, l, tp, db, ds)
        x1, gate, eidx = _wo_call(mixed_p, mixed_s, x, w_o, ln1_g3, ln1_b3, wr_t, br, l, tm, alpha)
        pos_rows, meta = _dispatch_call(eidx)
        pos0, pos1 = pos_rows[0], pos_rows[1]
        te, nv, nu = meta[0, :n_tiles], meta[1, :n_tiles], meta[2, :1]
        xs = _scatter_call(pos0, pos1, x1, spare, tm)
        y = _experts_call(te, nv, nu, xs, w_gate, w_up, w_down, l)
        x, x_bf = _combine_call(pos0, pos1, y, x1, gate, ln2_g3, ln2_b3, l, MOE_TILE, alpha)
        spare = y
        outs[0].append(ckv[:tp].reshape(nb, seq, rkv))
        outs[1].append(kpe[:tp].reshape(nb, seq, ROPE_DIM))
        outs[2].append(u[:tp].reshape(nb, seq, d_conv)[:, seq - keep:])
        outs[3].append(ckv[tp:].reshape(db, ds, rkv))
        outs[4].append(kpe[tp:].reshape(db, ds, ROPE_DIM))
        outs[5].append(u[tp:].reshape(db, ds, d_conv)[:, ds - keep:])

    return (x[:tp].reshape(nb, seq, d), x[tp:].reshape(db, ds, d)) + tuple(jnp.stack(o) for o in outs)
```

```python
import functools
import math

import jax
import jax.numpy as jnp
from jax import lax
from jax.experimental import pallas as pl
from jax.experimental.pallas import tpu as pltpu

F32 = jnp.float32
BF16 = jnp.bfloat16

CHUNK = 64
CONV_WIDTH = 31
N_HEADS = 16
QK_NOPE_DIM = 128
ROPE_DIM = 64
V_HEAD_DIM = 128
ROPE_THETA = 10000.0
N_EXPERTS = 16
N_EXPERT_GROUPS = 4
EXPERTS_PER_GROUP = N_EXPERTS // N_EXPERT_GROUPS
LN_EPS = 1e-5
RMS_EPS = 1e-6

LANES = 128
SUBLANES = 8
MOE_TILE = 256
PREFIX_CHUNK = 512
HEAD_PAD = 256
CONV_HALO = 32
CONV_OFF = CONV_HALO - (CONV_WIDTH - 1)
NEG_BIG = -1e30
MIB = 1024 * 1024


def _cparams(sem, vmem_mib):
    return pltpu.CompilerParams(dimension_semantics=sem, vmem_limit_bytes=vmem_mib * MIB)


def _dot(a, b):
    return jnp.dot(a, b, preferred_element_type=F32)


def _dot_nt(a, b):
    return lax.dot_general(a, b, (((1,), (1,)), ((), ())), preferred_element_type=F32)


def _layer_norm(y, g, b):
    mu = jnp.mean(y, axis=-1, keepdims=True)
    d = y - mu
    var = jnp.mean(d * d, axis=-1, keepdims=True)
    return d * lax.rsqrt(var + LN_EPS) * g + b


def _rms_norm(y, g):
    return y * lax.rsqrt(jnp.mean(y * y, axis=-1, keepdims=True) + RMS_EPS) * g


def _rope128(v, c, s1, s2):
    return v * c + pltpu.roll(v, 96, axis=1) * s1 + pltpu.roll(v, 32, axis=1) * s2


def _glu_kernel(x_ref, wa_ref, wg_ref, u_ref, wa_bf, wg_bf):
    @pl.when(pl.program_id(1) == 0)
    def _():
        wa_bf[...] = wa_ref[...].astype(BF16)
        wg_bf[...] = wg_ref[...].astype(BF16)

    x = x_ref[...]
    a = _dot_nt(x, wa_bf[...])
    g = _dot_nt(x, wg_bf[...])
    u_ref[...] = a * jax.nn.sigmoid(g)


def _glu_call(x_bf, w_in_t, l, d_conv, tm, tn):
    t, d = x_bf.shape
    nj = d_conv // tn
    return pl.pallas_call(
        _glu_kernel,
        grid=(nj, t // tm),
        in_specs=[pl.BlockSpec((tm, d), lambda j, i: (i, 0)),
                  pl.BlockSpec((None, tn, d), lambda j, i: (l, j, 0)),
                  pl.BlockSpec((None, tn, d), lambda j, i: (l, j + nj, 0))],
        out_specs=pl.BlockSpec((tm, tn), lambda j, i: (i, j)),
        out_shape=jax.ShapeDtypeStruct((t, d_conv), F32),
        scratch_shapes=[pltpu.VMEM((tn, d), BF16), pltpu.VMEM((tn, d), BF16)],
        compiler_params=_cparams(("arbitrary", "arbitrary"), 48),
        name="glu",
    )(x_bf, w_in_t, w_in_t)


def _latent_kernel(x_ref, wq_ref, wc_ref, wk_ref, qg_ref, cg_ref, c_ref, s1_ref, s2_ref, kmask_ref,
                   qn_ref, ckv_ref, kpe_ref, kpad_ref, wq_bf, wc_bf, wk_bf):
    @pl.when(pl.program_id(0) == 0)
    def _():
        wq_bf[...] = wq_ref[...].astype(BF16)
        wc_bf[...] = wc_ref[...].astype(BF16)
        wk_bf[...] = wk_ref[...].astype(BF16)

    x = x_ref[...]
    qn_ref[...] = _rms_norm(_dot_nt(x, wq_bf[...]), qg_ref[...]).astype(BF16)
    ckv_ref[...] = _rms_norm(_dot_nt(x, wc_bf[...]), cg_ref[...])
    kr = _dot_nt(x, wk_bf[...])
    k = _rope128(kr, c_ref[...], s1_ref[...], s2_ref[...])
    lane = lax.broadcasted_iota(jnp.int32, k.shape, 1)
    k = jnp.where(lane < ROPE_DIM, k, 0.0)
    kpe_ref[...] = k[:, :ROPE_DIM]
    kpad_ref[...] = (k + kmask_ref[...]).astype(BF16)


def _latent_call(x_bf, w_in_t, q_norm_g, kv_norm_g, tabs, l, splits, rq, rkv, tm):
    t, d = x_bf.shape
    assert splits[0] % rq == 0 and splits[1] % rkv == 0 and splits[2] % LANES == 0
    row = lambda i: (i, 0)
    return pl.pallas_call(
        _latent_kernel,
        grid=(t // tm,),
        in_specs=[pl.BlockSpec((tm, d), row),
                  pl.BlockSpec((None, rq, d), lambda i: (l, splits[0] // rq, 0)),
                  pl.BlockSpec((None, rkv, d), lambda i: (l, splits[1] // rkv, 0)),
                  pl.BlockSpec((None, LANES, d), lambda i: (l, splits[2] // LANES, 0)),
                  pl.BlockSpec((None, 1, rq), lambda i: (l, 0, 0)),
                  pl.BlockSpec((None, 1, rkv), lambda i: (l, 0, 0)),
                  pl.BlockSpec((tm, LANES), row),
                  pl.BlockSpec((tm, LANES), row),
                  pl.BlockSpec((tm, LANES), row),
                  pl.BlockSpec((tm, LANES), row)],
        out_specs=[pl.BlockSpec((tm, rq), row),
                   pl.BlockSpec((tm, rkv), row),
                   pl.BlockSpec((tm, ROPE_DIM), row),
                   pl.BlockSpec((tm, LANES), row)],
        out_shape=[jax.ShapeDtypeStruct((t, rq), BF16),
                   jax.ShapeDtypeStruct((t, rkv), F32),
                   jax.ShapeDtypeStruct((t, ROPE_DIM), F32),
                   jax.ShapeDtypeStruct((t, LANES), BF16)],
        scratch_shapes=[pltpu.VMEM((rq, d), BF16), pltpu.VMEM((rkv, d), BF16), pltpu.VMEM((LANES, d), BF16)],
        compiler_params=_cparams(("arbitrary",), 48),
        name="latent",
    )(x_bf, w_in_t, w_in_t, w_in_t, q_norm_g, kv_norm_g, *tabs)


def _gates_kernel(x_ref, w_ref, o_ref):
    o_ref[...] = jax.nn.sigmoid(_dot_nt(x_ref[...], w_ref[...])).astype(BF16)


def _gates_call(x_bf, w_g_t, l, tm, tn):
    t, d = x_bf.shape
    n = w_g_t.shape[1]
    return pl.pallas_call(
        _gates_kernel,
        grid=(n // tn, t // tm),
        in_specs=[pl.BlockSpec((tm, d), lambda j, i: (i, 0)),
                  pl.BlockSpec((None, tn, d), lambda j, i: (l, j, 0))],
        out_specs=pl.BlockSpec((tm, tn), lambda j, i: (i, j)),
        out_shape=jax.ShapeDtypeStruct((t, n), BF16),
        compiler_params=_cparams(("arbitrary", "arbitrary"), 48),
        name="gates",
    )(x_bf, w_g_t)


def _conv_tail(ext_ref, y_ref, cw_ref, cb_ref, lg_ref, lb_ref, wpw_bf, gate_ref, ga_ref, ts, rows):
    c_dim = y_ref.shape[1]
    for c in range(c_dim // LANES):
        cs = slice(c * LANES, (c + 1) * LANES)

        def rbody(r, carry, cs=cs):
            r0 = pl.multiple_of(r * rows, rows)
            win = ext_ref[pl.ds(r0, rows + CONV_HALO), cs]
            acc = jnp.zeros((rows, LANES), F32)
            for k in range(CONV_WIDTH):
                o = k + CONV_OFF
                acc = acc + win[o:o + rows, :] * cw_ref[k:k + 1, cs]
            y_ref[pl.ds(r0, rows), cs] = acc
            return carry

        lax.fori_loop(0, ts // rows, rbody, 0)

    y = _layer_norm(y_ref[...] + cb_ref[...], lg_ref[...], lb_ref[...])
    z = (y * jax.nn.sigmoid(y)).astype(BF16)
    a_out = _dot(z, wpw_bf[...])
    ga_ref[...] = (gate_ref[...].astype(F32) * a_out).astype(BF16)


def _conv_prompt_kernel(u_ref, halo_ref, cw_ref, cb_ref, lg_ref, lb_ref, wpw_ref, gate_ref, ga_ref,
                        ext_ref, y_ref, wpw_bf, *, ts, rows):
    b, i = pl.program_id(0), pl.program_id(1)

    @pl.when((b == 0) & (i == 0))
    def _():
        wpw_bf[...] = wpw_ref[...].astype(BF16)

    @pl.when(i == 0)
    def _():
        ext_ref[0:CONV_HALO, :] = jnp.zeros((CONV_HALO, ext_ref.shape[1]), F32)

    @pl.when(i > 0)
    def _():
        ext_ref[0:CONV_HALO, :] = halo_ref[...]

    ext_ref[CONV_HALO:CONV_HALO + ts, :] = u_ref[...]
    _conv_tail(ext_ref, y_ref, cw_ref, cb_ref, lg_ref, lb_ref, wpw_bf, gate_ref, ga_ref, ts, rows)


def _conv_sample_kernel(u_ref, hist_ref, cw_ref, cb_ref, lg_ref, lb_ref, wpw_ref, gate_ref, ga_ref,
                        ext_ref, y_ref, wpw_bf, *, ts, rows):
    @pl.when(pl.program_id(0) == 0)
    def _():
        wpw_bf[...] = wpw_ref[...].astype(BF16)

    ext_ref[0:CONV_HALO, :] = jnp.zeros((CONV_HALO, ext_ref.shape[1]), F32)
    ext_ref[CONV_OFF:CONV_HALO, :] = hist_ref[...]
    ext_ref[CONV_HALO:CONV_HALO + ts, :] = u_ref[...]
    _conv_tail(ext_ref, y_ref, cw_ref, cb_ref, lg_ref, lb_ref, wpw_bf, gate_ref, ga_ref, ts, rows)


def _conv_weight_specs(l, c_dim, d, nargs):
    z = (lambda *a: (l, 0, 0))
    return [pl.BlockSpec((None, CONV_WIDTH, c_dim), z),
            pl.BlockSpec((None, 1, c_dim), z),
            pl.BlockSpec((None, 1, c_dim), z),
            pl.BlockSpec((None, 1, c_dim), z),
            pl.BlockSpec((None, c_dim, d), z)]


def _conv_prompt_call(u, gates, conv_w, conv_b, ln_g, ln_b, w_pw, l, nb, seq, ts, rows):
    c_dim = u.shape[1]
    d = w_pw.shape[-1]
    ns = seq // ts
    hb = ts // CONV_HALO
    tile = lambda b, i: (b * ns + i, 0)
    return pl.pallas_call(
        functools.partial(_conv_prompt_kernel, ts=ts, rows=rows),
        grid=(nb, ns),
        in_specs=[pl.BlockSpec((ts, c_dim), tile),
                  pl.BlockSpec((CONV_HALO, c_dim), lambda b, i: (jnp.maximum((b * ns + i) * hb - 1, 0), 0))]
        + _conv_weight_specs(l, c_dim, d, 2)
        + [pl.BlockSpec((ts, d), tile)],
        out_specs=pl.BlockSpec((ts, d), tile),
        out_shape=jax.ShapeDtypeStruct((nb * seq, d), BF16),
        scratch_shapes=[pltpu.VMEM((CONV_HALO + ts, c_dim), F32), pltpu.VMEM((ts, c_dim), F32),
                        pltpu.VMEM((c_dim, d), BF16)],
        compiler_params=_cparams(("arbitrary", "arbitrary"), 56),
        name="conv_prompt",
    )(u, u, conv_w, conv_b, ln_g, ln_b, w_pw, gates)


def _conv_sample_call(u, gates, state_conv, conv_w, conv_b, ln_g, ln_b, w_pw, l, row0, nb, ts, rows):
    c_dim = u.shape[1]
    d = w_pw.shape[-1]
    t0 = row0 // ts
    tile = lambda b: (t0 + b, 0)
    return pl.pallas_call(
        functools.partial(_conv_sample_kernel, ts=ts, rows=rows),
        grid=(nb,),
        in_specs=[pl.BlockSpec((ts, c_dim), tile),
                  pl.BlockSpec((None, None, CONV_WIDTH - 1, c_dim), lambda b: (l, b, 0, 0))]
        + _conv_weight_specs(l, c_dim, d, 1)
        + [pl.BlockSpec((ts, d), tile)],
        out_specs=pl.BlockSpec((ts, d), lambda b: (b, 0)),
        out_shape=jax.ShapeDtypeStruct((nb * ts, d), BF16),
        scratch_shapes=[pltpu.VMEM((CONV_HALO + ts, c_dim), F32), pltpu.VMEM((ts, c_dim), F32),
                        pltpu.VMEM((c_dim, d), BF16)],
        compiler_params=_cparams(("arbitrary",), 56),
        name="conv_sample",
    )(u, state_conv, conv_w, conv_b, ln_g, ln_b, w_pw, gates)


def _q_kernel(qn_ref, w_ref, c_ref, s1_ref, s2_ref, qmask_ref, q_ref, *, scale):
    qn = qn_ref[...]
    c, s1, s2, qmask = c_ref[...], s1_ref[...], s2_ref[...], qmask_ref[...]
    for h in range(N_HEADS):
        qh = _dot(qn, w_ref[:, h * HEAD_PAD:(h + 1) * HEAD_PAD])
        q_ref[:, h * HEAD_PAD:h * HEAD_PAD + LANES] = (qh[:, :LANES] * scale).astype(BF16)
        q_ref[:, h * HEAD_PAD + LANES:(h + 1) * HEAD_PAD] = (
            _rope128(qh[:, LANES:], c, s1, s2) * scale + qmask).astype(BF16)


def _q_call(qn, w_q, tabs, l, tm, scale):
    t, r = qn.shape
    n = w_q.shape[-1]
    row = lambda i: (i, 0)
    return pl.pallas_call(
        functools.partial(_q_kernel, scale=scale),
        grid=(t // tm,),
        in_specs=[pl.BlockSpec((tm, r), row),
                  pl.BlockSpec((None, r, n), lambda i: (l, 0, 0)),
                  pl.BlockSpec((tm, LANES), row), pl.BlockSpec((tm, LANES), row), pl.BlockSpec((tm, LANES), row),
                  pl.BlockSpec((tm, LANES), row)],
        out_specs=pl.BlockSpec((tm, n), row),
        out_shape=jax.ShapeDtypeStruct((t, n), BF16),
        compiler_params=_cparams(("arbitrary",), 48),
        name="q_proj",
    )(qn, w_q, *tabs)


def _kv_kernel(lat_ref, kpad_ref, w_ref, k_ref, vt_ref, w_bf):
    @pl.when(pl.program_id(0) == 0)
    def _():
        w_bf[...] = w_ref[...].astype(BF16)

    lat = lat_ref[...].astype(BF16)
    kpad = kpad_ref[...]
    for h in range(N_HEADS):
        kv = _dot(lat, w_bf[:, h * HEAD_PAD:(h + 1) * HEAD_PAD])
        k_ref[:, h * HEAD_PAD:h * HEAD_PAD + LANES] = kv[:, :LANES].astype(BF16)
        k_ref[:, h * HEAD_PAD + LANES:(h + 1) * HEAD_PAD] = kpad
        vt_ref[h * V_HEAD_DIM:(h + 1) * V_HEAD_DIM, :] = kv[:, LANES:].T.astype(BF16)


def _kv_call(ckv, kpad, w_kv_b, l, rows, tm):
    r = ckv.shape[1]
    n = w_kv_b.shape[-1]
    row = lambda i: (i, 0)
    return pl.pallas_call(
        _kv_kernel,
        grid=(rows // tm,),
        in_specs=[pl.BlockSpec((tm, r), row),
                  pl.BlockSpec((tm, LANES), row),
                  pl.BlockSpec((None, r, n), lambda i: (l, 0, 0))],
        out_specs=[pl.BlockSpec((tm, N_HEADS * HEAD_PAD), row),
                   pl.BlockSpec((N_HEADS * V_HEAD_DIM, tm), lambda i: (0, i))],
        out_shape=[jax.ShapeDtypeStruct((rows, N_HEADS * HEAD_PAD), BF16),
                   jax.ShapeDtypeStruct((N_HEADS * V_HEAD_DIM, rows), BF16)],
        scratch_shapes=[pltpu.VMEM((r, n), BF16)],
        compiler_params=_cparams(("arbitrary",), 48),
        name="kv_proj",
    )(ckv, kpad, w_kv_b)


def _attn_prompt_kernel(qi_ref, kj_ref, q_ref, k_ref, vt_ref, ga_ref, gb_ref, o_ref,
                        s_a, mx_a, s_b, mx_b, acc_ref, m_ref, l_ref, *, n_steps, tq, unroll):
    def stage_scores(t, s_ref, mx_ref):
        q = q_ref[pl.ds(pl.multiple_of(qi_ref[t] * tq, tq), tq), :]
        k = k_ref[pl.ds(pl.multiple_of(kj_ref[t] * tq, tq), tq), :]
        s = _dot_nt(k, q)
        s_ref[...] = s
        mx_ref[...] = jnp.max(s, axis=0, keepdims=True)

    def step(t, s_cur, mx_cur, s_nxt, mx_nxt):
        stage_scores(t + 1, s_nxt, mx_nxt)
        cols = pl.ds(pl.multiple_of(qi_ref[t] * tq, tq), tq)
        m_old = m_ref[:, cols]
        m_new = jnp.maximum(m_old, mx_cur[...])
        alpha = jnp.exp2(m_old - m_new)
        p = jnp.exp2(s_cur[...] - m_new)
        l_ref[:, cols] = alpha * l_ref[:, cols] + jnp.sum(p, axis=0, keepdims=True)
        keys = pl.ds(pl.multiple_of(kj_ref[t] * tq, tq), tq)
        pv = _dot(vt_ref[:, keys], p.astype(BF16))
        acc_ref[:, cols] = alpha * acc_ref[:, cols] + pv
        m_ref[:, cols] = m_new

    m_ref[...] = jnp.full(m_ref.shape, NEG_BIG, F32)
    l_ref[...] = jnp.zeros(l_ref.shape, F32)
    acc_ref[...] = jnp.zeros(acc_ref.shape, F32)
    stage_scores(0, s_a, mx_a)

    def body(u, carry):
        for j in range(0, unroll, 2):
            step(unroll * u + j, s_a, mx_a, s_b, mx_b)
            step(unroll * u + j + 1, s_b, mx_b, s_a, mx_a)
        return carry

    lax.fori_loop(0, n_steps // unroll, body, 0)

    def finish(qi, carry):
        rows = pl.ds(pl.multiple_of(qi * tq, tq), tq)
        o = (acc_ref[:, rows] / l_ref[:, rows]).T
        o_ref[rows, :] = (ga_ref[rows, :].astype(F32) + gb_ref[rows, :].astype(F32) * o).astype(BF16)
        return carry

    lax.fori_loop(0, o_ref.shape[0] // tq, finish, 0)


def _attn_prompt_call(q, k, vt, ga, gates, nb, seq, tq):
    d = N_HEADS * V_HEAD_DIM
    nq = seq // tq
    unroll = 4
    assert nq >= 2 and seq // CHUNK <= LANES - ROPE_DIM
    tiles = [(qi, kj) for qi in range(nq) for kj in range(qi + 1)]
    tiles += [(0, 1)] * (-len(tiles) % unroll) + [(0, 0)]
    qi_tab = jnp.asarray([t[0] for t in tiles], jnp.int32)
    kj_tab = jnp.asarray([t[1] for t in tiles], jnp.int32)
    bh = lambda b, h, qt, kt: (b, h)
    return pl.pallas_call(
        functools.partial(_attn_prompt_kernel, n_steps=len(tiles) - 1, tq=tq, unroll=unroll),
        grid_spec=pltpu.PrefetchScalarGridSpec(
            num_scalar_prefetch=2, grid=(nb, N_HEADS),
            in_specs=[pl.BlockSpec((seq, HEAD_PAD), bh),
                      pl.BlockSpec((seq, HEAD_PAD), bh),
                      pl.BlockSpec((V_HEAD_DIM, seq), lambda b, h, qt, kt: (h, b)),
                      pl.BlockSpec((seq, V_HEAD_DIM), bh),
                      pl.BlockSpec((seq, V_HEAD_DIM), lambda b, h, qt, kt: (b, N_HEADS + h))],
            out_specs=pl.BlockSpec((seq, V_HEAD_DIM), bh),
            scratch_shapes=[pltpu.VMEM((tq, tq), F32), pltpu.VMEM((1, tq), F32),
                            pltpu.VMEM((tq, tq), F32), pltpu.VMEM((1, tq), F32),
                            pltpu.VMEM((V_HEAD_DIM, seq), F32), pltpu.VMEM((1, seq), F32), pltpu.VMEM((1, seq), F32)]),
        out_shape=jax.ShapeDtypeStruct((nb * seq, d), BF16),
        compiler_params=_cparams(("arbitrary", "arbitrary"), 48),
        name="attn_prompt",
    )(qi_tab, kj_tab, q, k, vt, ga, gates)


def _attn_sample_kernel(q_ref, latp_ref, kpep_ref, latn_ref, kpen_ref, w_ref, ga_ref, gb_ref, o_ref,
                        w_bf, qlat_ref, qpe_ref, kpp_ref, kpn_ref, olat_ref, *, ds):
    @pl.when(pl.program_id(0) == 0)
    def _():
        w_bf[...] = w_ref[...].astype(BF16)
        kpp_ref[...] = jnp.zeros(kpp_ref.shape, BF16)
        kpn_ref[...] = jnp.zeros(kpn_ref.shape, BF16)

    kpp_ref[:ROPE_DIM, :] = kpep_ref[...].astype(BF16)
    kpn_ref[:, :ROPE_DIM] = kpen_ref[...].astype(BF16)
    for h in range(N_HEADS):
        rows = slice(h * ds, (h + 1) * ds)
        qn = q_ref[:, h * HEAD_PAD:h * HEAD_PAD + LANES]
        qlat_ref[rows, :] = _dot_nt(qn, w_bf[:, h * HEAD_PAD:h * HEAD_PAD + LANES]).astype(BF16)
        qpe_ref[rows, :] = q_ref[:, h * HEAD_PAD + LANES:(h + 1) * HEAD_PAD]

    lat_p = latp_ref[...].astype(BF16)
    lat_n = latn_ref[...].astype(BF16)
    qlat, qpe = qlat_ref[...], qpe_ref[...]
    s_p = _dot_nt(qlat, lat_p) + _dot(qpe, kpp_ref[...])
    s_n = _dot_nt(qlat, lat_n) + _dot_nt(qpe, kpn_ref[...])
    m = jnp.maximum(jnp.max(s_p, axis=1, keepdims=True), jnp.max(s_n, axis=1, keepdims=True))
    p_p = jnp.exp2(s_p - m)
    p_n = jnp.exp2(s_n - m)
    den = jnp.sum(p_p, axis=1, keepdims=True) + jnp.sum(p_n, axis=1, keepdims=True)
    o_lat = _dot(p_p.astype(BF16), lat_p) + _dot(p_n.astype(BF16), lat_n)
    olat_ref[...] = (o_lat / den).astype(BF16)
    for h in range(N_HEADS):
        cols = slice(h * V_HEAD_DIM, (h + 1) * V_HEAD_DIM)
        o = _dot(olat_ref[h * ds:(h + 1) * ds, :], w_bf[:, h * HEAD_PAD + LANES:(h + 1) * HEAD_PAD])
        o_ref[:, cols] = (ga_ref[:, cols].astype(F32) + gb_ref[:, cols].astype(F32) * o).astype(BF16)


def _attn_sample_call(q, cache_lat, cache_pe, ckv, kpe, w_kv_b, ga_s, gates, l, row0, nb, ds):
    past, r = cache_lat.shape[2], cache_lat.shape[3]
    d = N_HEADS * V_HEAD_DIM
    t0 = row0 // ds
    tile = lambda b: (t0 + b, 0)
    return pl.pallas_call(
        functools.partial(_attn_sample_kernel, ds=ds),
        grid=(nb,),
        in_specs=[pl.BlockSpec((ds, N_HEADS * HEAD_PAD), tile),
                  pl.BlockSpec((None, None, past, r), lambda b: (l, b, 0, 0)),
                  pl.BlockSpec((None, None, ROPE_DIM, past), lambda b: (l, b, 0, 0)),
                  pl.BlockSpec((ds, r), tile),
                  pl.BlockSpec((ds, ROPE_DIM), tile),
                  pl.BlockSpec((None, r, N_HEADS * HEAD_PAD), lambda b: (l, 0, 0)),
                  pl.BlockSpec((ds, d), lambda b: (b, 0)),
                  pl.BlockSpec((ds, d), lambda b: (t0 + b, 1))],
        out_specs=pl.BlockSpec((ds, d), lambda b: (b, 0)),
        out_shape=jax.ShapeDtypeStruct((nb * ds, d), BF16),
        scratch_shapes=[pltpu.VMEM((r, N_HEADS * HEAD_PAD), BF16),
                        pltpu.VMEM((N_HEADS * ds, r), BF16),
                        pltpu.VMEM((N_HEADS * ds, LANES), BF16),
                        pltpu.VMEM((LANES, past), BF16),
                        pltpu.VMEM((ds, LANES), BF16),
                        pltpu.VMEM((N_HEADS * ds, r), BF16)],
        compiler_params=_cparams(("arbitrary",), 56),
        name="attn_sample",
    )(q, cache_lat, cache_pe, ckv, kpe, w_kv_b, ga_s, gates)


def _route(scores, bias):
    sel = [s + b for s, b in zip(scores, bias)]
    n = EXPERTS_PER_GROUP
    grp = []
    for g in range(N_EXPERT_GROUPS):
        v = sel[g * n:(g + 1) * n]
        best = None
        for i in range(n):
            for j in range(i + 1, n):
                pair = v[i] + v[j]
                best = pair if best is None else jnp.maximum(best, pair)
        grp.append(best)
    g_idx = jnp.zeros_like(grp[0], dtype=jnp.int32)
    g_best = grp[0]
    for g in range(1, N_EXPERT_GROUPS):
        better = grp[g] > g_best
        g_idx = jnp.where(better, g, g_idx)
        g_best = jnp.where(better, grp[g], g_best)

    def pick(rows_by_group):
        out = rows_by_group[0]
        for g in range(1, N_EXPERT_GROUPS):
            out = jnp.where(g_idx == g, rows_by_group[g], out)
        return out

    in_sel = [pick([sel[g * n + j] for g in range(N_EXPERT_GROUPS)]) for j in range(n)]
    in_sc = [pick([scores[g * n + j] for g in range(N_EXPERT_GROUPS)]) for j in range(n)]

    def argmax_first(vals, excluded):
        idx = None
        best = None
        for j in range(n):
            v = vals[j] if excluded is None else jnp.where(excluded == j, -jnp.inf, vals[j])
            if best is None:
                best, idx = v, jnp.zeros_like(g_idx)
            else:
                better = v > best
                idx = jnp.where(better, j, idx)
                best = jnp.where(better, v, best)
        return idx

    l0 = argmax_first(in_sel, None)
    l1 = argmax_first(in_sel, l0)

    def take(vals, idx):
        out = vals[0]
        for j in range(1, n):
            out = jnp.where(idx == j, vals[j], out)
        return out

    w0, w1 = take(in_sc, l0), take(in_sc, l1)
    tot = w0 + w1
    return g_idx * n + l0, g_idx * n + l1, w0 / tot, w1 / tot


def _wo_kernel(mixp_ref, mixs_ref, x_ref, w_ref, g_ref, b_ref, wr_ref, br_ref, x1_ref, gate_ref, eidx_ref, w_bf,
               *, alpha, n_prompt_tiles):
    @pl.when(pl.program_id(0) == 0)
    def _():
        w_bf[...] = w_ref[...].astype(BF16)

    mix = jnp.where(pl.program_id(0) < n_prompt_tiles, mixp_ref[...], mixs_ref[...])
    y = alpha * x_ref[...] + _dot(mix, w_bf[...])
    x1 = _layer_norm(y, g_ref[...], b_ref[...])
    x1_ref[...] = x1
    logits =lax.dot_general(wr_ref[...], x1, (((1,), (1,)), ((), ())), preferred_element_type=F32,
                             precision=lax.Precision.HIGHEST)
    sc = jax.nn.sigmoid(logits)
    br = br_ref[...]
    e0, e1, g0, g1 = _route([sc[e:e + 1, :] for e in range(N_EXPERTS)],
                            [br[e:e + 1, :] for e in range(N_EXPERTS)])
    rows = lax.broadcasted_iota(jnp.int32, (LANES, sc.shape[1]), 0)
    gate_ref[...] = jnp.where(rows == 0, g0, jnp.where(rows == 1, g1, 0.0)).T
    rows8 = lax.broadcasted_iota(jnp.int32, eidx_ref.shape, 0)
    eidx_ref[...] = jnp.where(rows8 == 0, e0, jnp.where(rows8 == 1, e1, 0))


def _wo_call(mixed_p, mixed_s, x, w_o, ln_g, ln_b, wr_t, br, l, tm, alpha):
    t, d = x.shape
    n_p = mixed_p.shape[0] // tm
    row = lambda i: (i, 0)
    vec = pl.BlockSpec((None, 1, d), lambda i: (l, 0, 0))
    return pl.pallas_call(
        functools.partial(_wo_kernel, alpha=alpha, n_prompt_tiles=n_p),
        grid=(t // tm,),
        in_specs=[pl.BlockSpec((tm, d), lambda i: (jnp.minimum(i, n_p - 1), 0)),
                  pl.BlockSpec((tm, d), lambda i: (jnp.maximum(i - n_p, 0), 0)),
                  pl.BlockSpec((tm, d), row),
                  pl.BlockSpec((None, d, d), lambda i: (l, 0, 0), pipeline_mode=pl.Buffered(1)),
                  vec, vec,
                  pl.BlockSpec((N_EXPERTS, d), lambda i: (0, 0)),
                  pl.BlockSpec((N_EXPERTS, 1), lambda i: (0, 0))],
        out_specs=[pl.BlockSpec((tm, d), row), pl.BlockSpec((tm, LANES), row),
                   pl.BlockSpec((SUBLANES, tm), lambda i: (0, i))],
        out_shape=[jax.ShapeDtypeStruct((t, d), F32), jax.ShapeDtypeStruct((t, LANES), F32),
                   jax.ShapeDtypeStruct((SUBLANES, t), jnp.int32)],
        scratch_shapes=[pltpu.VMEM((d, d), BF16)],
        compiler_params=_cparams(("arbitrary",), 56),
        name="wo_ln_router",
    )(mixed_p, mixed_s, x, w_o, ln_g, ln_b, wr_t, br)


def _dispatch_kernel(eidx_ref, pos_ref, meta_ref):
    t = eidx_ref.shape[1]
    e0, e1 = eidx_ref[0:1, :], eidx_ref[1:2, :]
    rows = lax.broadcasted_iota(jnp.int32, (N_EXPERTS, t), 0)
    hit0, hit1 = rows == e0, rows == e1
    oh = jnp.where(hit0 | hit1, 1.0, 0.0)
    cnt = jnp.sum(oh, axis=1, keepdims=True)
    padded = jnp.floor((cnt + (MOE_TILE - 1)) * (1.0 / MOE_TILE)) * MOE_TILE
    erow = lax.broadcasted_iota(jnp.int32, (N_EXPERTS, 1), 0)
    off = jnp.zeros((N_EXPERTS, 1), F32)
    run = jnp.zeros((1, 1), F32)
    for e in range(N_EXPERTS):
        off = jnp.where(erow == e, run, off)
        run = run + padded[e:e + 1, :]
    c = PREFIX_CHUNK
    before = jnp.where(lax.broadcasted_iota(jnp.int32, (c, c), 0) < lax.broadcasted_iota(jnp.int32, (c, c), 1),
                       1.0, 0.0).astype(BF16)
    pos_ref[...] = jnp.zeros(pos_ref.shape, jnp.int32)
    carry = off
    for j in range(t // c):
        cs = slice(j * c, (j + 1) * c)
        ohc = oh[:, cs]
        slot = _dot(ohc.astype(BF16), before) + carry
        pos_ref[0:1, cs] = jnp.sum(jnp.where(hit0[:, cs], slot, 0.0), axis=0, keepdims=True).astype(jnp.int32)
        pos_ref[1:2, cs] = jnp.sum(jnp.where(hit1[:, cs], slot, 0.0), axis=0, keepdims=True).astype(jnp.int32)
        carry = carry + jnp.sum(ohc, axis=1, keepdims=True)
    start = lax.broadcasted_iota(jnp.int32, (N_EXPERTS, LANES), 1).astype(F32) * MOE_TILE
    te = jnp.minimum(jnp.sum(jnp.where(off + padded <= start, 1.0, 0.0), axis=0, keepdims=True), N_EXPERTS - 1.0)
    mine = lax.broadcasted_iota(jnp.int32, (N_EXPERTS, LANES), 0).astype(F32) == te
    end_valid = jnp.sum(jnp.where(mine, off + cnt, 0.0), axis=0, keepdims=True)
    nvalid = jnp.clip(end_valid - start[0:1, :], 0.0, MOE_TILE)
    r8 = lax.broadcasted_iota(jnp.int32, meta_ref.shape, 0)
    meta = jnp.where(r8 == 0, te, jnp.where(r8 == 1, nvalid, jnp.where(r8 == 2, run * (1.0 / MOE_TILE), 0.0)))
    meta_ref[...] = meta.astype(jnp.int32)


def _dispatch_call(eidx):
    t = eidx.shape[1]
    assert t % PREFIX_CHUNK == 0
    return pl.pallas_call(
        _dispatch_kernel,
        out_shape=[jax.ShapeDtypeStruct((SUBLANES, t), jnp.int32), jax.ShapeDtypeStruct((SUBLANES, LANES), jnp.int32)],
        compiler_params=pltpu.CompilerParams(vmem_limit_bytes=32 * MIB),
        name="moe_dispatch",
    )(eidx)


def _row_copy(src, i, dst, j, sem):
    return pltpu.make_async_copy(src.at[pl.ds(i, 1), :], dst.at[pl.ds(j, 1), :], sem)


def _scatter_kernel(p0_ref, p1_ref, x_ref, init_hbm, xs_hbm, sem, *, rows):
    del init_hbm
    base = pl.program_id(0) * rows

    def body(r, carry):
        t = base + r
        _row_copy(x_ref, r, xs_hbm, p0_ref[t], sem).start()
        _row_copy(x_ref, r, xs_hbm, p1_ref[t], sem).start()
        return carry

    lax.fori_loop(0, rows, body, 0, unroll=8)
    for _ in range(2):
        pltpu.make_async_copy(x_ref, xs_hbm.at[pl.ds(0, rows), :], sem).wait()


def _scatter_call(pos0, pos1, x1, init, rows):
    t, d = x1.shape
    return pl.pallas_call(
        functools.partial(_scatter_kernel, rows=rows),
        grid_spec=pltpu.PrefetchScalarGridSpec(
            num_scalar_prefetch=2, grid=(t // rows,),
            in_specs=[pl.BlockSpec((rows, d), lambda i, p0, p1: (i, 0)), pl.BlockSpec(memory_space=pl.ANY)],
            out_specs=pl.BlockSpec(memory_space=pl.ANY),
            scratch_shapes=[pltpu.SemaphoreType.DMA(())]),
        out_shape=jax.ShapeDtypeStruct(init.shape, F32),
        input_output_aliases={3: 0},
        compiler_params=pltpu.CompilerParams(dimension_semantics=("arbitrary",), vmem_limit_bytes=32 * MIB,
                                             disable_bounds_checks=True),
        name="moe_scatter",
    )(pos0, pos1, x1, init)


def _experts_kernel(te_ref, nv_ref, nu_ref, xs_ref, wg_ref, wu_ref, wd_ref, y_ref, wg_bf, wu_bf, wd_bf):
    i = pl.program_id(0)
    nv = nv_ref[i]
    fresh = (i == 0) | (te_ref[i] != te_ref[jnp.maximum(i - 1, 0)])

    @pl.when(fresh & (nv > 0))
    def _():
        wg_bf[...] = wg_ref[...].astype(BF16)
        wu_bf[...] = wu_ref[...].astype(BF16)
        wd_bf[...] = wd_ref[...].astype(BF16)

    @pl.when(nv > 0)
    def _():
        row = lax.broadcasted_iota(jnp.int32, xs_ref.shape, 0)
        x = jnp.where(row < nv, xs_ref[...], 0.0).astype(BF16)
        hg = _dot(x, wg_bf[...])
        hu = _dot(x, wu_bf[...])
        h = hg * jax.nn.sigmoid(hg) * hu
        y_ref[...] = _dot(h.astype(BF16), wd_bf[...])

    @pl.when(nv == 0)
    def _():
        y_ref[...] = jnp.zeros(y_ref.shape, F32)


def _experts_call(te, nv, nu, xs, w_gate, w_up, w_down, l):
    n_slots, d = xs.shape
    f = w_gate.shape[-1]
    n_tiles = n_slots // MOE_TILE
    tile = lambda i, te, nv, nu: (jnp.minimum(i, nu[0] - 1), 0)
    return pl.pallas_call(
        _experts_kernel,
        grid_spec=pltpu.PrefetchScalarGridSpec(
            num_scalar_prefetch=3, grid=(n_tiles,),
            in_specs=[pl.BlockSpec((MOE_TILE, d), tile),
                      pl.BlockSpec((None, None, d, f), lambda i, te, nv, nu: (l, te[i], 0, 0)),
                      pl.BlockSpec((None, None, d, f), lambda i, te, nv, nu: (l, te[i], 0, 0)),
                      pl.BlockSpec((None, None, f, d), lambda i, te, nv, nu: (l, te[i], 0, 0))],
            out_specs=pl.BlockSpec((MOE_TILE, d), lambda i, te, nv, nu: (i, 0)),
            scratch_shapes=[pltpu.VMEM((d, f), BF16), pltpu.VMEM((d, f), BF16), pltpu.VMEM((f, d), BF16)]),
        out_shape=jax.ShapeDtypeStruct((n_slots, d), F32),
        compiler_params=_cparams(("arbitrary",), 56),
        name="moe_experts",
    )(te, nv, nu, xs, w_gate, w_up, w_down)


def _combine_kernel(p0_ref, p1_ref, y_hbm, x1_ref, gate_ref, g_ref, b_ref, x2_ref, x2bf_ref, buf, sems, *, tm, alpha):
    i = pl.program_id(0)
    slot = i % 2

    def issue(tile, s):
        base = tile * tm

        def body(r, carry):
            t = base + r
            _row_copy(y_hbm, p0_ref[t], buf.at[s, 0], r, sems.at[s]).start()
            _row_copy(y_hbm, p1_ref[t], buf.at[s, 1], r, sems.at[s]).start()
            return carry

        lax.fori_loop(0, tm, body, 0, unroll=8)

    @pl.when(i == 0)
    def _():
        issue(0, 0)

    @pl.when(i + 1 < pl.num_programs(0))
    def _():
        issue(i + 1, 1 - slot)

    for k in range(2):
        pltpu.make_async_copy(y_hbm.at[pl.ds(0, tm), :], buf.at[slot, k], sems.at[slot]).wait()
    gate = gate_ref[...]
    moe = gate[:, 0:1] * buf[slot, 0] + gate[:, 1:2] * buf[slot, 1]
    x2 = _layer_norm(alpha * x1_ref[...] + moe, g_ref[...], b_ref[...])
    x2_ref[...] = x2
    x2bf_ref[...] = x2.astype(BF16)


def _combine_call(pos0, pos1, y, x1, gate, ln_g, ln_b, l, tm, alpha):
    t, d = x1.shape
    row = lambda i, p0, p1: (i, 0)
    vec = pl.BlockSpec((None, 1, d), lambda i, p0, p1: (l, 0, 0))
    return pl.pallas_call(
        functools.partial(_combine_kernel, tm=tm, alpha=alpha),
        grid_spec=pltpu.PrefetchScalarGridSpec(
            num_scalar_prefetch=2, grid=(t // tm,),
            in_specs=[pl.BlockSpec(memory_space=pl.ANY),
                      pl.BlockSpec((tm, d), row), pl.BlockSpec((tm, LANES), row), vec, vec],
            out_specs=[pl.BlockSpec((tm, d), row), pl.BlockSpec((tm, d), row)],
            scratch_shapes=[pltpu.VMEM((2, 2, tm, d), F32), pltpu.SemaphoreType.DMA((2,))]),
        out_shape=[jax.ShapeDtypeStruct((t, d), F32), jax.ShapeDtypeStruct((t, d), BF16)],
        compiler_params=pltpu.CompilerParams(dimension_semantics=("arbitrary",), vmem_limit_bytes=48 * MIB,
                                             disable_bounds_checks=True),
        name="moe_combine_ln",
    )(pos0, pos1, y, x1, gate, ln_g, ln_b)


def _rope_tables(pos):
    half = ROPE_DIM // 2
    inv = jnp.float32(ROPE_THETA) ** (-jnp.arange(half, dtype=F32) / half)
    ang = pos.astype(F32)[:, None] * inv[None, :]
    cos, sin = jnp.cos(ang), jnp.sin(ang)
    z = jnp.zeros_like(cos)
    return (jnp.concatenate([cos, cos, z, z], 1),
            jnp.concatenate([-sin, z, z, z], 1),
            jnp.concatenate([z, sin, z, z], 1))


def kernel(x_prompt, x_sample, cache_kv_latent, cache_k_rope, state_conv, w_in, q_norm_g, w_q_b, kv_norm_g, w_kv_b,
           conv_w, conv_b, conv_ln_g, conv_ln_b, w_conv_pw, w_o, ln1_g, ln1_b, w_gate, w_up, w_down, ln2_g, ln2_b,
           w_router, b_router):
    nb, seq, d = x_prompt.shape
    db, ds, _ = x_sample.shape
    depth = w_in.shape[0]
    past = cache_kv_latent.shape[2]
    d_conv = conv_w.shape[-1]
    rq, rkv = q_norm_g.shape[-1], kv_norm_g.shape[-1]
    tp, tsmp = nb * seq, db * ds
    t_all = tp + tsmp
    splits = (2 * d_conv, 2 * d_conv + rq, 2 * d_conv + rq + rkv, 2 * d_conv + rq + rkv + ROPE_DIM)
    alpha = (2.0 * depth) ** 0.25
    scale = math.log2(math.e) / math.sqrt(QK_NOPE_DIM + ROPE_DIM)
    tm = 512
    tq = 512
    n_tiles = 2 * t_all // MOE_TILE + N_EXPERTS
    assert t_all % MOE_TILE == 0 and n_tiles <= LANES
    assert d == N_HEADS * V_HEAD_DIM and w_kv_b.shape[-1] == N_HEADS * HEAD_PAD
    assert tp % tm == 0 and tsmp % tm == 0 and seq % tq == 0 and tq % CHUNK == 0
    assert past % CHUNK == 0 and ds <= CHUNK and ds >= CONV_WIDTH - 1

    pos = jnp.concatenate([jnp.tile(jnp.arange(seq, dtype=jnp.int32), nb),
                           jnp.tile(past + jnp.arange(ds, dtype=jnp.int32), db)])
    tabs = _rope_tables(pos)
    lane_chunk = jnp.arange(LANES, dtype=jnp.int32)[None, :] - ROPE_DIM
    row_chunk = jnp.where(jnp.arange(t_all) < tp, pos // CHUNK, LANES)[:, None]
    qmask = jnp.where((lane_chunk >= 0) & (lane_chunk > row_chunk) & (row_chunk < LANES), NEG_BIG, 0.0).astype(F32)
    kmask = jnp.where(lane_chunk == row_chunk, 1.0, 0.0).astype(F32)
    tabs_k, tabs_q = tabs + (kmask,), tabs + (qmask,)

    w_in_t = jnp.swapaxes(w_in, 1, 2)
    w_g_t = w_in_t[:, splits[3]:, :].astype(BF16)
    cache_pe_t = jnp.swapaxes(cache_k_rope, 2, 3)
    wq = w_q_b.reshape(depth, rq, N_HEADS, QK_NOPE_DIM + ROPE_DIM)
    wq = jnp.pad(wq, ((0, 0), (0, 0), (0, 0), (0, HEAD_PAD - QK_NOPE_DIM - ROPE_DIM)))
    wq = wq.reshape(depth, rq, N_HEADS * HEAD_PAD).astype(BF16)
    vec3 = lambda a: a.reshape(depth, 1, a.shape[-1])
    q_norm_g3, kv_norm_g3 = vec3(q_norm_g), vec3(kv_norm_g)
    conv_b3, conv_ln_g3, conv_ln_b3 = vec3(conv_b), vec3(conv_ln_g), vec3(conv_ln_b)
    ln1_g3, ln1_b3, ln2_g3, ln2_b3 = vec3(ln1_g), vec3(ln1_b), vec3(ln2_g), vec3(ln2_b)
    wr_t = w_router.T
    br = b_router.reshape(N_EXPERTS, 1)

    x = jnp.concatenate([x_prompt.reshape(tp, d), x_sample.reshape(tsmp, d)], axis=0)
    x_bf = x.astype(BF16)

    spare = jnp.zeros((n_tiles * MOE_TILE, d), F32)
    keep = CONV_WIDTH - 1
    outs = [[] for _ in range(6)]
    for l in range(depth):
        u = _glu_call(x_bf, w_in_t, l, d_conv, tm, 512)
        qn, ckv, kpe, kpad = _latent_call(x_bf, w_in_t, q_norm_g3, kv_norm_g3, tabs_k, l, splits, rq, rkv, tm)
        gates = _gates_call(x_bf, w_g_t, l, tm, 1024)
        ga_p = _conv_prompt_call(u, gates, conv_w, conv_b3, conv_ln_g3, conv_ln_b3, w_conv_pw, l, nb, seq, 512, 64)
        ga_s = _conv_sample_call(u, gates, state_conv, conv_w, conv_b3, conv_ln_g3, conv_ln_b3, w_conv_pw,
                                 l, tp, db, ds, ds)
        q = _q_call(qn, wq, tabs_q, l, tm, scale)
        k, vt = _kv_call(ckv, kpad, w_kv_b, l, tp, tm)
        mixed_p = _attn_prompt_call(q, k, vt, ga_p, gates, nb, seq, tq)
        mixed_s = _attn_sample_call(q, cache_kv_latent, cache_pe_t, ckv, kpe, w_kv_b, ga_s, gates, l, tp, db, ds)
        x1, gate, eidx = _wo_call(mixed_p, mixed_s, x, w_o, ln1_g3, ln1_b3, wr_t, br, l, tm, alpha)
        pos_rows, meta = _dispatch_call(eidx)
        pos0, pos1 = pos_rows[0], pos_rows[1]
        te, nv, nu = meta[0, :n_tiles], meta[1, :n_tiles], meta[2, :1]
        xs = _scatter_call(pos0, pos1, x1, spare, tm)
        y = _experts_call(te, nv, nu, xs, w_gate, w_up, w_down, l)
        x, x_bf = _combine_call(pos0, pos1, y, x1, gate, ln2_g3, ln2_b3, l, MOE_TILE, alpha)
        spare = y
        outs[0].append(ckv[:tp].reshape(nb, seq, rkv))
        outs[1].append(kpe[:tp].reshape(nb, seq, ROPE_DIM))
        outs[2].append(u[:tp].reshape(nb, seq, d_conv)[:, seq - keep:])
        outs[3].append(ckv[tp:].reshape(db, ds, rkv))
        outs[4].append(kpe[tp:].reshape(db, ds, ROPE_DIM))
        outs[5].append(u[tp:].reshape(db, ds, d_conv)[:, ds - keep:])

    return (x[:tp].reshape(nb, seq, d), x[tp:].reshape(db, ds, d)) + tuple(jnp.stack(o) for o in outs)
```

```python
import functools
import math

import jax
import jax.numpy as jnp
from jax import lax
from jax.experimental import pallas as pl
from jax.experimental.pallas import tpu as pltpu

F32 = jnp.float32
BF16 = jnp.bfloat16

CHUNK = 64
CONV_WIDTH = 31
N_HEADS = 16
QK_NOPE_DIM = 128
ROPE_DIM = 64
V_HEAD_DIM = 128
ROPE_THETA = 10000.0
N_EXPERTS = 16
N_EXPERT_GROUPS = 4
EXPERTS_PER_GROUP = N_EXPERTS // N_EXPERT_GROUPS
LN_EPS = 1e-5
RMS_EPS = 1e-6

LANES = 128
SUBLANES = 8
MOE_TILE = 256
PREFIX_CHUNK = 512
SUM_ROWS = 16
HEAD_PAD = 256
CONV_HALO = 32
CONV_OFF = CONV_HALO - (CONV_WIDTH - 1)
NEG_BIG = -1e30
MIB = 1024 * 1024


def _cparams(sem, vmem_mib):
    return pltpu.CompilerParams(dimension_semantics=sem, vmem_limit_bytes=vmem_mib * MIB)


def _dot(a, b):
    return jnp.dot(a, b, preferred_element_type=F32)


def _dot_nt(a, b):
    return lax.dot_general(a, b, (((1,), (1,)), ((), ())), preferred_element_type=F32)


def _layer_norm(y, g, b):
    mu = jnp.mean(y, axis=-1, keepdims=True)
    d = y - mu
    var = jnp.mean(d * d, axis=-1, keepdims=True)
    return d * lax.rsqrt(var + LN_EPS) * g + b


def _rms_norm(y, g):
    return y * lax.rsqrt(jnp.mean(y * y, axis=-1, keepdims=True) + RMS_EPS) * g


def _rope128(v, c, s1, s2):
    return v * c + pltpu.roll(v, 96, axis=1) * s1 + pltpu.roll(v, 32, axis=1) * s2


def _glu_kernel(x_ref, wa_ref, wg_ref, u_ref, wa_bf, wg_bf):
    @pl.when(pl.program_id(1) == 0)
    def _():
        wa_bf[...] = wa_ref[...].astype(BF16)
        wg_bf[...] = wg_ref[...].astype(BF16)

    x = x_ref[...]
    a = _dot_nt(x, wa_bf[...])
    g = _dot_nt(x, wg_bf[...])
    u_ref[...] = a * jax.nn.sigmoid(g)


def _glu_call(x_bf, w_in_t, l, d_conv, tm, tn):
    t, d = x_bf.shape
    nj = d_conv // tn
    return pl.pallas_call(
        _glu_kernel,
        grid=(nj, t // tm),
        in_specs=[pl.BlockSpec((tm, d), lambda j, i: (i, 0)),
                  pl.BlockSpec((None, tn, d), lambda j, i: (l, j, 0)),
                  pl.BlockSpec((None, tn, d), lambda j, i: (l, j + nj, 0))],
        out_specs=pl.BlockSpec((tm, tn), lambda j, i: (i, j)),
        out_shape=jax.ShapeDtypeStruct((t, d_conv), F32),
        scratch_shapes=[pltpu.VMEM((tn, d), BF16), pltpu.VMEM((tn, d), BF16)],
        compiler_params=_cparams(("arbitrary", "arbitrary"), 48),
        name="glu",
    )(x_bf, w_in_t, w_in_t)


def _latent_kernel(x_ref, wq_ref, wc_ref, wk_ref, qg_ref, cg_ref, c_ref, s1_ref, s2_ref, kmask_ref,
                   qn_ref, ckv_ref, kpe_ref, kpad_ref, wq_bf, wc_bf, wk_bf):
    @pl.when(pl.program_id(0) == 0)
    def _():
        wq_bf[...] = wq_ref[...].astype(BF16)
        wc_bf[...] = wc_ref[...].astype(BF16)
        wk_bf[...] = wk_ref[...].astype(BF16)

    x = x_ref[...]
    qn_ref[...] = _rms_norm(_dot_nt(x, wq_bf[...]), qg_ref[...]).astype(BF16)
    ckv_ref[...] = _rms_norm(_dot_nt(x, wc_bf[...]), cg_ref[...])
    kr = _dot_nt(x, wk_bf[...])
    k = _rope128(kr, c_ref[...], s1_ref[...], s2_ref[...])
    lane = lax.broadcasted_iota(jnp.int32, k.shape, 1)
    k = jnp.where(lane < ROPE_DIM, k, 0.0)
    kpe_ref[...] = k[:, :ROPE_DIM]
    kpad_ref[...] = (k + kmask_ref[...]).astype(BF16)


def _latent_call(x_bf, w_in_t, q_norm_g, kv_norm_g, tabs, l, splits, rq, rkv, tm):
    t, d = x_bf.shape
    assert splits[0] % rq == 0 and splits[1] % rkv == 0 and splits[2] % LANES == 0
    row = lambda i: (i, 0)
    return pl.pallas_call(
        _latent_kernel,
        grid=(t // tm,),
        in_specs=[pl.BlockSpec((tm, d), row),
                  pl.BlockSpec((None, rq, d), lambda i: (l, splits[0] // rq, 0)),
                  pl.BlockSpec((None, rkv, d), lambda i: (l, splits[1] // rkv, 0)),
                  pl.BlockSpec((None, LANES, d), lambda i: (l, splits[2] // LANES, 0)),
                  pl.BlockSpec((None, 1, rq), lambda i: (l, 0, 0)),
                  pl.BlockSpec((None, 1, rkv), lambda i: (l, 0, 0)),
                  pl.BlockSpec((tm, LANES), row),
                  pl.BlockSpec((tm, LANES), row),
                  pl.BlockSpec((tm, LANES), row),
                  pl.BlockSpec((tm, LANES), row)],
        out_specs=[pl.BlockSpec((tm, rq), row),
                   pl.BlockSpec((tm, rkv), row),
                   pl.BlockSpec((tm, ROPE_DIM), row),
                   pl.BlockSpec((tm, LANES), row)],
        out_shape=[jax.ShapeDtypeStruct((t, rq), BF16),
                   jax.ShapeDtypeStruct((t, rkv), F32),
                   jax.ShapeDtypeStruct((t, ROPE_DIM), F32),
                   jax.ShapeDtypeStruct((t, LANES), BF16)],
        scratch_shapes=[pltpu.VMEM((rq, d), BF16), pltpu.VMEM((rkv, d), BF16), pltpu.VMEM((LANES, d), BF16)],
        compiler_params=_cparams(("arbitrary",), 48),
        name="latent",
    )(x_bf, w_in_t, w_in_t, w_in_t, q_norm_g, kv_norm_g, *tabs)


def _gates_kernel(x_ref, w_hbm, o_ref, w_f32, w_bf, sem, *, l, row0, tn):
    j = pl.program_id(0)

    def block_copy(jj):
        return pltpu.make_async_copy(w_hbm.at[l, pl.ds(pl.multiple_of(row0 + jj * tn, SUBLANES), tn), :], w_f32, sem)

    @pl.when(pl.program_id(1) == 0)
    def _():
        @pl.when(j == 0)
        def _():
            block_copy(0).start()

        block_copy(j).wait()
        w_bf[...] = w_f32[...].astype(BF16)

        @pl.when(j + 1 < pl.num_programs(0))
        def _():
            block_copy(j + 1).start()

    o_ref[...] = jax.nn.sigmoid(_dot_nt(x_ref[...], w_bf[...])).astype(BF16)


def _gates_call(x_bf, w_in_t, l, row0, tm, tn):
    t, d = x_bf.shape
    n = w_in_t.shape[1] - row0
    assert n % tn == 0 and row0 % SUBLANES == 0
    return pl.pallas_call(
        functools.partial(_gates_kernel, l=l, row0=row0, tn=tn),
        grid=(n // tn, t // tm),
        in_specs=[pl.BlockSpec((tm, d), lambda j, i: (i, 0)),
                  pl.BlockSpec(memory_space=pl.ANY)],
        out_specs=pl.BlockSpec((tm, tn), lambda j, i: (i, j)),
        out_shape=jax.ShapeDtypeStruct((t, n), BF16),
        scratch_shapes=[pltpu.VMEM((tn, d), F32), pltpu.VMEM((tn, d), BF16), pltpu.SemaphoreType.DMA(())],
        compiler_params=_cparams(("arbitrary", "arbitrary"), 48),
        name="gates",
    )(x_bf, w_in_t)


def _conv_tail(ext_ref, y_ref, cw_ref, cb_ref, lg_ref, lb_ref, wpw_bf, gate_ref, ga_ref, ts, rows):
    c_dim = y_ref.shape[1]
    for c in range(c_dim // LANES):
        cs = slice(c * LANES, (c + 1) * LANES)

        def rbody(r, carry, cs=cs):
            r0 = pl.multiple_of(r * rows, rows)
            win = ext_ref[pl.ds(r0, rows + CONV_HALO), cs]
            acc = jnp.zeros((rows, LANES), F32)
            for b in range(SUBLANES):
                taps = [o for o in range(CONV_OFF, CONV_OFF + CONV_WIDTH) if o % SUBLANES == b]
                n_win = rows + CONV_HALO
                shifted = win if b == 0 else pltpu.roll(win, n_win - b, axis=0)
                for o in taps:
                    acc = acc + shifted[o - b:o - b + rows, :] * cw_ref[o - CONV_OFF:o - CONV_OFF + 1, cs]
            y_ref[pl.ds(r0, rows), cs] = acc
            return carry

        lax.fori_loop(0, ts // rows, rbody, 0)

    y = _layer_norm(y_ref[...] + cb_ref[...], lg_ref[...], lb_ref[...])
    z = (y * jax.nn.sigmoid(y)).astype(BF16)
    a_out = _dot(z, wpw_bf[...])
    ga_ref[...] = (gate_ref[...].astype(F32) * a_out).astype(BF16)


def _conv_prompt_kernel(u_ref, halo_ref, cw_ref, cb_ref, lg_ref, lb_ref, wpw_ref, gate_ref, ga_ref,
                        ext_ref, y_ref, wpw_bf, *, ts, rows):
    b, i = pl.program_id(0), pl.program_id(1)

    @pl.when((b == 0) & (i == 0))
    def _():
        wpw_bf[...] = wpw_ref[...].astype(BF16)

    @pl.when(i == 0)
    def _():
        ext_ref[0:CONV_HALO, :] = jnp.zeros((CONV_HALO, ext_ref.shape[1]), F32)

    @pl.when(i > 0)
    def _():
        ext_ref[0:CONV_HALO, :] = halo_ref[...]

    ext_ref[CONV_HALO:CONV_HALO + ts, :] = u_ref[...]
    _conv_tail(ext_ref, y_ref, cw_ref, cb_ref, lg_ref, lb_ref, wpw_bf, gate_ref, ga_ref, ts, rows)


def _conv_sample_kernel(u_ref, hist_ref, cw_ref, cb_ref, lg_ref, lb_ref, wpw_ref, gate_ref, ga_ref,
                        ext_ref, y_ref, wpw_bf, *, ts, rows):
    @pl.when(pl.program_id(0) == 0)
    def _():
        wpw_bf[...] = wpw_ref[...].astype(BF16)

    ext_ref[0:CONV_HALO, :] = jnp.zeros((CONV_HALO, ext_ref.shape[1]), F32)
    ext_ref[CONV_OFF:CONV_HALO, :] = hist_ref[...]
    ext_ref[CONV_HALO:CONV_HALO + ts, :] = u_ref[...]
    _conv_tail(ext_ref, y_ref, cw_ref, cb_ref, lg_ref, lb_ref, wpw_bf, gate_ref, ga_ref, ts, rows)


def _conv_weight_specs(l, c_dim, d, nargs):
    z = (lambda *a: (l, 0, 0))
    return [pl.BlockSpec((None, CONV_WIDTH, c_dim), z),
            pl.BlockSpec((None, 1, c_dim), z),
            pl.BlockSpec((None, 1, c_dim), z),
            pl.BlockSpec((None, 1, c_dim), z),
            pl.BlockSpec((None, c_dim, d), z)]


def _conv_prompt_call(u, gates, conv_w, conv_b, ln_g, ln_b, w_pw, l, nb, seq, ts, rows):
    c_dim = u.shape[1]
    d = w_pw.shape[-1]
    ns = seq // ts
    hb = ts // CONV_HALO
    tile = lambda b, i: (b * ns + i, 0)
    return pl.pallas_call(
        functools.partial(_conv_prompt_kernel, ts=ts, rows=rows),
        grid=(nb, ns),
        in_specs=[pl.BlockSpec((ts, c_dim), tile),
                  pl.BlockSpec((CONV_HALO, c_dim), lambda b, i: (jnp.maximum((b * ns + i) * hb - 1, 0), 0))]
        + _conv_weight_specs(l, c_dim, d, 2)
        + [pl.BlockSpec((ts, d), tile)],
        out_specs=pl.BlockSpec((ts, d), tile),
        out_shape=jax.ShapeDtypeStruct((nb * seq, d), BF16),
        scratch_shapes=[pltpu.VMEM((CONV_HALO + ts, c_dim), F32), pltpu.VMEM((ts, c_dim), F32),
                        pltpu.VMEM((c_dim, d), BF16)],
        compiler_params=_cparams(("arbitrary", "arbitrary"), 56),
        name="conv_prompt",
    )(u, u, conv_w, conv_b, ln_g, ln_b, w_pw, gates)


def _conv_sample_call(u, gates, state_conv, conv_w, conv_b, ln_g, ln_b, w_pw, l, row0, nb, ts, rows):
    c_dim = u.shape[1]
    d = w_pw.shape[-1]
    t0 = row0 // ts
    tile = lambda b: (t0 + b, 0)
    return pl.pallas_call(
        functools.partial(_conv_sample_kernel, ts=ts, rows=rows),
        grid=(nb,),
        in_specs=[pl.BlockSpec((ts, c_dim), tile),
                  pl.BlockSpec((None, None, CONV_WIDTH - 1, c_dim), lambda b: (l, b, 0, 0))]
        + _conv_weight_specs(l, c_dim, d, 1)
        + [pl.BlockSpec((ts, d), tile)],
        out_specs=pl.BlockSpec((ts, d), lambda b: (b, 0)),
        out_shape=jax.ShapeDtypeStruct((nb * ts, d), BF16),
        scratch_shapes=[pltpu.VMEM((CONV_HALO + ts, c_dim), F32), pltpu.VMEM((ts, c_dim), F32),
                        pltpu.VMEM((c_dim, d), BF16)],
        compiler_params=_cparams(("arbitrary",), 56),
        name="conv_sample",
    )(u, state_conv, conv_w, conv_b, ln_g, ln_b, w_pw, gates)


def _q_kernel(qn_ref, w_ref, c_ref, s1_ref, s2_ref, qmask_ref, q_ref, *, scale):
    qn = qn_ref[...]
    c, s1, s2, qmask = c_ref[...], s1_ref[...], s2_ref[...], qmask_ref[...]
    for h in range(N_HEADS):
        qh = _dot(qn, w_ref[:, h * HEAD_PAD:(h + 1) * HEAD_PAD])
        q_ref[:, h * HEAD_PAD:h * HEAD_PAD + LANES] = (qh[:, :LANES] * scale).astype(BF16)
        q_ref[:, h * HEAD_PAD + LANES:(h + 1) * HEAD_PAD] = (
            _rope128(qh[:, LANES:], c, s1, s2) * scale + qmask).astype(BF16)


def _q_call(qn, w_q, tabs, l, tm, scale):
    t, r = qn.shape
    n = w_q.shape[-1]
    row = lambda i: (i, 0)
    return pl.pallas_call(
        functools.partial(_q_kernel, scale=scale),
        grid=(t // tm,),
        in_specs=[pl.BlockSpec((tm, r), row),
                  pl.BlockSpec((None, r, n), lambda i: (l, 0, 0)),
                  pl.BlockSpec((tm, LANES), row), pl.BlockSpec((tm, LANES), row), pl.BlockSpec((tm, LANES), row),
                  pl.BlockSpec((tm, LANES), row)],
        out_specs=pl.BlockSpec((tm, n), row),
        out_shape=jax.ShapeDtypeStruct((t, n), BF16),
        compiler_params=_cparams(("arbitrary",), 48),
        name="q_proj",
    )(qn, w_q, *tabs)


def _kv_kernel(lat_ref, kpad_ref, w_ref, k_ref, vt_ref, w_bf):
    @pl.when(pl.program_id(0) == 0)
    def _():
        w_bf[...] = w_ref[...].astype(BF16)

    lat = lat_ref[...].astype(BF16)
    kpad = kpad_ref[...]
    for h in range(N_HEADS):
        kv = _dot(lat, w_bf[:, h * HEAD_PAD:(h + 1) * HEAD_PAD])
        k_ref[:, h * HEAD_PAD:h * HEAD_PAD + LANES] = kv[:, :LANES].astype(BF16)
        k_ref[:, h * HEAD_PAD + LANES:(h + 1) * HEAD_PAD] = kpad
        vt_ref[h * V_HEAD_DIM:(h + 1) * V_HEAD_DIM, :] = kv[:, LANES:].T.astype(BF16)


def _kv_call(ckv, kpad, w_kv_b, l, rows, tm):
    r = ckv.shape[1]
    n = w_kv_b.shape[-1]
    row = lambda i: (i, 0)
    return pl.pallas_call(
        _kv_kernel,
        grid=(rows // tm,),
        in_specs=[pl.BlockSpec((tm, r), row),
                  pl.BlockSpec((tm, LANES), row),
                  pl.BlockSpec((None, r, n), lambda i: (l, 0, 0))],
        out_specs=[pl.BlockSpec((tm, N_HEADS * HEAD_PAD), row),
                   pl.BlockSpec((N_HEADS * V_HEAD_DIM, tm), lambda i: (0, i))],
        out_shape=[jax.ShapeDtypeStruct((rows, N_HEADS * HEAD_PAD), BF16),
                   jax.ShapeDtypeStruct((N_HEADS * V_HEAD_DIM, rows), BF16)],
        scratch_shapes=[pltpu.VMEM((r, n), BF16)],
        compiler_params=_cparams(("arbitrary",), 48),
        name="kv_proj",
    )(ckv, kpad, w_kv_b)


def _attn_prompt_kernel(qi_ref, kj_ref, q_ref, k_ref, vt_ref, ga_ref, gb_ref, o_ref, *scratch,
                        n_steps, tq, unroll, n_buf):
    def stage_scores(t, s_ref, mx_ref):
        q = q_ref[pl.ds(pl.multiple_of(qi_ref[t] * tq, tq), tq), :]
        k = k_ref[pl.ds(pl.multiple_of(kj_ref[t] * tq, tq), tq), :]
        s = _dot_nt(k, q)
        s_ref[...] = s
        mx_ref[...] = jnp.max(s, axis=0, keepdims=True)

    bufs = [(scratch[2 * j], scratch[2 * j + 1]) for j in range(n_buf)]
    acc_ref, m_ref = scratch[2 * n_buf], scratch[2 * n_buf + 1]

    def step(t, j):
        s_cur, mx_cur = bufs[j]
        stage_scores(t + n_buf - 1, *bufs[(j - 1) % n_buf])
        cols = pl.ds(pl.multiple_of(qi_ref[t] * tq, tq), tq)
        m_old = m_ref[:, cols]
        m_new = jnp.maximum(m_old, mx_cur[...])
        alpha = jnp.exp2(m_old - m_new)
        p = jnp.exp2((s_cur[...] - m_new).astype(BF16))
        keys = pl.ds(pl.multiple_of(kj_ref[t] * tq, tq), tq)
        lhs = jnp.concatenate([vt_ref[:, keys], jnp.ones((SUM_ROWS, tq), BF16)], axis=0)
        acc_ref[:, cols] = alpha * acc_ref[:, cols] + _dot(lhs, p)
        m_ref[:, cols] = m_new

    m_ref[...] = jnp.full(m_ref.shape, NEG_BIG, F32)
    acc_ref[...] = jnp.zeros(acc_ref.shape, F32)
    for j in range(n_buf - 1):
        stage_scores(j, *bufs[j])

    def body(u, carry):
        for j in range(unroll):
            step(unroll * u + j, j % n_buf)
        return carry

    lax.fori_loop(0, n_steps // unroll, body, 0)

    def finish(qi, carry):
        rows = pl.ds(pl.multiple_of(qi * tq, tq), tq)
        den = acc_ref[V_HEAD_DIM:V_HEAD_DIM + 1, rows]
        o = (acc_ref[:V_HEAD_DIM, rows] / den).T
        o_ref[rows, :] = (ga_ref[rows, :].astype(F32) + gb_ref[rows, :].astype(F32) * o).astype(BF16)
        return carry

    lax.fori_loop(0, o_ref.shape[0] // tq, finish, 0)


def _attn_prompt_call(q, k, vt, ga, gates, nb, seq, tq):
    d = N_HEADS * V_HEAD_DIM
    nq = seq // tq
    n_buf, unroll = 3, 6
    assert nq >= 2 and seq // CHUNK <= LANES - ROPE_DIM
    tiles = [(qi, kj) for qi in range(nq) for kj in range(qi + 1)]
    tiles += [(0, 1)] * (-len(tiles) % unroll)
    n_steps = len(tiles)
    tiles += [(0, 0)] * (n_buf - 1)
    qi_tab = jnp.asarray([t[0] for t in tiles], jnp.int32)
    kj_tab = jnp.asarray([t[1] for t in tiles], jnp.int32)
    bh = lambda b, h, qt, kt: (b, h)
    return pl.pallas_call(
        functools.partial(_attn_prompt_kernel, n_steps=n_steps, tq=tq, unroll=unroll, n_buf=n_buf),
        grid_spec=pltpu.PrefetchScalarGridSpec(
            num_scalar_prefetch=2, grid=(nb, N_HEADS),
            in_specs=[pl.BlockSpec((seq, HEAD_PAD), bh),
                      pl.BlockSpec((seq, HEAD_PAD), bh),
                      pl.BlockSpec((V_HEAD_DIM, seq), lambda b, h, qt, kt: (h, b)),
                      pl.BlockSpec((seq, V_HEAD_DIM), bh),
                      pl.BlockSpec((seq, V_HEAD_DIM), lambda b, h, qt, kt: (b, N_HEADS + h))],
            out_specs=pl.BlockSpec((seq, V_HEAD_DIM), bh),
            scratch_shapes=[pltpu.VMEM((tq, tq), F32), pltpu.VMEM((1, tq), F32)] * n_buf
            + [pltpu.VMEM((V_HEAD_DIM + SUM_ROWS, seq), F32), pltpu.VMEM((1, seq), F32)]),
        out_shape=jax.ShapeDtypeStruct((nb * seq, d), BF16),
        compiler_params=_cparams(("arbitrary", "arbitrary"), 48),
        name="attn_prompt",
    )(qi_tab, kj_tab, q, k, vt, ga, gates)


def _attn_sample_kernel(q_ref, latp_ref, kpep_ref, latn_ref, kpen_ref, w_ref, ga_ref, gb_ref, o_ref,
                        w_bf, qlat_ref, qpe_ref, kpp_ref, kpn_ref, olat_ref, *, ds):
    @pl.when(pl.program_id(0) == 0)
    def _():
        w_bf[...] = w_ref[...].astype(BF16)
        kpp_ref[...] = jnp.zeros(kpp_ref.shape, BF16)
        kpn_ref[...] = jnp.zeros(kpn_ref.shape, BF16)

    kpp_ref[:ROPE_DIM, :] = kpep_ref[...].astype(BF16)
    kpn_ref[:, :ROPE_DIM] = kpen_ref[...].astype(BF16)
    for h in range(N_HEADS):
        rows = slice(h * ds, (h + 1) * ds)
        qn = q_ref[:, h * HEAD_PAD:h * HEAD_PAD + LANES]
        qlat_ref[rows, :] = _dot_nt(qn, w_bf[:, h * HEAD_PAD:h * HEAD_PAD + LANES]).astype(BF16)
        qpe_ref[rows, :] = q_ref[:, h * HEAD_PAD + LANES:(h + 1) * HEAD_PAD]

    lat_p = latp_ref[...].astype(BF16)
    lat_n = latn_ref[...].astype(BF16)
    qlat, qpe = qlat_ref[...], qpe_ref[...]
    s_p = _dot_nt(qlat, lat_p) + _dot(qpe, kpp_ref[...])
    s_n = _dot_nt(qlat, lat_n) + _dot_nt(qpe, kpn_ref[...])
    m = jnp.maximum(jnp.max(s_p, axis=1, keepdims=True), jnp.max(s_n, axis=1, keepdims=True))
    p_p = jnp.exp2(s_p - m)
    p_n = jnp.exp2(s_n - m)
    den = jnp.sum(p_p, axis=1, keepdims=True) + jnp.sum(p_n, axis=1, keepdims=True)
    o_lat = _dot(p_p.astype(BF16), lat_p) + _dot(p_n.astype(BF16), lat_n)
    olat_ref[...] = (o_lat / den).astype(BF16)
    for h in range(N_HEADS):
        cols = slice(h * V_HEAD_DIM, (h + 1) * V_HEAD_DIM)
        o = _dot(olat_ref[h * ds:(h + 1) * ds, :], w_bf[:, h * HEAD_PAD + LANES:(h + 1) * HEAD_PAD])
        o_ref[:, cols] = (ga_ref[:, cols].astype(F32) + gb_ref[:, cols].astype(F32) * o).astype(BF16)


def _attn_sample_call(q, cache_lat, cache_pe, ckv, kpe, w_kv_b, ga_s, gates, l, row0, nb, ds):
    past, r = cache_lat.shape[2], cache_lat.shape[3]
    d = N_HEADS * V_HEAD_DIM
    t0 = row0 // ds
    tile = lambda b: (t0 + b, 0)
    return pl.pallas_call(
        functools.partial(_attn_sample_kernel, ds=ds),
        grid=(nb,),
        in_specs=[pl.BlockSpec((ds, N_HEADS * HEAD_PAD), tile),
                  pl.BlockSpec((None, None, past, r), lambda b: (l, b, 0, 0)),
                  pl.BlockSpec((None, None, ROPE_DIM, past), lambda b: (l, b, 0, 0)),
                  pl.BlockSpec((ds, r), tile),
                  pl.BlockSpec((ds, ROPE_DIM), tile),
                  pl.BlockSpec((None, r, N_HEADS * HEAD_PAD), lambda b: (l, 0, 0)),
                  pl.BlockSpec((ds, d), lambda b: (b, 0)),
                  pl.BlockSpec((ds, d), lambda b: (t0 + b, 1))],
        out_specs=pl.BlockSpec((ds, d), lambda b: (b, 0)),
        out_shape=jax.ShapeDtypeStruct((nb * ds, d), BF16),
        scratch_shapes=[pltpu.VMEM((r, N_HEADS * HEAD_PAD), BF16),
                        pltpu.VMEM((N_HEADS * ds, r), BF16),
                        pltpu.VMEM((N_HEADS * ds, LANES), BF16),
                        pltpu.VMEM((LANES, past), BF16),
                        pltpu.VMEM((ds, LANES), BF16),
                        pltpu.VMEM((N_HEADS * ds, r), BF16)],
        compiler_params=_cparams(("arbitrary",), 56),
        name="attn_sample",
    )(q, cache_lat, cache_pe, ckv, kpe, w_kv_b, ga_s, gates)


def _route(scores, bias):
    sel = [s + b for s, b in zip(scores, bias)]
    n = EXPERTS_PER_GROUP
    grp = []
    for g in range(N_EXPERT_GROUPS):
        v = sel[g * n:(g + 1) * n]
        best = None
        for i in range(n):
            for j in range(i + 1, n):
                pair = v[i] + v[j]
                best = pair if best is None else jnp.maximum(best, pair)
        grp.append(best)
    g_idx = jnp.zeros_like(grp[0], dtype=jnp.int32)
    g_best = grp[0]
    for g in range(1, N_EXPERT_GROUPS):
        better = grp[g] > g_best
        g_idx = jnp.where(better, g, g_idx)
        g_best = jnp.where(better, grp[g], g_best)

    def pick(rows_by_group):
        out = rows_by_group[0]
        for g in range(1, N_EXPERT_GROUPS):
            out = jnp.where(g_idx == g, rows_by_group[g], out)
        return out

    in_sel = [pick([sel[g * n + j] for g in range(N_EXPERT_GROUPS)]) for j in range(n)]
    in_sc = [pick([scores[g * n + j] for g in range(N_EXPERT_GROUPS)]) for j in range(n)]

    def argmax_first(vals, excluded):
        idx = None
        best = None
        for j in range(n):
            v = vals[j] if excluded is None else jnp.where(excluded == j, -jnp.inf, vals[j])
            if best is None:
                best, idx = v, jnp.zeros_like(g_idx)
            else:
                better = v > best
                idx = jnp.where(better, j, idx)
                best = jnp.where(better, v, best)
        return idx

    l0 = argmax_first(in_sel, None)
    l1 = argmax_first(in_sel, l0)

    def take(vals, idx):
        out = vals[0]
        for j in range(1, n):
            out = jnp.where(idx == j, vals[j], out)
        return out

    w0, w1 = take(in_sc, l0), take(in_sc, l1)
    tot = w0 + w1
    return g_idx * n + l0, g_idx * n + l1, w0 / tot, w1 / tot


def _wo_kernel(mixp_ref, mixs_ref, x_ref, w_bf, g_ref, b_ref, wr_ref, br_ref, x1_ref, gate_ref, eidx_ref,
               *, alpha, n_prompt_tiles, n_sub):
    from_prompt = pl.program_id(0) < n_prompt_tiles
    br = br_ref[...]
    tm = x_ref.shape[0]
    sub = tm // n_sub
    for j in range(n_sub):
        rs = slice(j * sub, (j + 1) * sub)
        mix = jnp.where(from_prompt, mixp_ref[rs, :], mixs_ref[rs, :])
        y = alpha * x_ref[rs, :] + _dot(mix, w_bf[...])
        x1 = _layer_norm(y, g_ref[...], b_ref[...])
        x1_ref[rs, :] = x1
        logits = lax.dot_general(wr_ref[...], x1, (((1,), (1,)), ((), ())), preferred_element_type=F32,
                                 precision=lax.Precision.HIGHEST)
        sc = jax.nn.sigmoid(logits)
        e0, e1, g0, g1 = _route([sc[e:e + 1, :] for e in range(N_EXPERTS)],
                                [br[e:e + 1, :] for e in range(N_EXPERTS)])
        rows = lax.broadcasted_iota(jnp.int32, (LANES, sub), 0)
        gate_ref[rs, :] = jnp.where(rows == 0, g0, jnp.where(rows == 1, g1, 0.0)).T
        rows8 = lax.broadcasted_iota(jnp.int32, (SUBLANES, sub), 0)
        eidx_ref[:, rs] = jnp.where(rows8 == 0, e0, jnp.where(rows8 == 1, e1, 0))


def _wo_call(mixed_p, mixed_s, x, w_o, ln_g, ln_b, wr_t, br, l, tm, alpha):
    t, d = x.shape
    n_p = mixed_p.shape[0] // tm
    row = lambda i: (i, 0)
    vec = pl.BlockSpec((None, 1, d), lambda i: (l, 0, 0))
    return pl.pallas_call(
        functools.partial(_wo_kernel, alpha=alpha, n_prompt_tiles=n_p, n_sub=1),
        grid=(t // tm,),
        in_specs=[pl.BlockSpec((tm, d), lambda i: (jnp.minimum(i, n_p - 1), 0)),
                  pl.BlockSpec((tm, d), lambda i: (jnp.maximum(i - n_p, 0), 0)),
                  pl.BlockSpec((tm, d), row),
                  pl.BlockSpec((None, d, d), lambda i: (l, 0, 0), pipeline_mode=pl.Buffered(1)),
                  vec, vec,
                  pl.BlockSpec((N_EXPERTS, d), lambda i: (0, 0)),
                  pl.BlockSpec((N_EXPERTS, 1), lambda i: (0, 0))],
        out_specs=[pl.BlockSpec((tm, d), row), pl.BlockSpec((tm, LANES), row),
                   pl.BlockSpec((SUBLANES, tm), lambda i: (0, i))],
        out_shape=[jax.ShapeDtypeStruct((t, d), F32), jax.ShapeDtypeStruct((t, LANES), F32),
                   jax.ShapeDtypeStruct((SUBLANES, t), jnp.int32)],
        compiler_params=_cparams(("arbitrary",), 56),
        name="wo_ln_router",
    )(mixed_p, mixed_s, x, w_o, ln_g, ln_b, wr_t, br)


def _dispatch_kernel(eidx_ref, pos_ref, meta_ref):
    t = eidx_ref.shape[1]
    e0, e1 = eidx_ref[0:1, :], eidx_ref[1:2, :]
    rows = lax.broadcasted_iota(jnp.int32, (N_EXPERTS, t), 0)
    hit0, hit1 = rows == e0, rows == e1
    oh = jnp.where(hit0 | hit1, 1.0, 0.0)
    cnt = jnp.sum(oh, axis=1, keepdims=True)
    padded = jnp.floor((cnt + (MOE_TILE - 1)) * (1.0 / MOE_TILE)) * MOE_TILE
    erow = lax.broadcasted_iota(jnp.int32, (N_EXPERTS, 1), 0)
    off = jnp.zeros((N_EXPERTS, 1), F32)
    run = jnp.zeros((1, 1), F32)
    for e in range(N_EXPERTS):
        off = jnp.where(erow == e, run, off)
        run = run + padded[e:e + 1, :]
    c = PREFIX_CHUNK
    before = jnp.where(lax.broadcasted_iota(jnp.int32, (c, c), 0) < lax.broadcasted_iota(jnp.int32, (c, c), 1),
                       1.0, 0.0).astype(BF16)
    pos_ref[...] = jnp.zeros(pos_ref.shape, jnp.int32)
    carry = off
    for j in range(t // c):
        cs = slice(j * c, (j + 1) * c)
        ohc = oh[:, cs]
        slot = _dot(ohc.astype(BF16), before) + carry
        pos_ref[0:1, cs] = jnp.sum(jnp.where(hit0[:, cs], slot, 0.0), axis=0, keepdims=True).astype(jnp.int32)
        pos_ref[1:2, cs] = jnp.sum(jnp.where(hit1[:, cs], slot, 0.0), axis=0, keepdims=True).astype(jnp.int32)
        carry = carry + jnp.sum(ohc, axis=1, keepdims=True)
    start = lax.broadcasted_iota(jnp.int32, (N_EXPERTS, LANES), 1).astype(F32) * MOE_TILE
    te = jnp.minimum(jnp.sum(jnp.where(off + padded <= start, 1.0, 0.0), axis=0, keepdims=True), N_EXPERTS - 1.0)
    mine = lax.broadcasted_iota(jnp.int32, (N_EXPERTS, LANES), 0).astype(F32) == te
    end_valid = jnp.sum(jnp.where(mine, off + cnt, 0.0), axis=0, keepdims=True)
    nvalid = jnp.clip(end_valid - start[0:1, :], 0.0, MOE_TILE)
    ordinal = jnp.zeros((N_EXPERTS, 1), F32)
    seen = jnp.zeros((1, 1), F32)
    for e in range(N_EXPERTS):
        ordinal = jnp.where(erow == e, seen, ordinal)
        seen = seen + jnp.where(cnt[e:e + 1, :] > 0.0, 1.0, 0.0)
    nxt = erow.astype(F32)
    later = jnp.full((1, 1), -1.0, F32)
    for e in reversed(range(N_EXPERTS)):
        nxt = jnp.where((erow == e) & (later >= 0.0), later, nxt)
        later = jnp.where(cnt[e:e + 1, :] > 0.0, float(e), later)
    parity = jnp.sum(jnp.where(mine, ordinal - 2.0 * jnp.floor(ordinal * 0.5), 0.0), axis=0, keepdims=True)
    nxt_tile = jnp.sum(jnp.where(mine, nxt, 0.0), axis=0, keepdims=True)
    r8 = lax.broadcasted_iota(jnp.int32, meta_ref.shape, 0)
    meta = jnp.zeros(meta_ref.shape, F32)
    for r, v in enumerate([te, nvalid, run * (1.0 / MOE_TILE), parity, nxt_tile]):
        meta = jnp.where(r8 == r, v, meta)
    meta_ref[...] = meta.astype(jnp.int32)


def _dispatch_call(eidx):
    t = eidx.shape[1]
    assert t % PREFIX_CHUNK == 0
    return pl.pallas_call(
        _dispatch_kernel,
        out_shape=[jax.ShapeDtypeStruct((SUBLANES, t), jnp.int32), jax.ShapeDtypeStruct((SUBLANES, LANES), jnp.int32)],
        compiler_params=pltpu.CompilerParams(vmem_limit_bytes=32 * MIB),
        name="moe_dispatch",
    )(eidx)


def _row_copy(src, i, dst, j, sem):
    return pltpu.make_async_copy(src.at[pl.ds(i, 1), :], dst.at[pl.ds(j, 1), :], sem)


def _scatter_kernel(p0_ref, p1_ref, x_ref, init_hbm, xs_hbm, sem, *, rows):
    del init_hbm
    base = pl.program_id(0) * rows

    def body(r, carry):
        t = base + r
        _row_copy(x_ref, r, xs_hbm, p0_ref[t], sem).start()
        _row_copy(x_ref, r, xs_hbm, p1_ref[t], sem).start()
        return carry

    lax.fori_loop(0, rows, body, 0, unroll=8)
    for _ in range(2):
        pltpu.make_async_copy(x_ref, xs_hbm.at[pl.ds(0, rows), :], sem).wait()


def _scatter_call(pos0, pos1, x1, init, rows):
    t, d = x1.shape
    return pl.pallas_call(
        functools.partial(_scatter_kernel, rows=rows),
        grid_spec=pltpu.PrefetchScalarGridSpec(
            num_scalar_prefetch=2, grid=(t // rows,),
            in_specs=[pl.BlockSpec((rows, d), lambda i, p0, p1: (i, 0)), pl.BlockSpec(memory_space=pl.ANY)],
            out_specs=pl.BlockSpec(memory_space=pl.ANY),
            scratch_shapes=[pltpu.SemaphoreType.DMA(())]),
        out_shape=jax.ShapeDtypeStruct(init.shape, init.dtype),
        input_output_aliases={3: 0},
        compiler_params=pltpu.CompilerParams(dimension_semantics=("arbitrary",), vmem_limit_bytes=32 * MIB,
                                             disable_bounds_checks=True),
        name="moe_scatter",
    )(pos0, pos1, x1, init)


def _experts_kernel(te_ref, nv_ref, nu_ref, slot_ref, nxt_ref, xs_ref, wg_hbm, wu_hbm, wd_hbm, y_ref,
                    wg_f, wu_f, wd_f, wg_bf, wu_bf, wd_bf, sems, *, l):
    i = pl.program_id(0)
    nv = nv_ref[i]
    e = te_ref[i]
    s = slot_ref[i]
    fresh = (i == 0) | (e != te_ref[jnp.maximum(i - 1, 0)])

    def weight_copies(expert, slot):
        return [pltpu.make_async_copy(w.at[l, expert], buf.at[slot], sems.at[slot])
                for w, buf in ((wg_hbm, wg_f), (wu_hbm, wu_f), (wd_hbm, wd_f))]

    @pl.when(fresh & (nv > 0))
    def _():
        @pl.when(i == 0)
        def _():
            for cp in weight_copies(e, s):
                cp.start()

        for cp in weight_copies(e, s):
            cp.wait()
        wg_bf[...] = wg_f[s].astype(BF16)
        wu_bf[...] = wu_f[s].astype(BF16)
        wd_bf[...] = wd_f[s].astype(BF16)

        @pl.when(nxt_ref[i] != e)
        def _():
            for cp in weight_copies(nxt_ref[i], 1 - s):
                cp.start()

    @pl.when(nv > 0)
    def _():
        row = lax.broadcasted_iota(jnp.int32, xs_ref.shape, 0)
        x = jnp.where(row < nv, xs_ref[...], 0.0).astype(BF16)
        hg = _dot(x, wg_bf[...])
        hu = _dot(x, wu_bf[...])
        h = hg * jax.nn.sigmoid(hg) * hu
        y_ref[...] = _dot(h.astype(BF16), wd_bf[...])

    @pl.when(nv == 0)
    def _():
        y_ref[...] = jnp.zeros(y_ref.shape, F32)


def _experts_call(meta, xs, w_gate, w_up, w_down, l):
    n_slots, d = xs.shape
    f = w_gate.shape[-1]
    n_tiles = n_slots // MOE_TILE
    any_spec = pl.BlockSpec(memory_space=pl.ANY)
    return pl.pallas_call(
        functools.partial(_experts_kernel, l=l),
        grid_spec=pltpu.PrefetchScalarGridSpec(
            num_scalar_prefetch=5, grid=(n_tiles,),
            in_specs=[pl.BlockSpec((MOE_TILE, d), lambda i, te, nv, nu, sl, nx: (jnp.minimum(i, nu[0] - 1), 0)),
                      any_spec, any_spec, any_spec],
            out_specs=pl.BlockSpec((MOE_TILE, d), lambda i, te, nv, nu, sl, nx: (i, 0)),
            scratch_shapes=[pltpu.VMEM((2, d, f), F32), pltpu.VMEM((2, d, f), F32), pltpu.VMEM((2, f, d), F32),
                            pltpu.VMEM((d, f), BF16), pltpu.VMEM((d, f), BF16), pltpu.VMEM((f, d), BF16),
                            pltpu.SemaphoreType.DMA((2,))]),
        out_shape=jax.ShapeDtypeStruct((n_slots, d), F32),
        compiler_params=_cparams(("arbitrary",), 56),
        name="moe_experts",
    )(*meta, xs, w_gate, w_up, w_down)


def _combine_kernel(p0_ref, p1_ref, y_hbm, x1_ref, gate_ref, g_ref, b_ref, x2_ref, x2bf_ref, buf, sems, *, tm, alpha):
    i = pl.program_id(0)
    slot = i % 2

    def issue(tile, s):
        base = tile * tm

        def body(r, carry):
            t = base + r
            _row_copy(y_hbm, p0_ref[t], buf.at[s, 0], r, sems.at[s]).start()
            _row_copy(y_hbm, p1_ref[t], buf.at[s, 1], r, sems.at[s]).start()
            return carry

        lax.fori_loop(0, tm, body, 0, unroll=8)

    @pl.when(i == 0)
    def _():
        issue(0, 0)

    @pl.when(i + 1 < pl.num_programs(0))
    def _():
        issue(i + 1, 1 - slot)

    for k in range(2):
        pltpu.make_async_copy(y_hbm.at[pl.ds(0, tm), :], buf.at[slot, k], sems.at[slot]).wait()
    gate = gate_ref[...]
    moe = gate[:, 0:1] * buf[slot, 0] + gate[:, 1:2] * buf[slot, 1]
    x2 =_layer_norm(alpha * x1_ref[...] + moe, g_ref[...], b_ref[...])
    x2_ref[...] = x2
    x2bf_ref[...] = x2.astype(BF16)


def _combine_call(pos0, pos1, y, x1, gate, ln_g, ln_b, l, tm, alpha):
    t, d = x1.shape
    row = lambda i, p0, p1: (i, 0)
    vec = pl.BlockSpec((None, 1, d), lambda i, p0, p1: (l, 0, 0))
    return pl.pallas_call(
        functools.partial(_combine_kernel, tm=tm, alpha=alpha),
        grid_spec=pltpu.PrefetchScalarGridSpec(
            num_scalar_prefetch=2, grid=(t // tm,),
            in_specs=[pl.BlockSpec(memory_space=pl.ANY),
                      pl.BlockSpec((tm, d), row), pl.BlockSpec((tm, LANES), row), vec, vec],
            out_specs=[pl.BlockSpec((tm, d), row), pl.BlockSpec((tm, d), row)],
            scratch_shapes=[pltpu.VMEM((2, 2, tm, d), F32), pltpu.SemaphoreType.DMA((2,))]),
        out_shape=[jax.ShapeDtypeStruct((t, d), F32), jax.ShapeDtypeStruct((t, d), BF16)],
        compiler_params=pltpu.CompilerParams(dimension_semantics=("arbitrary",), vmem_limit_bytes=48 * MIB,
                                             disable_bounds_checks=True),
        name="moe_combine_ln",
    )(pos0, pos1, y, x1, gate, ln_g, ln_b)


def _rope_tables(pos):
    half = ROPE_DIM // 2
    inv = jnp.float32(ROPE_THETA) ** (-jnp.arange(half, dtype=F32) / half)
    ang = pos.astype(F32)[:, None] * inv[None, :]
    cos, sin = jnp.cos(ang), jnp.sin(ang)
    z = jnp.zeros_like(cos)
    return (jnp.concatenate([cos, cos, z, z], 1),
            jnp.concatenate([-sin, z, z, z], 1),
            jnp.concatenate([z, sin, z, z], 1))


def kernel(x_prompt, x_sample, cache_kv_latent, cache_k_rope, state_conv, w_in, q_norm_g, w_q_b, kv_norm_g, w_kv_b,
           conv_w, conv_b, conv_ln_g, conv_ln_b, w_conv_pw, w_o, ln1_g, ln1_b, w_gate, w_up, w_down, ln2_g, ln2_b,
           w_router, b_router):
    nb, seq, d = x_prompt.shape
    db, ds, _ = x_sample.shape
    depth = w_in.shape[0]
    past = cache_kv_latent.shape[2]
    d_conv = conv_w.shape[-1]
    rq, rkv = q_norm_g.shape[-1], kv_norm_g.shape[-1]
    tp, tsmp = nb * seq, db * ds
    t_all = tp + tsmp
    splits = (2 * d_conv, 2 * d_conv + rq, 2 * d_conv + rq + rkv, 2 * d_conv + rq + rkv + ROPE_DIM)
    alpha = (2.0 * depth) ** 0.25
    scale = math.log2(math.e) / math.sqrt(QK_NOPE_DIM + ROPE_DIM)
    tm = 512
    tq = 512
    n_tiles = 2 * t_all // MOE_TILE + N_EXPERTS
    assert t_all % MOE_TILE == 0 and n_tiles <= LANES
    assert d == N_HEADS * V_HEAD_DIM and w_kv_b.shape[-1] == N_HEADS * HEAD_PAD
    assert tp % tm == 0 and tsmp % tm == 0 and seq % tq == 0 and tq % CHUNK == 0
    assert past % CHUNK == 0 and ds <= CHUNK and ds >= CONV_WIDTH - 1

    pos = jnp.concatenate([jnp.tile(jnp.arange(seq, dtype=jnp.int32), nb),
                           jnp.tile(past + jnp.arange(ds, dtype=jnp.int32), db)])
    tabs = _rope_tables(pos)
    lane_chunk = jnp.arange(LANES, dtype=jnp.int32)[None, :] - ROPE_DIM
    row_chunk = jnp.where(jnp.arange(t_all) < tp, pos // CHUNK, LANES)[:, None]
    qmask = jnp.where((lane_chunk >= 0) & (lane_chunk > row_chunk) & (row_chunk < LANES), NEG_BIG, 0.0).astype(F32)
    kmask = jnp.where(lane_chunk == row_chunk, 1.0, 0.0).astype(F32)
    tabs_k, tabs_q = tabs + (kmask,), tabs + (qmask,)

    w_in_t = jnp.swapaxes(w_in, 1, 2)
    cache_pe_t = jnp.swapaxes(cache_k_rope, 2, 3)
    wq = w_q_b.reshape(depth, rq, N_HEADS, QK_NOPE_DIM + ROPE_DIM)
    wq = jnp.pad(wq, ((0, 0), (0, 0), (0, 0), (0, HEAD_PAD - QK_NOPE_DIM - ROPE_DIM)))
    wq = wq.reshape(depth, rq, N_HEADS * HEAD_PAD).astype(BF16)
    vec3 = lambda a: a.reshape(depth, 1, a.shape[-1])
    q_norm_g3, kv_norm_g3 = vec3(q_norm_g), vec3(kv_norm_g)
    conv_b3, conv_ln_g3, conv_ln_b3 = vec3(conv_b), vec3(conv_ln_g), vec3(conv_ln_b)
    ln1_g3, ln1_b3, ln2_g3, ln2_b3 = vec3(ln1_g), vec3(ln1_b), vec3(ln2_g), vec3(ln2_b)
    wr_t = w_router.T
    w_o_bf = w_o.astype(BF16)
    br = b_router.reshape(N_EXPERTS, 1)

    x = jnp.concatenate([x_prompt.reshape(tp, d), x_sample.reshape(tsmp, d)], axis=0)
    x_bf = x.astype(BF16)

    spare = jnp.zeros((n_tiles * MOE_TILE, d), F32)
    keep = CONV_WIDTH - 1
    outs = [[] for _ in range(6)]
    for l in range(depth):
        u = _glu_call(x_bf, w_in_t, l, d_conv, tm, 512)
        qn, ckv, kpe, kpad = _latent_call(x_bf, w_in_t, q_norm_g3, kv_norm_g3, tabs_k, l, splits, rq, rkv, tm)
        gates = _gates_call(x_bf, w_in_t, l, splits[3], tm, 1024)
        ga_p = _conv_prompt_call(u, gates, conv_w, conv_b3, conv_ln_g3, conv_ln_b3, w_conv_pw, l, nb, seq, 512, 64)
        ga_s = _conv_sample_call(u, gates, state_conv, conv_w, conv_b3, conv_ln_g3, conv_ln_b3, w_conv_pw,
                                 l, tp, db, ds, ds)
        q = _q_call(qn, wq, tabs_q, l, tm, scale)
        k, vt = _kv_call(ckv, kpad, w_kv_b, l, tp, tm)
        mixed_p = _attn_prompt_call(q, k, vt, ga_p, gates, nb, seq, tq)
        mixed_s = _attn_sample_call(q, cache_kv_latent, cache_pe_t, ckv, kpe, w_kv_b, ga_s, gates, l, tp, db, ds)
        x1, gate, eidx = _wo_call(mixed_p, mixed_s, x, w_o_bf, ln1_g3, ln1_b3, wr_t, br, l, tm, alpha)
        pos_rows, meta = _dispatch_call(eidx)
        pos0, pos1 = pos_rows[0], pos_rows[1]
        tile_meta = (meta[0, :n_tiles], meta[1, :n_tiles], meta[2, :1], meta[3, :n_tiles], meta[4, :n_tiles])
        xs = _scatter_call(pos0, pos1, x1, spare, tm)
        y = _experts_call(tile_meta, xs, w_gate, w_up, w_down, l)
        x, x_bf = _combine_call(pos0, pos1, y, x1, gate, ln2_g3, ln2_b3, l, MOE_TILE, alpha)
        spare = y
        outs[0].append(ckv[:tp].reshape(nb, seq, rkv))
        outs[1].append(kpe[:tp].reshape(nb, seq, ROPE_DIM))
        outs[2].append(u[:tp].reshape(nb, seq, d_conv)[:, seq - keep:])
        outs[3].append(ckv[tp:].reshape(db, ds, rkv))
        outs[4].append(kpe[tp:].reshape(db, ds, ROPE_DIM))
        outs[5].append(u[tp:].reshape(db, ds, d_conv)[:, ds - keep:])

    return (x[:tp].reshape(nb, seq, d), x[tp:].reshape(db, ds, d)) + tuple(jnp.stack(o) for o in outs)
```

```python
import functools
import math

import jax
import jax.numpy as jnp
from jax import lax
from jax.experimental import pallas as pl
from jax.experimental.pallas import tpu as pltpu

F32 = jnp.float32
BF16 = jnp.bfloat16

CHUNK = 64
CONV_WIDTH = 31
N_HEADS = 16
QK_NOPE_DIM = 128
ROPE_DIM = 64
V_HEAD_DIM = 128
ROPE_THETA = 10000.0
N_EXPERTS = 16
N_EXPERT_GROUPS = 4
EXPERTS_PER_GROUP = N_EXPERTS // N_EXPERT_GROUPS
LN_EPS = 1e-5
RMS_EPS = 1e-6

LANES = 128
SUBLANES = 8
MOE_TILE = 256
PREFIX_CHUNK = 512
HEAD_PAD = 256
CONV_HALO = 32
CONV_OFF = CONV_HALO - (CONV_WIDTH - 1)
NEG_BIG = -1e30
MIB = 1024 * 1024


def _cparams(sem, vmem_mib):
    return pltpu.CompilerParams(dimension_semantics=sem, vmem_limit_bytes=vmem_mib * MIB)


def _dot(a, b):
    return jnp.dot(a, b, preferred_element_type=F32)


def _dot_nt(a, b):
    return lax.dot_general(a, b, (((1,), (1,)), ((), ())), preferred_element_type=F32)


def _layer_norm(y, g, b):
    mu = jnp.mean(y, axis=-1, keepdims=True)
    d = y - mu
    var = jnp.mean(d * d, axis=-1, keepdims=True)
    return d * lax.rsqrt(var + LN_EPS) * g + b


def _rms_norm(y, g):
    return y * lax.rsqrt(jnp.mean(y * y, axis=-1, keepdims=True) + RMS_EPS) * g


def _rope128(v, c, s1, s2):
    return v * c + pltpu.roll(v, 96, axis=1) * s1 + pltpu.roll(v, 32, axis=1) * s2


def _glu_kernel(x_ref, wa_ref, wg_ref, u_ref, wa_bf, wg_bf):
    @pl.when(pl.program_id(1) == 0)
    def _():
        wa_bf[...] = wa_ref[...].astype(BF16)
        wg_bf[...] = wg_ref[...].astype(BF16)

    x = x_ref[...]
    a = _dot_nt(x, wa_bf[...])
    g = _dot_nt(x, wg_bf[...])
    u_ref[...] = a * jax.nn.sigmoid(g)


def _glu_call(x_bf, w_in_t, l, d_conv, tm, tn):
    t, d = x_bf.shape
    nj = d_conv // tn
    return pl.pallas_call(
        _glu_kernel,
        grid=(nj, t // tm),
        in_specs=[pl.BlockSpec((tm, d), lambda j, i: (i, 0)),
                  pl.BlockSpec((None, tn, d), lambda j, i: (l, j, 0)),
                  pl.BlockSpec((None, tn, d), lambda j, i: (l, j + nj, 0))],
        out_specs=pl.BlockSpec((tm, tn), lambda j, i: (i, j)),
        out_shape=jax.ShapeDtypeStruct((t, d_conv), F32),
        scratch_shapes=[pltpu.VMEM((tn, d), BF16), pltpu.VMEM((tn, d), BF16)],
        compiler_params=_cparams(("arbitrary", "arbitrary"), 48),
        name="glu",
    )(x_bf, w_in_t, w_in_t)


def _latent_kernel(x_ref, wq_ref, wc_ref, wk_ref, qg_ref, cg_ref, c_ref, s1_ref, s2_ref, kmask_ref,
                   qn_ref, ckv_ref, kpe_ref, kpad_ref, wq_bf, wc_bf, wk_bf):
    @pl.when(pl.program_id(0) == 0)
    def _():
        wq_bf[...] = wq_ref[...].astype(BF16)
        wc_bf[...] = wc_ref[...].astype(BF16)
        wk_bf[...] = wk_ref[...].astype(BF16)

    x = x_ref[...]
    qn_ref[...] = _rms_norm(_dot_nt(x, wq_bf[...]), qg_ref[...]).astype(BF16)
    ckv_ref[...] = _rms_norm(_dot_nt(x, wc_bf[...]), cg_ref[...])
    kr = _dot_nt(x, wk_bf[...])
    k = _rope128(kr, c_ref[...], s1_ref[...], s2_ref[...])
    lane = lax.broadcasted_iota(jnp.int32, k.shape, 1)
    k = jnp.where(lane < ROPE_DIM, k, 0.0)
    kpe_ref[...] = k[:, :ROPE_DIM]
    kpad_ref[...] = (k + kmask_ref[...]).astype(BF16)


def _latent_call(x_bf, w_in_t, q_norm_g, kv_norm_g, tabs, l, splits, rq, rkv, tm):
    t, d = x_bf.shape
    assert splits[0] % rq == 0 and splits[1] % rkv == 0 and splits[2] % LANES == 0
    row = lambda i: (i, 0)
    return pl.pallas_call(
        _latent_kernel,
        grid=(t // tm,),
        in_specs=[pl.BlockSpec((tm, d), row),
                  pl.BlockSpec((None, rq, d), lambda i: (l, splits[0] // rq, 0)),
                  pl.BlockSpec((None, rkv, d), lambda i: (l, splits[1] // rkv, 0)),
                  pl.BlockSpec((None, LANES, d), lambda i: (l, splits[2] // LANES, 0)),
                  pl.BlockSpec((None, 1, rq), lambda i: (l, 0, 0)),
                  pl.BlockSpec((None, 1, rkv), lambda i: (l, 0, 0)),
                  pl.BlockSpec((tm, LANES), row),
                  pl.BlockSpec((tm, LANES), row),
                  pl.BlockSpec((tm, LANES), row),
                  pl.BlockSpec((tm, LANES), row)],
        out_specs=[pl.BlockSpec((tm, rq), row),
                   pl.BlockSpec((tm, rkv), row),
                   pl.BlockSpec((tm, ROPE_DIM), row),
                   pl.BlockSpec((tm, LANES), row)],
        out_shape=[jax.ShapeDtypeStruct((t, rq), BF16),
                   jax.ShapeDtypeStruct((t, rkv), F32),
                   jax.ShapeDtypeStruct((t, ROPE_DIM), F32),
                   jax.ShapeDtypeStruct((t, LANES), BF16)],
        scratch_shapes=[pltpu.VMEM((rq, d), BF16), pltpu.VMEM((rkv, d), BF16), pltpu.VMEM((LANES, d), BF16)],
        compiler_params=_cparams(("arbitrary",), 48),
        name="latent",
    )(x_bf, w_in_t, w_in_t, w_in_t, q_norm_g, kv_norm_g, *tabs)


def _gates_kernel(x_ref, w_hbm, o_ref, w_f32, w_bf, sem, *, l, row0, tn):
    j = pl.program_id(0)

    def block_copy(jj):
        return pltpu.make_async_copy(w_hbm.at[l, pl.ds(pl.multiple_of(row0 + jj * tn, SUBLANES), tn), :], w_f32, sem)

    @pl.when(pl.program_id(1) == 0)
    def _():
        @pl.when(j == 0)
        def _():
            block_copy(0).start()

        block_copy(j).wait()
        w_bf[...] = w_f32[...].astype(BF16)

        @pl.when(j + 1 < pl.num_programs(0))
        def _():
            block_copy(j + 1).start()

    o_ref[...] = jax.nn.sigmoid(_dot_nt(x_ref[...], w_bf[...])).astype(BF16)


def _gates_call(x_bf, w_in_t, l, row0, tm, tn):
    t, d = x_bf.shape
    n = w_in_t.shape[1] - row0
    assert n % tn == 0 and row0 % SUBLANES == 0
    return pl.pallas_call(
        functools.partial(_gates_kernel, l=l, row0=row0, tn=tn),
        grid=(n // tn, t // tm),
        in_specs=[pl.BlockSpec((tm, d), lambda j, i: (i, 0)),
                  pl.BlockSpec(memory_space=pl.ANY)],
        out_specs=pl.BlockSpec((tm, tn), lambda j, i: (i, j)),
        out_shape=jax.ShapeDtypeStruct((t, n), BF16),
        scratch_shapes=[pltpu.VMEM((tn, d), F32), pltpu.VMEM((tn, d), BF16), pltpu.SemaphoreType.DMA(())],
        compiler_params=_cparams(("arbitrary", "arbitrary"), 48),
        name="gates",
    )(x_bf, w_in_t)


def _conv_tail(ext_ref, y_ref, cw_ref, cb_ref, lg_ref, lb_ref, wpw_bf, gate_ref, ga_ref, ts, rows):
    c_dim = y_ref.shape[1]
    for c in range(c_dim // LANES):
        cs = slice(c * LANES, (c + 1) * LANES)

        def rbody(r, carry, cs=cs):
            r0 = pl.multiple_of(r * rows, rows)
            win = ext_ref[pl.ds(r0, rows + CONV_HALO), cs]
            acc = jnp.zeros((rows, LANES), F32)
            for b in range(SUBLANES):
                taps = [o for o in range(CONV_OFF, CONV_OFF + CONV_WIDTH) if o % SUBLANES == b]
                n_win = rows + CONV_HALO
                shifted = win if b == 0 else pltpu.roll(win, n_win - b, axis=0)
                for o in taps:
                    acc = acc + shifted[o - b:o - b + rows, :] * cw_ref[o - CONV_OFF:o - CONV_OFF + 1, cs]
            y_ref[pl.ds(r0, rows), cs] = acc
            return carry

        lax.fori_loop(0, ts // rows, rbody, 0)

    y = _layer_norm(y_ref[...] + cb_ref[...], lg_ref[...], lb_ref[...])
    z = (y * jax.nn.sigmoid(y)).astype(BF16)
    a_out = _dot(z, wpw_bf[...])
    ga_ref[...] = (gate_ref[...].astype(F32) * a_out).astype(BF16)


def _conv_prompt_kernel(u_ref, halo_ref, cw_ref, cb_ref, lg_ref, lb_ref, wpw_ref, gate_ref, ga_ref,
                        ext_ref, y_ref, wpw_bf, *, ts, rows):
    b, i = pl.program_id(0), pl.program_id(1)

    @pl.when((b == 0) & (i == 0))
    def _():
        wpw_bf[...] = wpw_ref[...].astype(BF16)

    @pl.when(i == 0)
    def _():
        ext_ref[0:CONV_HALO, :] = jnp.zeros((CONV_HALO, ext_ref.shape[1]), F32)

    @pl.when(i > 0)
    def _():
        ext_ref[0:CONV_HALO, :] = halo_ref[...]

    ext_ref[CONV_HALO:CONV_HALO + ts, :] = u_ref[...]
    _conv_tail(ext_ref, y_ref, cw_ref, cb_ref, lg_ref, lb_ref, wpw_bf, gate_ref, ga_ref, ts, rows)


def _conv_sample_kernel(u_ref, hist_ref, cw_ref, cb_ref, lg_ref, lb_ref, wpw_ref, gate_ref, ga_ref,
                        ext_ref, y_ref, wpw_bf, *, ts, rows):
    @pl.when(pl.program_id(0) == 0)
    def _():
        wpw_bf[...] = wpw_ref[...].astype(BF16)

    ext_ref[0:CONV_HALO, :] = jnp.zeros((CONV_HALO, ext_ref.shape[1]), F32)
    ext_ref[CONV_OFF:CONV_HALO, :] = hist_ref[...]
    ext_ref[CONV_HALO:CONV_HALO + ts, :] = u_ref[...]
    _conv_tail(ext_ref, y_ref, cw_ref, cb_ref, lg_ref, lb_ref, wpw_bf, gate_ref, ga_ref, ts, rows)


def _conv_weight_specs(l, c_dim, d, nargs):
    z = (lambda *a: (l, 0, 0))
    return [pl.BlockSpec((None, CONV_WIDTH, c_dim), z),
            pl.BlockSpec((None, 1, c_dim), z),
            pl.BlockSpec((None, 1, c_dim), z),
            pl.BlockSpec((None, 1, c_dim), z),
            pl.BlockSpec((None, c_dim, d), z)]


def _conv_prompt_call(u, gates, conv_w, conv_b, ln_g, ln_b, w_pw, l, nb, seq, ts, rows):
    c_dim = u.shape[1]
    d = w_pw.shape[-1]
    ns = seq // ts
    hb = ts // CONV_HALO
    tile = lambda b, i: (b * ns + i, 0)
    return pl.pallas_call(
        functools.partial(_conv_prompt_kernel, ts=ts, rows=rows),
        grid=(nb, ns),
        in_specs=[pl.BlockSpec((ts, c_dim), tile),
                  pl.BlockSpec((CONV_HALO, c_dim), lambda b, i: (jnp.maximum((b * ns + i) * hb - 1, 0), 0))]
        + _conv_weight_specs(l, c_dim, d, 2)
        + [pl.BlockSpec((ts, d), tile)],
        out_specs=pl.BlockSpec((ts, d), tile),
        out_shape=jax.ShapeDtypeStruct((nb * seq, d), BF16),
        scratch_shapes=[pltpu.VMEM((CONV_HALO + ts, c_dim), F32), pltpu.VMEM((ts, c_dim), F32),
                        pltpu.VMEM((c_dim, d), BF16)],
        compiler_params=_cparams(("arbitrary", "arbitrary"), 56),
        name="conv_prompt",
    )(u, u, conv_w, conv_b, ln_g, ln_b, w_pw, gates)


def _conv_sample_call(u, gates, state_conv, conv_w, conv_b, ln_g, ln_b, w_pw, l, row0, nb, ts, rows):
    c_dim = u.shape[1]
    d = w_pw.shape[-1]
    t0 = row0 // ts
    tile = lambda b: (t0 + b, 0)
    return pl.pallas_call(
        functools.partial(_conv_sample_kernel, ts=ts, rows=rows),
        grid=(nb,),
        in_specs=[pl.BlockSpec((ts, c_dim), tile),
                  pl.BlockSpec((None, None, CONV_WIDTH - 1, c_dim), lambda b: (l, b, 0, 0))]
        + _conv_weight_specs(l, c_dim, d, 1)
        + [pl.BlockSpec((ts, d), tile)],
        out_specs=pl.BlockSpec((ts, d), lambda b: (b, 0)),
        out_shape=jax.ShapeDtypeStruct((nb * ts, d), BF16),
        scratch_shapes=[pltpu.VMEM((CONV_HALO + ts, c_dim), F32), pltpu.VMEM((ts, c_dim), F32),
                        pltpu.VMEM((c_dim, d), BF16)],
        compiler_params=_cparams(("arbitrary",), 56),
        name="conv_sample",
    )(u, state_conv, conv_w, conv_b, ln_g, ln_b, w_pw, gates)


def _q_kernel(qn_ref, w_ref, c_ref, s1_ref, s2_ref, qmask_ref, q_ref, *, scale):
    qn = qn_ref[...]
    c, s1, s2, qmask = c_ref[...], s1_ref[...], s2_ref[...], qmask_ref[...]
    for h in range(N_HEADS):
        qh = _dot(qn, w_ref[:, h * HEAD_PAD:(h + 1) * HEAD_PAD])
        q_ref[:, h * HEAD_PAD:h * HEAD_PAD + LANES] = (qh[:, :LANES] * scale).astype(BF16)
        q_ref[:, h * HEAD_PAD + LANES:(h + 1) * HEAD_PAD] = (
            _rope128(qh[:, LANES:], c, s1, s2) * scale + qmask).astype(BF16)


def _q_call(qn, w_q, tabs, l, tm, scale):
    t, r = qn.shape
    n = w_q.shape[-1]
    row = lambda i: (i, 0)
    return pl.pallas_call(
        functools.partial(_q_kernel, scale=scale),
        grid=(t // tm,),
        in_specs=[pl.BlockSpec((tm, r), row),
                  pl.BlockSpec((None, r, n), lambda i: (l, 0, 0)),
                  pl.BlockSpec((tm, LANES), row), pl.BlockSpec((tm, LANES), row), pl.BlockSpec((tm, LANES), row),
                  pl.BlockSpec((tm, LANES), row)],
        out_specs=pl.BlockSpec((tm, n), row),
        out_shape=jax.ShapeDtypeStruct((t, n), BF16),
        compiler_params=_cparams(("arbitrary",), 48),
        name="q_proj",
    )(qn, w_q, *tabs)


def _kv_kernel(lat_ref, kpad_ref, w_ref, k_ref, vt_ref, w_bf):
    @pl.when(pl.program_id(0) == 0)
    def _():
        w_bf[...] = w_ref[...].astype(BF16)

    lat = lat_ref[...].astype(BF16)
    kpad = kpad_ref[...]
    for h in range(N_HEADS):
        kv = _dot(lat, w_bf[:, h * HEAD_PAD:(h + 1) * HEAD_PAD])
        k_ref[:, h * HEAD_PAD:h * HEAD_PAD + LANES] = kv[:, :LANES].astype(BF16)
        k_ref[:, h * HEAD_PAD + LANES:(h + 1) * HEAD_PAD] = kpad
        vt_ref[h * V_HEAD_DIM:(h + 1) * V_HEAD_DIM, :] = kv[:, LANES:].T.astype(BF16)


def _kv_call(ckv, kpad, w_kv_b, l, rows, tm):
    r = ckv.shape[1]
    n = w_kv_b.shape[-1]
    row = lambda i: (i, 0)
    return pl.pallas_call(
        _kv_kernel,
        grid=(rows // tm,),
        in_specs=[pl.BlockSpec((tm, r), row),
                  pl.BlockSpec((tm, LANES), row),
                  pl.BlockSpec((None, r, n), lambda i: (l, 0, 0))],
        out_specs=[pl.BlockSpec((tm, N_HEADS * HEAD_PAD), row),
                   pl.BlockSpec((N_HEADS * V_HEAD_DIM, tm), lambda i: (0, i))],
        out_shape=[jax.ShapeDtypeStruct((rows, N_HEADS * HEAD_PAD), BF16),
                   jax.ShapeDtypeStruct((N_HEADS * V_HEAD_DIM, rows), BF16)],
        scratch_shapes=[pltpu.VMEM((r, n), BF16)],
        compiler_params=_cparams(("arbitrary",), 48),
        name="kv_proj",
    )(ckv, kpad, w_kv_b)


def _attn_prompt_kernel(qi_ref, kj_ref, q_ref, k_ref, vt_ref, ga_ref, gb_ref, o_ref, *scratch,
                        n_steps, tq, unroll, n_buf, heads):
    per_head = 2 * n_buf + 3

    def head_refs(h):
        sc = scratch[h * per_head:(h + 1) * per_head]
        return [(sc[2 * j], sc[2 * j + 1]) for j in range(n_buf)], sc[2 * n_buf], sc[2 * n_buf + 1], sc[2 * n_buf + 2]

    def stage_scores(h, t, s_ref, mx_ref):
        q = q_ref[pl.ds(pl.multiple_of(qi_ref[t] * tq, tq), tq), h * HEAD_PAD:(h + 1) * HEAD_PAD]
        k = k_ref[pl.ds(pl.multiple_of(kj_ref[t] * tq, tq), tq), h * HEAD_PAD:(h + 1) * HEAD_PAD]
        s = _dot_nt(k, q)
        s_ref[...] = s
        mx_ref[...] = jnp.max(s, axis=0, keepdims=True)

    def step(h, t, j):
        bufs, acc_ref, m_ref, l_ref = head_refs(h)
        s_cur, mx_cur = bufs[j]
        stage_scores(h, t + n_buf - 1, *bufs[(j - 1) % n_buf])
        cols = pl.ds(pl.multiple_of(qi_ref[t] * tq, tq), tq)
        m_old = m_ref[:, cols]
        m_new = jnp.maximum(m_old, mx_cur[...])
        alpha = jnp.exp2(m_old - m_new)
        p = jnp.exp2(s_cur[...] - m_new)
        l_ref[:, cols] = alpha * l_ref[:, cols] + jnp.sum(p, axis=0, keepdims=True)
        keys = pl.ds(pl.multiple_of(kj_ref[t] * tq, tq), tq)
        pv = _dot(vt_ref[h * V_HEAD_DIM:(h + 1) * V_HEAD_DIM, keys], p.astype(BF16))
        acc_ref[:, cols] = alpha * acc_ref[:, cols] + pv
        m_ref[:, cols] = m_new

    for h in range(heads):
        bufs, acc_ref, m_ref, l_ref = head_refs(h)
        m_ref[...] = jnp.full(m_ref.shape, NEG_BIG, F32)
        l_ref[...] = jnp.zeros(l_ref.shape, F32)
        acc_ref[...] = jnp.zeros(acc_ref.shape, F32)
        for j in range(n_buf - 1):
            stage_scores(h, j, *bufs[j])

    def body(u, carry):
        for j in range(unroll):
            for h in range(heads):
                step(h, unroll * u + j, j % n_buf)
        return carry

    lax.fori_loop(0, n_steps // unroll, body, 0)

    def finish(qi, carry):
        rows = pl.ds(pl.multiple_of(qi * tq, tq), tq)
        for h in range(heads):
            _, acc_ref, _, l_ref = head_refs(h)
            cols = slice(h * V_HEAD_DIM, (h + 1) * V_HEAD_DIM)
            o = (acc_ref[:, rows] / l_ref[:, rows]).T
            o_ref[rows, cols] = (ga_ref[rows, cols].astype(F32) + gb_ref[rows, cols].astype(F32) * o).astype(BF16)
        return carry

    lax.fori_loop(0, o_ref.shape[0] // tq, finish, 0)


def _attn_prompt_call(q, k, vt, ga, gates, nb, seq, tq):
    d = N_HEADS * V_HEAD_DIM
    nq = seq // tq
    n_buf, unroll, heads = 2, 4, 1
    assert nq >= 2 and seq // CHUNK <= LANES - ROPE_DIM
    tiles = [(qi, kj) for qi in range(nq) for kj in range(qi + 1)]
    tiles += [(0, 1)] * (-len(tiles) % unroll)
    n_steps = len(tiles)
    tiles += [(0, 0)] * (n_buf - 1)
    qi_tab = jnp.asarray([t[0] for t in tiles], jnp.int32)
    kj_tab = jnp.asarray([t[1] for t in tiles], jnp.int32)
    bh = lambda b, h, qt, kt: (b, h)
    return pl.pallas_call(
        functools.partial(_attn_prompt_kernel, n_steps=n_steps, tq=tq, unroll=unroll, n_buf=n_buf, heads=heads),
        grid_spec=pltpu.PrefetchScalarGridSpec(
            num_scalar_prefetch=2, grid=(nb, N_HEADS // heads),
            in_specs=[pl.BlockSpec((seq, heads * HEAD_PAD), bh),
                      pl.BlockSpec((seq, heads * HEAD_PAD), bh),
                      pl.BlockSpec((heads * V_HEAD_DIM, seq), lambda b, h, qt, kt: (h, b)),
                      pl.BlockSpec((seq, heads * V_HEAD_DIM), bh),
                      pl.BlockSpec((seq, heads * V_HEAD_DIM), lambda b, h, qt, kt: (b, N_HEADS // heads + h))],
            out_specs=pl.BlockSpec((seq, heads * V_HEAD_DIM), bh),
            scratch_shapes=([pltpu.VMEM((tq, tq), F32), pltpu.VMEM((1, tq), F32)] * n_buf
                            + [pltpu.VMEM((V_HEAD_DIM, seq), F32), pltpu.VMEM((1, seq), F32),
                               pltpu.VMEM((1, seq), F32)]) * heads),
        out_shape=jax.ShapeDtypeStruct((nb * seq, d), BF16),
        compiler_params=_cparams(("arbitrary", "arbitrary"), 56),
        name="attn_prompt",
    )(qi_tab, kj_tab, q, k, vt, ga, gates)


def _attn_sample_kernel(q_ref, latp_ref, kpep_ref, latn_ref, kpen_ref, w_ref, ga_ref, gb_ref, o_ref,
                        w_bf, qlat_ref, qpe_ref, kpp_ref, kpn_ref, olat_ref, *, ds):
    @pl.when(pl.program_id(0) == 0)
    def _():
        w_bf[...] = w_ref[...].astype(BF16)
        kpp_ref[...] = jnp.zeros(kpp_ref.shape, BF16)
        kpn_ref[...] = jnp.zeros(kpn_ref.shape, BF16)

    kpp_ref[:ROPE_DIM, :] = kpep_ref[...].astype(BF16)
    kpn_ref[:, :ROPE_DIM] = kpen_ref[...].astype(BF16)
    for h in range(N_HEADS):
        rows = slice(h * ds, (h + 1) * ds)
        qn = q_ref[:, h * HEAD_PAD:h * HEAD_PAD + LANES]
        qlat_ref[rows, :] = _dot_nt(qn, w_bf[:, h * HEAD_PAD:h * HEAD_PAD + LANES]).astype(BF16)
        qpe_ref[rows, :] = q_ref[:, h * HEAD_PAD + LANES:(h + 1) * HEAD_PAD]

    lat_p = latp_ref[...].astype(BF16)
    lat_n = latn_ref[...].astype(BF16)
    qlat, qpe = qlat_ref[...], qpe_ref[...]
    s_p = _dot_nt(qlat, lat_p) + _dot(qpe, kpp_ref[...])
    s_n = _dot_nt(qlat, lat_n) + _dot_nt(qpe, kpn_ref[...])
    m = jnp.maximum(jnp.max(s_p, axis=1, keepdims=True), jnp.max(s_n, axis=1, keepdims=True))
    p_p = jnp.exp2(s_p - m)
    p_n = jnp.exp2(s_n - m)
    den = jnp.sum(p_p, axis=1, keepdims=True) + jnp.sum(p_n, axis=1, keepdims=True)
    o_lat = _dot(p_p.astype(BF16), lat_p) + _dot(p_n.astype(BF16), lat_n)
    olat_ref[...] = (o_lat / den).astype(BF16)
    for h in range(N_HEADS):
        cols = slice(h * V_HEAD_DIM, (h + 1) * V_HEAD_DIM)
        o = _dot(olat_ref[h * ds:(h + 1) * ds, :], w_bf[:, h * HEAD_PAD + LANES:(h + 1) * HEAD_PAD])
        o_ref[:, cols] = (ga_ref[:, cols].astype(F32) + gb_ref[:, cols].astype(F32) * o).astype(BF16)


def _attn_sample_call(q, cache_lat, cache_pe, ckv, kpe, w_kv_b, ga_s, gates, l, row0, nb, ds):
    past, r = cache_lat.shape[2], cache_lat.shape[3]
    d = N_HEADS * V_HEAD_DIM
    t0 = row0 // ds
    tile = lambda b: (t0 + b, 0)
    return pl.pallas_call(
        functools.partial(_attn_sample_kernel, ds=ds),
        grid=(nb,),
        in_specs=[pl.BlockSpec((ds, N_HEADS * HEAD_PAD), tile),
                  pl.BlockSpec((None, None, past, r), lambda b: (l, b, 0, 0)),
                  pl.BlockSpec((None, None, ROPE_DIM, past), lambda b: (l, b, 0, 0)),
                  pl.BlockSpec((ds, r), tile),
                  pl.BlockSpec((ds, ROPE_DIM), tile),
                  pl.BlockSpec((None, r, N_HEADS * HEAD_PAD), lambda b: (l, 0, 0)),
                  pl.BlockSpec((ds, d), lambda b: (b, 0)),
                  pl.BlockSpec((ds, d), lambda b: (t0 + b, 1))],
        out_specs=pl.BlockSpec((ds, d), lambda b: (b, 0)),
        out_shape=jax.ShapeDtypeStruct((nb * ds, d), BF16),
        scratch_shapes=[pltpu.VMEM((r, N_HEADS * HEAD_PAD), BF16),
                        pltpu.VMEM((N_HEADS * ds, r), BF16),
                        pltpu.VMEM((N_HEADS * ds, LANES), BF16),
                        pltpu.VMEM((LANES, past), BF16),
                        pltpu.VMEM((ds, LANES), BF16),
                        pltpu.VMEM((N_HEADS * ds, r), BF16)],
        compiler_params=_cparams(("arbitrary",), 56),
        name="attn_sample",
    )(q, cache_lat, cache_pe, ckv, kpe, w_kv_b, ga_s, gates)


def _route(scores, bias):
    sel = [s + b for s, b in zip(scores, bias)]
    n = EXPERTS_PER_GROUP
    grp = []
    for g in range(N_EXPERT_GROUPS):
        v = sel[g * n:(g + 1) * n]
        best = None
        for i in range(n):
            for j in range(i + 1, n):
                pair = v[i] + v[j]
                best = pair if best is None else jnp.maximum(best, pair)
        grp.append(best)
    g_idx = jnp.zeros_like(grp[0], dtype=jnp.int32)
    g_best = grp[0]
    for g in range(1, N_EXPERT_GROUPS):
        better = grp[g] > g_best
        g_idx = jnp.where(better, g, g_idx)
        g_best = jnp.where(better, grp[g], g_best)

    def pick(rows_by_group):
        out = rows_by_group[0]
        for g in range(1, N_EXPERT_GROUPS):
            out = jnp.where(g_idx == g, rows_by_group[g], out)
        return out

    in_sel = [pick([sel[g * n + j] for g in range(N_EXPERT_GROUPS)]) for j in range(n)]
    in_sc = [pick([scores[g * n + j] for g in range(N_EXPERT_GROUPS)]) for j in range(n)]

    def argmax_first(vals, excluded):
        idx = None
        best = None
        for j in range(n):
            v = vals[j] if excluded is None else jnp.where(excluded == j, -jnp.inf, vals[j])
            if best is None:
                best, idx = v, jnp.zeros_like(g_idx)
            else:
                better = v > best
                idx = jnp.where(better, j, idx)
                best = jnp.where(better, v, best)
        return idx

    l0 = argmax_first(in_sel, None)
    l1 = argmax_first(in_sel, l0)

    def take(vals, idx):
        out = vals[0]
        for j in range(1, n):
            out = jnp.where(idx == j, vals[j], out)
        return out

    w0, w1 = take(in_sc, l0), take(in_sc, l1)
    tot = w0 + w1
    return g_idx * n + l0, g_idx * n + l1, w0 / tot, w1 / tot


def _wo_kernel(mixp_ref, mixs_ref, x_ref, w_bf, g_ref, b_ref, wr_ref, br_ref, x1_ref, gate_ref, eidx_ref,
               *, alpha, n_prompt_tiles):
    mix = jnp.where(pl.program_id(0) < n_prompt_tiles, mixp_ref[...], mixs_ref[...])
    y = alpha * x_ref[...] + _dot(mix, w_bf[...])
    x1 = _layer_norm(y, g_ref[...], b_ref[...])
    x1_ref[...] = x1
    logits = lax.dot_general(wr_ref[...], x1, (((1,), (1,)), ((), ())), preferred_element_type=F32,
                             precision=lax.Precision.HIGHEST)
    sc = jax.nn.sigmoid(logits)
    br = br_ref[...]
    e0, e1, g0, g1 = _route([sc[e:e + 1, :] for e in range(N_EXPERTS)],
                            [br[e:e + 1, :] for e in range(N_EXPERTS)])
    rows = lax.broadcasted_iota(jnp.int32, (LANES, sc.shape[1]), 0)
    gate_ref[...] = jnp.where(rows == 0, g0, jnp.where(rows == 1, g1, 0.0)).T
    rows8 = lax.broadcasted_iota(jnp.int32, eidx_ref.shape, 0)
    eidx_ref[...] = jnp.where(rows8 == 0, e0, jnp.where(rows8 == 1, e1, 0))


def _wo_call(mixed_p, mixed_s, x, w_o, ln_g, ln_b, wr_t, br, l, tm, alpha):
    t, d = x.shape
    n_p = mixed_p.shape[0] // tm
    row = lambda i: (i, 0)
    vec = pl.BlockSpec((None, 1, d), lambda i: (l, 0, 0))
    return pl.pallas_call(
        functools.partial(_wo_kernel, alpha=alpha, n_prompt_tiles=n_p),
        grid=(t // tm,),
        in_specs=[pl.BlockSpec((tm, d), lambda i: (jnp.minimum(i, n_p - 1), 0)),
                  pl.BlockSpec((tm, d), lambda i: (jnp.maximum(i - n_p, 0), 0)),
                  pl.BlockSpec((tm, d), row),
                  pl.BlockSpec((None, d, d), lambda i: (l, 0, 0), pipeline_mode=pl.Buffered(1)),
                  vec, vec,
                  pl.BlockSpec((N_EXPERTS, d), lambda i: (0, 0)),
                  pl.BlockSpec((N_EXPERTS, 1), lambda i: (0, 0))],
        out_specs=[pl.BlockSpec((tm, d), row), pl.BlockSpec((tm, LANES), row),
                   pl.BlockSpec((SUBLANES, tm), lambda i: (0, i))],
        out_shape=[jax.ShapeDtypeStruct((t, d), F32), jax.ShapeDtypeStruct((t, LANES), F32),
                   jax.ShapeDtypeStruct((SUBLANES, t), jnp.int32)],
        compiler_params=_cparams(("arbitrary",), 56),
        name="wo_ln_router",
    )(mixed_p, mixed_s, x, w_o, ln_g, ln_b, wr_t, br)


def _dispatch_kernel(eidx_ref, pos_ref, meta_ref):
    t = eidx_ref.shape[1]
    e0, e1 = eidx_ref[0:1, :], eidx_ref[1:2, :]
    rows = lax.broadcasted_iota(jnp.int32, (N_EXPERTS, t), 0)
    hit0, hit1 = rows == e0, rows == e1
    oh = jnp.where(hit0 | hit1, 1.0, 0.0)
    cnt = jnp.sum(oh, axis=1, keepdims=True)
    padded = jnp.floor((cnt + (MOE_TILE - 1)) * (1.0 / MOE_TILE)) * MOE_TILE
    erow = lax.broadcasted_iota(jnp.int32, (N_EXPERTS, 1), 0)
    off = jnp.zeros((N_EXPERTS, 1), F32)
    run = jnp.zeros((1, 1), F32)
    for e in range(N_EXPERTS):
        off = jnp.where(erow == e, run, off)
        run = run + padded[e:e + 1, :]
    c = PREFIX_CHUNK
    before = jnp.where(lax.broadcasted_iota(jnp.int32, (c, c), 0) < lax.broadcasted_iota(jnp.int32, (c, c), 1),
                       1.0, 0.0).astype(BF16)
    pos_ref[...] = jnp.zeros(pos_ref.shape, jnp.int32)
    carry = off
    for j in range(t // c):
        cs = slice(j * c, (j + 1) * c)
        ohc = oh[:, cs]
        slot = _dot(ohc.astype(BF16), before) + carry
        pos_ref[0:1, cs] = jnp.sum(jnp.where(hit0[:, cs], slot, 0.0), axis=0, keepdims=True).astype(jnp.int32)
        pos_ref[1:2, cs] = jnp.sum(jnp.where(hit1[:, cs], slot, 0.0), axis=0, keepdims=True).astype(jnp.int32)
        carry = carry + jnp.sum(ohc, axis=1, keepdims=True)
    start = lax.broadcasted_iota(jnp.int32, (N_EXPERTS, LANES), 1).astype(F32) * MOE_TILE
    te = jnp.minimum(jnp.sum(jnp.where(off + padded <= start, 1.0, 0.0), axis=0, keepdims=True), N_EXPERTS - 1.0)
    mine = lax.broadcasted_iota(jnp.int32, (N_EXPERTS, LANES), 0).astype(F32) == te
    end_valid = jnp.sum(jnp.where(mine, off + cnt, 0.0), axis=0, keepdims=True)
    nvalid = jnp.clip(end_valid - start[0:1, :], 0.0, MOE_TILE)
    ordinal = jnp.zeros((N_EXPERTS, 1), F32)
    seen = jnp.zeros((1, 1), F32)
    for e in range(N_EXPERTS):
        ordinal = jnp.where(erow == e, seen, ordinal)
        seen = seen + jnp.where(cnt[e:e + 1, :] > 0.0, 1.0, 0.0)
    nxt = erow.astype(F32)
    later = jnp.full((1, 1), -1.0, F32)
    for e in reversed(range(N_EXPERTS)):
        nxt = jnp.where((erow == e) & (later >= 0.0), later, nxt)
        later = jnp.where(cnt[e:e + 1, :] > 0.0, float(e), later)
    parity = jnp.sum(jnp.where(mine, ordinal - 2.0 * jnp.floor(ordinal * 0.5), 0.0), axis=0, keepdims=True)
    nxt_tile = jnp.sum(jnp.where(mine, nxt, 0.0), axis=0, keepdims=True)
    r8 = lax.broadcasted_iota(jnp.int32, meta_ref.shape, 0)
    meta = jnp.zeros(meta_ref.shape, F32)
    for r, v in enumerate([te, nvalid, run * (1.0 / MOE_TILE), parity, nxt_tile]):
        meta = jnp.where(r8 == r, v, meta)
    meta_ref[...] = meta.astype(jnp.int32)


def _dispatch_call(eidx):
    t = eidx.shape[1]
    assert t % PREFIX_CHUNK == 0
    return pl.pallas_call(
        _dispatch_kernel,
        out_shape=[jax.ShapeDtypeStruct((SUBLANES, t), jnp.int32), jax.ShapeDtypeStruct((SUBLANES, LANES), jnp.int32)],
        compiler_params=pltpu.CompilerParams(vmem_limit_bytes=32 * MIB),
        name="moe_dispatch",
    )(eidx)


def _row_copy(src, i, dst, j, sem):
    return pltpu.make_async_copy(src.at[pl.ds(i, 1), :], dst.at[pl.ds(j, 1), :], sem)


def _scatter_kernel(p0_ref, p1_ref, x_ref, init_hbm, xs_hbm, sem, *, rows):
    del init_hbm
    base = pl.program_id(0) * rows

    def body(r, carry):
        t = base + r
        _row_copy(x_ref, r, xs_hbm, p0_ref[t], sem).start()
        _row_copy(x_ref, r, xs_hbm, p1_ref[t], sem).start()
        return carry

    lax.fori_loop(0, rows, body, 0, unroll=8)
    for _ in range(2):
        pltpu.make_async_copy(x_ref, xs_hbm.at[pl.ds(0, rows), :], sem).wait()


def _scatter_call(pos0, pos1, x1, init, rows):
    t, d = x1.shape
    return pl.pallas_call(
        functools.partial(_scatter_kernel, rows=rows),
        grid_spec=pltpu.PrefetchScalarGridSpec(
            num_scalar_prefetch=2, grid=(t // rows,),
            in_specs=[pl.BlockSpec((rows, d), lambda i, p0, p1: (i, 0)), pl.BlockSpec(memory_space=pl.ANY)],
            out_specs=pl.BlockSpec(memory_space=pl.ANY),
            scratch_shapes=[pltpu.SemaphoreType.DMA(())]),
        out_shape=jax.ShapeDtypeStruct(init.shape, init.dtype),
        input_output_aliases={3: 0},
        compiler_params=pltpu.CompilerParams(dimension_semantics=("arbitrary",), vmem_limit_bytes=32 * MIB,
                                             disable_bounds_checks=True),
        name="moe_scatter",
    )(pos0, pos1, x1, init)


def _experts_kernel(te_ref, nv_ref, nu_ref, slot_ref, nxt_ref, xs_ref, wg_hbm, wu_hbm, wd_hbm, y_ref,
                    wg_f, wu_f, wd_f, wg_bf, wu_bf, wd_bf, sems, *, l):
    i = pl.program_id(0)
    nv = nv_ref[i]
    e = te_ref[i]
    s = slot_ref[i]
    fresh = (i == 0) | (e != te_ref[jnp.maximum(i - 1, 0)])

    def weight_copies(expert, slot):
        return [pltpu.make_async_copy(w.at[l, expert], buf.at[slot], sems.at[slot])
                for w, buf in ((wg_hbm, wg_f), (wu_hbm, wu_f), (wd_hbm, wd_f))]

    @pl.when(fresh & (nv > 0))
    def _():
        @pl.when(i == 0)
        def _():
            for cp in weight_copies(e, s):
                cp.start()

        for cp in weight_copies(e, s):
            cp.wait()
        wg_bf[...] = wg_f[s].astype(BF16)
        wu_bf[...] = wu_f[s].astype(BF16)
        wd_bf[...] = wd_f[s].astype(BF16)

        @pl.when(nxt_ref[i] != e)
        def _():
            for cp in weight_copies(nxt_ref[i], 1 - s):
                cp.start()

    @pl.when(nv > 0)
    def _():
        row = lax.broadcasted_iota(jnp.int32, xs_ref.shape, 0)
        x = jnp.where(row < nv, xs_ref[...], 0.0).astype(BF16)
        hg = _dot(x, wg_bf[...])
        hu = _dot(x, wu_bf[...])
        h = hg * jax.nn.sigmoid(hg) * hu
        y_ref[...] = _dot(h.astype(BF16), wd_bf[...])

    @pl.when(nv == 0)
    def _():
        y_ref[...] = jnp.zeros(y_ref.shape, F32)


def _experts_call(meta, xs, w_gate, w_up, w_down, l):
    n_slots, d = xs.shape
    f = w_gate.shape[-1]
    n_tiles = n_slots // MOE_TILE
    any_spec = pl.BlockSpec(memory_space=pl.ANY)
    return pl.pallas_call(
        functools.partial(_experts_kernel, l=l),
        grid_spec=pltpu.PrefetchScalarGridSpec(
            num_scalar_prefetch=5, grid=(n_tiles,),
            in_specs=[pl.BlockSpec((MOE_TILE, d), lambda i, te, nv, nu, sl, nx: (jnp.minimum(i, nu[0] - 1), 0)),
                      any_spec, any_spec, any_spec],
            out_specs=pl.BlockSpec((MOE_TILE, d), lambda i, te, nv, nu, sl, nx: (i, 0)),
            scratch_shapes=[pltpu.VMEM((2, d, f), F32), pltpu.VMEM((2, d, f), F32), pltpu.VMEM((2, f, d), F32),
                            pltpu.VMEM((d, f), BF16), pltpu.VMEM((d, f), BF16), pltpu.VMEM((f, d), BF16),
                            pltpu.SemaphoreType.DMA((2,))]),
        out_shape=jax.ShapeDtypeStruct((n_slots, d), F32),
        compiler_params=_cparams(("arbitrary",), 56),
        name="moe_experts",
    )(*meta, xs, w_gate, w_up, w_down)


def _combine_kernel(p0_ref, p1_ref, y_hbm, x1_ref, gate_ref, g_ref, b_ref, out_a, out_b, buf, sems,
                    *, tm, alpha, n_prompt_tiles):
    i = pl.program_id(0)
    slot = i % 2

    def issue(tile, s):
        base = tile * tm

        def body(r, carry):
            t = base + r
            _row_copy(y_hbm, p0_ref[t], buf.at[s, 0], r, sems.at[s]).start()
            _row_copy(y_hbm, p1_ref[t], buf.at[s, 1], r, sems.at[s]).start()
            return carry

        lax.fori_loop(0, tm, body, 0, unroll=8)

    @pl.when(i == 0)
    def _():
        issue(0, 0)

    @pl.when(i + 1 < pl.num_programs(0))
    def _():
        issue(i + 1, 1 - slot)

    for k in range(2):
        pltpu.make_async_copy(y_hbm.at[pl.ds(0, tm), :], buf.at[slot, k], sems.at[slot]).wait()
    gate = gate_ref[...]
    moe = gate[:, 0:1] * buf[slot, 0] + gate[:, 1:2] * buf[slot, 1]
    x2 = _layer_norm(alpha * x1_ref[...] + moe, g_ref[...], b_ref[...])
    if n_prompt_tiles is None:
        out_a[...] = x2
        out_b[...] = x2.astype(BF16)
    else:
        @pl.when(i < n_prompt_tiles)
        def _():
            out_a[...] = x2

        @pl.when(i >= n_prompt_tiles)
        def _():
            out_b[...] = x2


def _combine_call(pos0, pos1, y, x1, gate, ln_g, ln_b, l, tm, alpha, prompt_rows=None):
    t, d = x1.shape
    row = lambda i, p0, p1: (i, 0)
    vec = pl.BlockSpec((None, 1, d), lambda i, p0, p1: (l, 0, 0))
    if prompt_rows is None:
        n_p = None
        out_specs = [pl.BlockSpec((tm, d), row), pl.BlockSpec((tm, d), row)]
        out_shape = [jax.ShapeDtypeStruct((t, d), F32), jax.ShapeDtypeStruct((t, d), BF16)]
    else:
        n_p = prompt_rows // tm
        out_specs = [pl.BlockSpec((tm, d), lambda i, p0, p1: (jnp.minimum(i, n_p - 1), 0)),
                     pl.BlockSpec((tm, d), lambda i, p0, p1: (jnp.maximum(i - n_p, 0), 0))]
        out_shape = [jax.ShapeDtypeStruct((prompt_rows, d), F32), jax.ShapeDtypeStruct((t - prompt_rows, d), F32)]
    return pl.pallas_call(
        functools.partial(_combine_kernel, tm=tm, alpha=alpha, n_prompt_tiles=n_p),
        grid_spec=pltpu.PrefetchScalarGridSpec(
            num_scalar_prefetch=2, grid=(t // tm,),
            in_specs=[pl.BlockSpec(memory_space=pl.ANY),
                      pl.BlockSpec((tm, d), row), pl.BlockSpec((tm, LANES), row), vec, vec],
            out_specs=out_specs,
            scratch_shapes=[pltpu.VMEM((2, 2, tm, d), F32), pltpu.SemaphoreType.DMA((2,))]),
        out_shape=out_shape,
        compiler_params=pltpu.CompilerParams(dimension_semantics=("arbitrary",), vmem_limit_bytes=48 * MIB,
                                             disable_bounds_checks=True),
        name="moe_combine_ln",
    )(pos0, pos1, y, x1, gate, ln_g, ln_b)


def _rope_tables(pos):
    half = ROPE_DIM // 2
    inv = jnp.float32(ROPE_THETA) ** (-jnp.arange(half, dtype=F32) / half)
    ang = pos.astype(F32)[:, None] * inv[None, :]
    cos, sin = jnp.cos(ang), jnp.sin(ang)
    z = jnp.zeros_like(cos)
    return (jnp.concatenate([cos, cos, z, z], 1),
            jnp.concatenate([-sin, z, z, z], 1),
            jnp.concatenate([z, sin, z, z], 1))


def kernel(x_prompt, x_sample, cache_kv_latent, cache_k_rope, state_conv, w_in, q_norm_g, w_q_b, kv_norm_g, w_kv_b,
           conv_w, conv_b, conv_ln_g, conv_ln_b, w_conv_pw, w_o, ln1_g, ln1_b, w_gate, w_up, w_down, ln2_g, ln2_b,
           w_router, b_router):
    nb, seq, d = x_prompt.shape
    db, ds, _ = x_sample.shape
    depth = w_in.shape[0]
    past = cache_kv_latent.shape[2]
    d_conv = conv_w.shape[-1]
    rq, rkv = q_norm_g.shape[-1], kv_norm_g.shape[-1]
    tp, tsmp = nb * seq, db * ds
    t_all = tp + tsmp
    splits = (2 * d_conv, 2 * d_conv + rq, 2 * d_conv + rq + rkv, 2 * d_conv + rq + rkv + ROPE_DIM)
    alpha = (2.0 * depth) ** 0.25
    scale = math.log2(math.e) / math.sqrt(QK_NOPE_DIM + ROPE_DIM)
    tm = 512
    tq = 512
    n_tiles = 2 * t_all // MOE_TILE + N_EXPERTS
    assert t_all % MOE_TILE == 0 and n_tiles <= LANES
    assert d == N_HEADS * V_HEAD_DIM and w_kv_b.shape[-1] == N_HEADS * HEAD_PAD
    assert tp % tm == 0 and tsmp % tm == 0 and seq % tq == 0 and tq % CHUNK == 0
    assert past % CHUNK == 0 and ds <= CHUNK and ds >= CONV_WIDTH - 1

    pos = jnp.concatenate([jnp.tile(jnp.arange(seq, dtype=jnp.int32), nb),
                           jnp.tile(past + jnp.arange(ds, dtype=jnp.int32), db)])
    tabs = _rope_tables(pos)
    lane_chunk = jnp.arange(LANES, dtype=jnp.int32)[None, :] - ROPE_DIM
    row_chunk = jnp.where(jnp.arange(t_all) < tp, pos // CHUNK, LANES)[:, None]
    qmask = jnp.where((lane_chunk >= 0) & (lane_chunk > row_chunk) & (row_chunk < LANES), NEG_BIG, 0.0).astype(F32)
    kmask = jnp.where(lane_chunk == row_chunk, 1.0, 0.0).astype(F32)
    tabs_k, tabs_q = tabs + (kmask,), tabs + (qmask,)

    w_in_t = jnp.swapaxes(w_in, 1, 2)
    cache_pe_t = jnp.swapaxes(cache_k_rope, 2, 3)
    wq = w_q_b.reshape(depth, rq, N_HEADS, QK_NOPE_DIM + ROPE_DIM)
    wq = jnp.pad(wq, ((0, 0), (0, 0), (0, 0), (0, HEAD_PAD - QK_NOPE_DIM - ROPE_DIM)))
    wq = wq.reshape(depth, rq, N_HEADS * HEAD_PAD).astype(BF16)
    vec3 = lambda a: a.reshape(depth, 1, a.shape[-1])
    q_norm_g3, kv_norm_g3 = vec3(q_norm_g), vec3(kv_norm_g)
    conv_b3, conv_ln_g3, conv_ln_b3 = vec3(conv_b), vec3(conv_ln_g), vec3(conv_ln_b)
    ln1_g3, ln1_b3, ln2_g3, ln2_b3 = vec3(ln1_g), vec3(ln1_b), vec3(ln2_g), vec3(ln2_b)
    wr_t = w_router.T
    w_o_bf = w_o.astype(BF16)
    br = b_router.reshape(N_EXPERTS, 1)

    x = jnp.concatenate([x_prompt.reshape(tp, d), x_sample.reshape(tsmp, d)], axis=0)
    x_bf = x.astype(BF16)

    spare = jnp.zeros((n_tiles * MOE_TILE, d), F32)
    keep = CONV_WIDTH - 1
    outs = [[] for _ in range(6)]
    for l in range(depth):
        u = _glu_call(x_bf, w_in_t, l, d_conv, tm, 512)
        qn, ckv, kpe, kpad = _latent_call(x_bf, w_in_t, q_norm_g3, kv_norm_g3, tabs_k, l, splits, rq, rkv, tm)
        gates = _gates_call(x_bf, w_in_t, l, splits[3], tm, 1024)
        ga_p = _conv_prompt_call(u, gates, conv_w, conv_b3, conv_ln_g3, conv_ln_b3, w_conv_pw, l, nb, seq, 512, 128)
        ga_s = _conv_sample_call(u, gates, state_conv, conv_w, conv_b3, conv_ln_g3, conv_ln_b3, w_conv_pw,
                                 l, tp, db, ds, ds)
        q = _q_call(qn, wq, tabs_q, l, tm, scale)
        k, vt = _kv_call(ckv, kpad, w_kv_b, l, tp, tm)
        mixed_p = _attn_prompt_call(q, k, vt, ga_p, gates, nb, seq, tq)
        mixed_s = _attn_sample_call(q, cache_kv_latent, cache_pe_t, ckv, kpe, w_kv_b, ga_s, gates, l, tp, db, ds)
        x1, gate, eidx = _wo_call(mixed_p, mixed_s, x, w_o_bf, ln1_g3, ln1_b3, wr_t, br, l, tm, alpha)
        pos_rows, meta = _dispatch_call(eidx)
        pos0, pos1 = pos_rows[0], pos_rows[1]
        tile_meta = (meta[0, :n_tiles], meta[1, :n_tiles], meta[2, :1], meta[3, :n_tiles], meta[4, :n_tiles])
        xs = _scatter_call(pos0, pos1, x1, spare, tm)
        y = _experts_call(tile_meta, xs, w_gate, w_up, w_down, l)
        if l + 1 < depth:
            x, x_bf = _combine_call(pos0, pos1, y, x1, gate, ln2_g3, ln2_b3, l, MOE_TILE, alpha)
        else:
            y_prompt, y_sample = _combine_call(pos0, pos1, y, x1, gate, ln2_g3, ln2_b3, l, MOE_TILE, alpha, tp)
        spare = y
        outs[0].append(ckv[:tp].reshape(nb, seq, rkv))
        outs[1].append(kpe[:tp].reshape(nb, seq, ROPE_DIM))
        outs[2].append(u[:tp].reshape(nb, seq, d_conv)[:, seq - keep:])
        outs[3].append(ckv[tp:].reshape(db, ds, rkv))
        outs[4].append(kpe[tp:].reshape(db, ds, ROPE_DIM))
        outs[5].append(u[tp:].reshape(db, ds, d_conv)[:, ds - keep:])

    return (y_prompt.reshape(nb, seq, d), y_sample.reshape(db, ds, d)) + tuple(jnp.stack(o) for o in outs)
```

```python
import functools
import math

import jax
import jax.numpy as jnp
from jax import lax
from jax.experimental import pallas as pl
from jax.experimental.pallas import tpu as pltpu

F32 = jnp.float32
BF16 = jnp.bfloat16

CHUNK = 64
CONV_WIDTH = 31
N_HEADS = 16
QK_NOPE_DIM = 128
ROPE_DIM = 64
V_HEAD_DIM = 128
ROPE_THETA = 10000.0
N_EXPERTS = 16
N_EXPERT_GROUPS = 4
EXPERTS_PER_GROUP = N_EXPERTS // N_EXPERT_GROUPS
LN_EPS = 1e-5
RMS_EPS = 1e-6

LANES = 128
SUBLANES = 8
MOE_TILE = 256
PREFIX_CHUNK = 512
HEAD_PAD = 256
CONV_HALO = 32
CONV_OFF = CONV_HALO - (CONV_WIDTH - 1)
NEG_BIG = -1e30
MIB = 1024 * 1024


def _cparams(sem, vmem_mib):
    return pltpu.CompilerParams(dimension_semantics=sem, vmem_limit_bytes=vmem_mib * MIB)


def _dot(a, b):
    return jnp.dot(a, b, preferred_element_type=F32)


def _dot_nt(a, b):
    return lax.dot_general(a, b, (((1,), (1,)), ((), ())), preferred_element_type=F32)


def _layer_norm(y, g, b):
    mu = jnp.mean(y, axis=-1, keepdims=True)
    d = y - mu
    var = jnp.mean(d * d, axis=-1, keepdims=True)
    return d * lax.rsqrt(var + LN_EPS) * g + b


def _rms_norm(y, g):
    return y * lax.rsqrt(jnp.mean(y * y, axis=-1, keepdims=True) + RMS_EPS) * g


def _sigmoid(x):
    return 0.5 * jnp.tanh(0.5 * x) + 0.5


def _rope128(v, c, s1, s2):
    return v * c + pltpu.roll(v, 96, axis=1) * s1 + pltpu.roll(v, 32, axis=1) * s2


def _glu_kernel(x_ref, wa_ref, wg_ref, u_ref, wa_bf, wg_bf):
    @pl.when(pl.program_id(1) == 0)
    def _():
        wa_bf[...] = wa_ref[...].astype(BF16)
        wg_bf[...] = wg_ref[...].astype(BF16)

    x = x_ref[...]
    a = _dot_nt(x, wa_bf[...])
    g = _dot_nt(x, wg_bf[...])
    u_ref[...] = a * _sigmoid(g)


def _glu_call(x_bf, w_in_t, l, d_conv, tm, tn):
    t, d = x_bf.shape
    nj = d_conv // tn
    return pl.pallas_call(
        _glu_kernel,
        grid=(nj, t // tm),
        in_specs=[pl.BlockSpec((tm, d), lambda j, i: (i, 0)),
                  pl.BlockSpec((None, tn, d), lambda j, i: (l, j, 0)),
                  pl.BlockSpec((None, tn, d), lambda j, i: (l, j + nj, 0))],
        out_specs=pl.BlockSpec((tm, tn), lambda j, i: (i, j)),
        out_shape=jax.ShapeDtypeStruct((t, d_conv), F32),
        scratch_shapes=[pltpu.VMEM((tn, d), BF16), pltpu.VMEM((tn, d), BF16)],
        compiler_params=_cparams(("arbitrary", "arbitrary"), 48),
        name="glu",
    )(x_bf, w_in_t, w_in_t)


def _latent_kernel(x_ref, wq_ref, wc_ref, wk_ref, qg_ref, cg_ref, c_ref, s1_ref, s2_ref, kmask_ref,
                   qn_ref, ckvp_ref, ckvs_ref, kpep_ref, kpes_ref, kpad_ref, wq_bf, wc_bf, wk_bf, *, n_prompt_tiles):
    i = pl.program_id(0)

    @pl.when(i == 0)
    def _():
        wq_bf[...] = wq_ref[...].astype(BF16)
        wc_bf[...] = wc_ref[...].astype(BF16)
        wk_bf[...] = wk_ref[...].astype(BF16)

    x = x_ref[...]
    qn_ref[...] = _rms_norm(_dot_nt(x, wq_bf[...]), qg_ref[...]).astype(BF16)
    ckv = _rms_norm(_dot_nt(x, wc_bf[...]), cg_ref[...])
    kr = _dot_nt(x, wk_bf[...])
    k = _rope128(kr, c_ref[...], s1_ref[...], s2_ref[...])
    lane = lax.broadcasted_iota(jnp.int32, k.shape, 1)
    k = jnp.where(lane < ROPE_DIM, k, 0.0)
    kpad_ref[...] = (k + kmask_ref[...]).astype(BF16)

    @pl.when(i < n_prompt_tiles)
    def _():
        ckvp_ref[...] = ckv
        kpep_ref[...] = k[:, :ROPE_DIM]

    @pl.when(i >= n_prompt_tiles)
    def _():
        ckvs_ref[...] = ckv
        kpes_ref[...] = k[:, :ROPE_DIM]


def _latent_call(x_bf, w_in_t, q_norm_g, kv_norm_g, tabs, l, splits, rq, rkv, tm, prompt_rows):
    t, d = x_bf.shape
    assert splits[0] % rq == 0 and splits[1] % rkv == 0 and splits[2] % LANES == 0
    n_p = prompt_rows // tm
    row = lambda i: (i, 0)
    prow = lambda i: (jnp.minimum(i, n_p - 1), 0)
    srow = lambda i: (jnp.maximum(i - n_p, 0), 0)
    return pl.pallas_call(
        functools.partial(_latent_kernel, n_prompt_tiles=n_p),
        grid=(t // tm,),
        in_specs=[pl.BlockSpec((tm, d), row),
                  pl.BlockSpec((None, rq, d), lambda i: (l, splits[0] // rq, 0)),
                  pl.BlockSpec((None, rkv, d), lambda i: (l, splits[1] // rkv, 0)),
                  pl.BlockSpec((None, LANES, d), lambda i: (l, splits[2] // LANES, 0)),
                  pl.BlockSpec((None, 1, rq), lambda i: (l, 0, 0)),
                  pl.BlockSpec((None, 1, rkv), lambda i: (l, 0, 0)),
                  pl.BlockSpec((tm, LANES), row),
                  pl.BlockSpec((tm, LANES), row),
                  pl.BlockSpec((tm, LANES), row),
                  pl.BlockSpec((tm, LANES), row)],
        out_specs=[pl.BlockSpec((tm, rq), row),
                   pl.BlockSpec((tm, rkv), prow), pl.BlockSpec((tm, rkv), srow),
                   pl.BlockSpec((tm, ROPE_DIM), prow), pl.BlockSpec((tm, ROPE_DIM), srow),
                   pl.BlockSpec((tm, LANES), row)],
        out_shape=[jax.ShapeDtypeStruct((t, rq), BF16),
                   jax.ShapeDtypeStruct((prompt_rows, rkv), F32), jax.ShapeDtypeStruct((t - prompt_rows, rkv), F32),
                   jax.ShapeDtypeStruct((prompt_rows, ROPE_DIM), F32),
                   jax.ShapeDtypeStruct((t - prompt_rows, ROPE_DIM), F32),
                   jax.ShapeDtypeStruct((t, LANES), BF16)],
        scratch_shapes=[pltpu.VMEM((rq, d), BF16), pltpu.VMEM((rkv, d), BF16), pltpu.VMEM((LANES, d), BF16)],
        compiler_params=_cparams(("arbitrary",), 48),
        name="latent",
    )(x_bf, w_in_t, w_in_t, w_in_t, q_norm_g, kv_norm_g, *tabs)


def _gates_kernel(x_ref, w_hbm, o_ref, w_f32, w_bf, sem, *, l, row0, tn):
    j = pl.program_id(0)

    def block_copy(jj):
        return pltpu.make_async_copy(w_hbm.at[l, pl.ds(pl.multiple_of(row0 + jj * tn, SUBLANES), tn), :], w_f32, sem)

    @pl.when(pl.program_id(1) == 0)
    def _():
        @pl.when(j == 0)
        def _():
            block_copy(0).start()

        block_copy(j).wait()
        w_bf[...] = w_f32[...].astype(BF16)

        @pl.when(j + 1 < pl.num_programs(0))
        def _():
            block_copy(j + 1).start()

    o_ref[...] = _sigmoid(_dot_nt(x_ref[...], w_bf[...])).astype(BF16)


def _gates_call(x_bf, w_in_t, l, row0, tm, tn):
    t, d = x_bf.shape
    n = w_in_t.shape[1] - row0
    assert n % tn == 0 and row0 % SUBLANES == 0
    return pl.pallas_call(
        functools.partial(_gates_kernel, l=l, row0=row0, tn=tn),
        grid=(n // tn, t // tm),
        in_specs=[pl.BlockSpec((tm, d), lambda j, i: (i, 0)),
                  pl.BlockSpec(memory_space=pl.ANY)],
        out_specs=pl.BlockSpec((tm, tn), lambda j, i: (i, j)),
        out_shape=jax.ShapeDtypeStruct((t, n), BF16),
        scratch_shapes=[pltpu.VMEM((tn, d), F32), pltpu.VMEM((tn, d), BF16), pltpu.SemaphoreType.DMA(())],
        compiler_params=_cparams(("arbitrary", "arbitrary"), 48),
        name="gates",
    )(x_bf, w_in_t)


def _conv_tail(ext_ref, y_ref, cw_ref, cb_ref, lg_ref, lb_ref, wpw_bf, gate_ref, ga_ref, ts, rows):
    c_dim = y_ref.shape[1]
    for c in range(c_dim // LANES):
        cs = slice(c * LANES, (c + 1) * LANES)

        def rbody(r, carry, cs=cs):
            r0 = pl.multiple_of(r * rows, rows)
            win = ext_ref[pl.ds(r0, rows + CONV_HALO), cs]
            acc = jnp.zeros((rows, LANES), F32)
            for b in range(SUBLANES):
                taps = [o for o in range(CONV_OFF, CONV_OFF + CONV_WIDTH) if o % SUBLANES == b]
                n_win = rows + CONV_HALO
                shifted = win if b == 0 else pltpu.roll(win, n_win - b, axis=0)
                for o in taps:
                    acc = acc + shifted[o - b:o - b + rows, :] * cw_ref[o - CONV_OFF:o - CONV_OFF + 1, cs]
            y_ref[pl.ds(r0, rows), cs] = acc
            return carry

        lax.fori_loop(0, ts // rows, rbody, 0)

    y = _layer_norm(y_ref[...] + cb_ref[...], lg_ref[...], lb_ref[...])
    z = (y * _sigmoid(y)).astype(BF16)
    a_out = _dot(z, wpw_bf[...])
    ga_ref[...] = (gate_ref[...].astype(F32) * a_out).astype(BF16)


def _conv_prompt_kernel(u_ref, halo_ref, cw_ref, cb_ref, lg_ref, lb_ref, wpw_ref, gate_ref, ga_ref,
                        ext_ref, y_ref, wpw_bf, *, ts, rows):
    b, i = pl.program_id(0), pl.program_id(1)

    @pl.when((b == 0) & (i == 0))
    def _():
        wpw_bf[...] = wpw_ref[...].astype(BF16)

    @pl.when(i == 0)
    def _():
        ext_ref[0:CONV_HALO, :] = jnp.zeros((CONV_HALO, ext_ref.shape[1]), F32)

    @pl.when(i > 0)
    def _():
        ext_ref[0:CONV_HALO, :] = halo_ref[...]

    ext_ref[CONV_HALO:CONV_HALO + ts, :] = u_ref[...]
    _conv_tail(ext_ref, y_ref, cw_ref, cb_ref, lg_ref, lb_ref, wpw_bf, gate_ref, ga_ref, ts, rows)


def _conv_sample_kernel(u_ref, hist_ref, cw_ref, cb_ref, lg_ref, lb_ref, wpw_ref, gate_ref, ga_ref,
                        ext_ref, y_ref, wpw_bf, *, ts, rows):
    @pl.when(pl.program_id(0) == 0)
    def _():
        wpw_bf[...] = wpw_ref[...].astype(BF16)

    ext_ref[0:CONV_HALO, :] = jnp.zeros((CONV_HALO, ext_ref.shape[1]), F32)
    ext_ref[CONV_OFF:CONV_HALO, :] = hist_ref[...]
    ext_ref[CONV_HALO:CONV_HALO + ts, :] = u_ref[...]
    _conv_tail(ext_ref, y_ref, cw_ref, cb_ref, lg_ref, lb_ref, wpw_bf, gate_ref, ga_ref, ts, rows)


def _conv_weight_specs(l, c_dim, d, nargs):
    z = (lambda *a: (l, 0, 0))
    return [pl.BlockSpec((None, CONV_WIDTH, c_dim), z),
            pl.BlockSpec((None, 1, c_dim), z),
            pl.BlockSpec((None, 1, c_dim), z),
            pl.BlockSpec((None, 1, c_dim), z),
            pl.BlockSpec((None, c_dim, d), z)]


def _conv_prompt_call(u, gates, conv_w, conv_b, ln_g, ln_b, w_pw, l, nb, seq, ts, rows):
    c_dim = u.shape[1]
    d = w_pw.shape[-1]
    ns = seq // ts
    hb = ts // CONV_HALO
    tile = lambda b, i: (b * ns + i, 0)
    return pl.pallas_call(
        functools.partial(_conv_prompt_kernel, ts=ts, rows=rows),
        grid=(nb, ns),
        in_specs=[pl.BlockSpec((ts, c_dim), tile),
                  pl.BlockSpec((CONV_HALO, c_dim), lambda b, i: (jnp.maximum((b * ns + i) * hb - 1, 0), 0))]
        + _conv_weight_specs(l, c_dim, d, 2)
        + [pl.BlockSpec((ts, d), tile)],
        out_specs=pl.BlockSpec((ts, d), tile),
        out_shape=jax.ShapeDtypeStruct((nb * seq, d), BF16),
        scratch_shapes=[pltpu.VMEM((CONV_HALO + ts, c_dim), F32), pltpu.VMEM((ts, c_dim), F32),
                        pltpu.VMEM((c_dim, d), BF16)],
        compiler_params=_cparams(("arbitrary", "arbitrary"), 56),
        name="conv_prompt",
    )(u, u, conv_w, conv_b, ln_g, ln_b, w_pw, gates)


def _conv_sample_call(u, gates, state_conv, conv_w, conv_b, ln_g, ln_b, w_pw, l, row0, nb, ts, rows):
    c_dim = u.shape[1]
    d = w_pw.shape[-1]
    t0 = row0 // ts
    tile = lambda b: (t0 + b, 0)
    return pl.pallas_call(
        functools.partial(_conv_sample_kernel, ts=ts, rows=rows),
        grid=(nb,),
        in_specs=[pl.BlockSpec((ts, c_dim), tile),
                  pl.BlockSpec((None, None, CONV_WIDTH - 1, c_dim), lambda b: (l, b, 0, 0))]
        + _conv_weight_specs(l, c_dim, d, 1)
        + [pl.BlockSpec((ts, d), tile)],
        out_specs=pl.BlockSpec((ts, d), lambda b: (b, 0)),
        out_shape=jax.ShapeDtypeStruct((nb * ts, d), BF16),
        scratch_shapes=[pltpu.VMEM((CONV_HALO + ts, c_dim), F32), pltpu.VMEM((ts, c_dim), F32),
                        pltpu.VMEM((c_dim, d), BF16)],
        compiler_params=_cparams(("arbitrary",), 56),
        name="conv_sample",
    )(u, state_conv, conv_w, conv_b, ln_g, ln_b, w_pw, gates)


def _q_kernel(qn_ref, w_ref, c_ref, s1_ref, s2_ref, qmask_ref, q_ref, *, scale):
    qn = qn_ref[...]
    c, s1, s2, qmask = c_ref[...], s1_ref[...], s2_ref[...], qmask_ref[...]
    for h in range(N_HEADS):
        qh = _dot(qn, w_ref[:, h * HEAD_PAD:(h + 1) * HEAD_PAD])
        q_ref[:, h * HEAD_PAD:h * HEAD_PAD + LANES] = (qh[:, :LANES] * scale).astype(BF16)
        q_ref[:, h * HEAD_PAD + LANES:(h + 1) * HEAD_PAD] = (
            _rope128(qh[:, LANES:], c, s1, s2) * scale + qmask).astype(BF16)


def _q_call(qn, w_q, tabs, l, tm, scale):
    t, r = qn.shape
    n = w_q.shape[-1]
    row = lambda i: (i, 0)
    return pl.pallas_call(
        functools.partial(_q_kernel, scale=scale),
        grid=(t // tm,),
        in_specs=[pl.BlockSpec((tm, r), row),
                  pl.BlockSpec((None, r, n), lambda i: (l, 0, 0)),
                  pl.BlockSpec((tm, LANES), row), pl.BlockSpec((tm, LANES), row), pl.BlockSpec((tm, LANES), row),
                  pl.BlockSpec((tm, LANES), row)],
        out_specs=pl.BlockSpec((tm, n), row),
        out_shape=jax.ShapeDtypeStruct((t, n), BF16),
        compiler_params=_cparams(("arbitrary",), 48),
        name="q_proj",
    )(qn, w_q, *tabs)


def _kv_kernel(lat_ref, kpad_ref, w_ref, k_ref, vt_ref, w_bf):
    @pl.when(pl.program_id(0) == 0)
    def _():
        w_bf[...] = w_ref[...].astype(BF16)

    lat = lat_ref[...].astype(BF16)
    kpad = kpad_ref[...]
    for h in range(N_HEADS):
        kv = _dot(lat, w_bf[:, h * HEAD_PAD:(h + 1) * HEAD_PAD])
        k_ref[:, h * HEAD_PAD:h * HEAD_PAD + LANES] = kv[:, :LANES].astype(BF16)
        k_ref[:, h * HEAD_PAD + LANES:(h + 1) * HEAD_PAD] = kpad
        vt_ref[h * V_HEAD_DIM:(h + 1) * V_HEAD_DIM, :] = kv[:, LANES:].T.astype(BF16)


def _kv_call(ckv, kpad, w_kv_b, l, rows, tm):
    r = ckv.shape[1]
    n = w_kv_b.shape[-1]
    row = lambda i: (i, 0)
    return pl.pallas_call(
        _kv_kernel,
        grid=(rows // tm,),
        in_specs=[pl.BlockSpec((tm, r), row),
                  pl.BlockSpec((tm, LANES), row),
                  pl.BlockSpec((None, r, n), lambda i: (l, 0, 0))],
        out_specs=[pl.BlockSpec((tm, N_HEADS * HEAD_PAD), row),
                   pl.BlockSpec((N_HEADS * V_HEAD_DIM, tm), lambda i: (0, i))],
        out_shape=[jax.ShapeDtypeStruct((rows, N_HEADS * HEAD_PAD), BF16),
                   jax.ShapeDtypeStruct((N_HEADS * V_HEAD_DIM, rows), BF16)],
        scratch_shapes=[pltpu.VMEM((r, n), BF16)],
        compiler_params=_cparams(("arbitrary",), 48),
        name="kv_proj",
    )(ckv, kpad, w_kv_b)


def _attn_prompt_kernel(qi_ref, kj_ref, q_ref, k_ref, vt_ref, ga_ref, gb_ref, o_ref, *scratch,
                        n_steps, tq, tk, unroll, n_buf, heads):
    per_head = 2 * n_buf + 3

    def head_refs(h):
        sc = scratch[h * per_head:(h + 1) * per_head]
        return [(sc[2 * j], sc[2 * j + 1]) for j in range(n_buf)], sc[2 * n_buf], sc[2 * n_buf + 1], sc[2 * n_buf + 2]

    def stage_scores(h, t, s_ref, mx_ref):
        q = q_ref[pl.ds(pl.multiple_of(qi_ref[t] * tq, tq), tq), h * HEAD_PAD:(h + 1) * HEAD_PAD]
        k = k_ref[pl.ds(pl.multiple_of(kj_ref[t] * tk, tk), tk), h * HEAD_PAD:(h + 1) * HEAD_PAD]
        s = _dot_nt(k, q)
        s_ref[...] = s
        mx_ref[...] = jnp.max(s, axis=0, keepdims=True)

    def step(h, t, j):
        bufs, acc_ref, m_ref, l_ref = head_refs(h)
        s_cur, mx_cur = bufs[j]
        stage_scores(h, t + n_buf - 1, *bufs[(j - 1) % n_buf])
        cols = pl.ds(pl.multiple_of(qi_ref[t] * tq, tq), tq)
        m_old = m_ref[:, cols]
        m_new = jnp.maximum(m_old, mx_cur[...])
        alpha = jnp.exp2(m_old - m_new)
        p = jnp.exp2(s_cur[...] - m_new)
        l_ref[:, cols] = alpha * l_ref[:, cols] + jnp.sum(p, axis=0, keepdims=True)
        keys = pl.ds(pl.multiple_of(kj_ref[t] * tk, tk), tk)
        pv = _dot(vt_ref[h * V_HEAD_DIM:(h + 1) * V_HEAD_DIM, keys], p.astype(BF16))
        acc_ref[:, cols] = alpha * acc_ref[:, cols] + pv
        m_ref[:, cols] = m_new

    for h in range(heads):
        bufs, acc_ref, m_ref, l_ref = head_refs(h)
        m_ref[...] = jnp.full(m_ref.shape, NEG_BIG, F32)
        l_ref[...] = jnp.zeros(l_ref.shape, F32)
        acc_ref[...] = jnp.zeros(acc_ref.shape, F32)
        for j in range(n_buf - 1):
            stage_scores(h, j, *bufs[j])

    def body(u, carry):
        for j in range(unroll):
            for h in range(heads):
                step(h, unroll * u + j, j % n_buf)
        return carry

    lax.fori_loop(0, n_steps // unroll, body, 0)

    def finish(qi, carry):
        rows = pl.ds(pl.multiple_of(qi * tq, tq), tq)
        for h in range(heads):
            _, acc_ref, _, l_ref = head_refs(h)
            cols = slice(h * V_HEAD_DIM, (h + 1) * V_HEAD_DIM)
            o = (acc_ref[:, rows] / l_ref[:, rows]).T
            o_ref[rows, cols] = (ga_ref[rows, cols].astype(F32) + gb_ref[rows, cols].astype(F32) * o).astype(BF16)
        return carry

    lax.fori_loop(0, o_ref.shape[0] // tq, finish, 0)


def _attn_prompt_call(q, k, vt, ga, gates, nb, seq, tq, tk):
    d = N_HEADS * V_HEAD_DIM
    nq = seq // tq
    n_buf, unroll, heads = 2, 4, 1
    assert nq >= 2 and seq // CHUNK <= LANES - ROPE_DIM and seq % tk == 0
    tiles = [(qi, kj) for qi in range(nq) for kj in range(-(-(qi + 1) * tq // tk))]
    masked_tile = (0, -(-tq // tk))
    assert (masked_tile[1] + 1) * tk <= seq
    tiles += [masked_tile] * (-len(tiles) % unroll)
    n_steps = len(tiles)
    tiles += [(0, 0)] * (n_buf - 1)
    qi_tab = jnp.asarray([t[0] for t in tiles], jnp.int32)
    kj_tab = jnp.asarray([t[1] for t in tiles], jnp.int32)
    bh = lambda b, h, qt, kt: (b, h)
    return pl.pallas_call(
        functools.partial(_attn_prompt_kernel, n_steps=n_steps, tq=tq, tk=tk, unroll=unroll, n_buf=n_buf,
                          heads=heads),
        grid_spec=pltpu.PrefetchScalarGridSpec(
            num_scalar_prefetch=2, grid=(nb, N_HEADS // heads),
            in_specs=[pl.BlockSpec((seq, heads * HEAD_PAD), bh),
                      pl.BlockSpec((seq, heads * HEAD_PAD), bh),
                      pl.BlockSpec((heads * V_HEAD_DIM, seq), lambda b, h, qt, kt: (h, b)),
                      pl.BlockSpec((seq, heads * V_HEAD_DIM), bh),
                      pl.BlockSpec((seq, heads * V_HEAD_DIM), lambda b, h, qt, kt: (b, N_HEADS // heads + h))],
            out_specs=pl.BlockSpec((seq, heads * V_HEAD_DIM), bh),
            scratch_shapes=([pltpu.VMEM((tk, tq), F32), pltpu.VMEM((1, tq), F32)] * n_buf
                            + [pltpu.VMEM((V_HEAD_DIM, seq), F32), pltpu.VMEM((1, seq), F32),
                               pltpu.VMEM((1, seq), F32)]) * heads),
        out_shape=jax.ShapeDtypeStruct((nb * seq, d), BF16),
        compiler_params=_cparams(("arbitrary", "arbitrary"), 56),
        name="attn_prompt",
    )(qi_tab, kj_tab, q, k, vt, ga, gates)


def _attn_sample_kernel(q_ref, latp_ref, kpep_ref, latn_ref, kpen_ref, w_ref, ga_ref, gb_ref, o_ref,
                        w_bf, qlat_ref, qpe_ref, kpp_ref, kpn_ref, olat_ref, *, ds):
    @pl.when(pl.program_id(0) == 0)
    def _():
        w_bf[...] = w_ref[...].astype(BF16)
        kpp_ref[...] = jnp.zeros(kpp_ref.shape, BF16)
        kpn_ref[...] = jnp.zeros(kpn_ref.shape, BF16)

    kpp_ref[:ROPE_DIM, :] = kpep_ref[...].astype(BF16)
    kpn_ref[:, :ROPE_DIM] = kpen_ref[...].astype(BF16)
    for h in range(N_HEADS):
        rows = slice(h * ds, (h + 1) * ds)
        qn = q_ref[:, h * HEAD_PAD:h * HEAD_PAD + LANES]
        qlat_ref[rows, :] = _dot_nt(qn, w_bf[:, h * HEAD_PAD:h * HEAD_PAD + LANES]).astype(BF16)
        qpe_ref[rows, :] = q_ref[:, h * HEAD_PAD + LANES:(h + 1) * HEAD_PAD]

    lat_p = latp_ref[...].astype(BF16)
    lat_n = latn_ref[...].astype(BF16)
    qlat, qpe = qlat_ref[...], qpe_ref[...]
    s_p = _dot_nt(qlat, lat_p) + _dot(qpe, kpp_ref[...])
    s_n = _dot_nt(qlat, lat_n) + _dot_nt(qpe, kpn_ref[...])
    m = jnp.maximum(jnp.max(s_p, axis=1, keepdims=True), jnp.max(s_n, axis=1, keepdims=True))
    p_p = jnp.exp2(s_p - m)
    p_n = jnp.exp2(s_n - m)
    den = jnp.sum(p_p, axis=1, keepdims=True) + jnp.sum(p_n, axis=1, keepdims=True)
    o_lat = _dot(p_p.astype(BF16), lat_p) + _dot(p_n.astype(BF16), lat_n)
    olat_ref[...] = (o_lat / den).astype(BF16)
    for h in range(N_HEADS):
        cols = slice(h * V_HEAD_DIM, (h + 1) * V_HEAD_DIM)
        o = _dot(olat_ref[h * ds:(h + 1) * ds, :], w_bf[:, h * HEAD_PAD + LANES:(h + 1) * HEAD_PAD])
        o_ref[:, cols] = (ga_ref[:, cols].astype(F32) + gb_ref[:, cols].astype(F32) * o).astype(BF16)


def _attn_sample_call(q, cache_lat, cache_pe, ckv_s, kpe_s, w_kv_b, ga_s, gates, l, row0, nb, ds):
    past, r = cache_lat.shape[2], cache_lat.shape[3]
    d = N_HEADS * V_HEAD_DIM
    t0 = row0 // ds
    tile = lambda b: (t0 + b, 0)
    return pl.pallas_call(
        functools.partial(_attn_sample_kernel, ds=ds),
        grid=(nb,),
        in_specs=[pl.BlockSpec((ds, N_HEADS * HEAD_PAD), tile),
                  pl.BlockSpec((None, None, past, r), lambda b: (l, b, 0, 0)),
                  pl.BlockSpec((None, None, ROPE_DIM, past), lambda b: (l, b, 0, 0)),
                  pl.BlockSpec((ds, r), lambda b: (b, 0)),
                  pl.BlockSpec((ds, ROPE_DIM), lambda b: (b, 0)),
                  pl.BlockSpec((None, r, N_HEADS * HEAD_PAD), lambda b: (l, 0, 0)),
                  pl.BlockSpec((ds, d), lambda b: (b, 0)),
                  pl.BlockSpec((ds, d), lambda b: (t0 + b, 1))],
        out_specs=pl.BlockSpec((ds, d), lambda b: (b, 0)),
        out_shape=jax.ShapeDtypeStruct((nb * ds, d), BF16),
        scratch_shapes=[pltpu.VMEM((r, N_HEADS * HEAD_PAD), BF16),
                        pltpu.VMEM((N_HEADS * ds, r), BF16),
                        pltpu.VMEM((N_HEADS * ds, LANES), BF16),
                        pltpu.VMEM((LANES, past), BF16),
                        pltpu.VMEM((ds, LANES), BF16),
                        pltpu.VMEM((N_HEADS * ds, r), BF16)],
        compiler_params=_cparams(("arbitrary",), 56),
        name="attn_sample",
    )(q, cache_lat, cache_pe, ckv_s, kpe_s, w_kv_b, ga_s, gates)


def _route(scores, bias):
    sel = [s + b for s, b in zip(scores, bias)]
    n = EXPERTS_PER_GROUP
    grp = []
    for g in range(N_EXPERT_GROUPS):
        v = sel[g * n:(g + 1) * n]
        best = None
        for i in range(n):
            for j in range(i + 1, n):
                pair = v[i] + v[j]
                best = pair if best is None else jnp.maximum(best, pair)
        grp.append(best)
    g_idx = jnp.zeros_like(grp[0], dtype=jnp.int32)
    g_best = grp[0]
    for g in range(1, N_EXPERT_GROUPS):
        better = grp[g] > g_best
        g_idx = jnp.where(better, g, g_idx)
        g_best = jnp.where(better, grp[g], g_best)

    def pick(rows_by_group):
        out = rows_by_group[0]
        for g in range(1, N_EXPERT_GROUPS):
            out = jnp.where(g_idx == g, rows_by_group[g], out)
        return out

    in_sel = [pick([sel[g * n + j] for g in range(N_EXPERT_GROUPS)]) for j in range(n)]
    in_sc = [pick([scores[g * n + j] for g in range(N_EXPERT_GROUPS)]) for j in range(n)]

    def argmax_first(vals, excluded):
        idx = None
        best = None
        for j in range(n):
            v = vals[j] if excluded is None else jnp.where(excluded == j, -jnp.inf, vals[j])
            if best is None:
                best, idx = v, jnp.zeros_like(g_idx)
            else:
                better = v > best
                idx = jnp.where(better, j, idx)
                best = jnp.where(better, v, best)
        return idx

    l0 = argmax_first(in_sel, None)
    l1 = argmax_first(in_sel, l0)

    def take(vals, idx):
        out = vals[0]
        for j in range(1, n):
            out = jnp.where(idx == j, vals[j], out)
        return out

    w0, w1 = take(in_sc, l0), take(in_sc, l1)
    tot = w0 + w1
    return g_idx * n + l0, g_idx * n + l1, w0 / tot, w1 / tot


def _wo_kernel(mixp_ref, mixs_ref, x_ref, w_bf, g_ref, b_ref, wr_ref, br_ref, x1_ref, gate_ref, eidx_ref,
               *, alpha, n_prompt_tiles):
    mix = jnp.where(pl.program_id(0) < n_prompt_tiles, mixp_ref[...], mixs_ref[...])
    y = alpha * x_ref[...] + _dot(mix, w_bf[...])
    x1 = _layer_norm(y, g_ref[...], b_ref[...])
    x1_ref[...] = x1
    logits = lax.dot_general(wr_ref[...], x1, (((1,), (1,)), ((), ())), preferred_element_type=F32,
                             precision=lax.Precision.HIGHEST)
    sc = jax.nn.sigmoid(logits)
    br = br_ref[...]
    e0, e1, g0, g1 = _route([sc[e:e + 1, :] for e in range(N_EXPERTS)],
                            [br[e:e + 1, :] for e in range(N_EXPERTS)])
    rows = lax.broadcasted_iota(jnp.int32, (LANES, sc.shape[1]), 0)
    gate_ref[...] = jnp.where(rows == 0, g0, jnp.where(rows == 1, g1, 0.0)).T
    rows8 = lax.broadcasted_iota(jnp.int32, eidx_ref.shape, 0)
    eidx_ref[...] = jnp.where(rows8 == 0, e0, jnp.where(rows8 == 1, e1, 0))


def _wo_call(mixed_p, mixed_s, x, w_o, ln_g, ln_b, wr_t, br, l, tm, alpha):
    t, d = x.shape
    n_p = mixed_p.shape[0] // tm
    row = lambda i: (i, 0)
    vec = pl.BlockSpec((None, 1, d), lambda i: (l, 0, 0))
    return pl.pallas_call(
        functools.partial(_wo_kernel, alpha=alpha, n_prompt_tiles=n_p),
        grid=(t // tm,),
        in_specs=[pl.BlockSpec((tm, d), lambda i: (jnp.minimum(i, n_p - 1), 0)),
                  pl.BlockSpec((tm, d), lambda i: (jnp.maximum(i - n_p, 0), 0)),
                  pl.BlockSpec((tm, d), row),
                  pl.BlockSpec((None, d, d), lambda i: (l, 0, 0), pipeline_mode=pl.Buffered(1)),
                  vec, vec,
                  pl.BlockSpec((N_EXPERTS, d), lambda i: (0, 0)),
                  pl.BlockSpec((N_EXPERTS, 1), lambda i: (0, 0))],
        out_specs=[pl.BlockSpec((tm, d), row), pl.BlockSpec((tm, LANES), row),
                   pl.BlockSpec((SUBLANES, tm), lambda i: (0, i))],
        out_shape=[jax.ShapeDtypeStruct((t, d), F32), jax.ShapeDtypeStruct((t, LANES), F32),
                   jax.ShapeDtypeStruct((SUBLANES, t), jnp.int32)],
        compiler_params=_cparams(("arbitrary",), 56),
        name="wo_ln_router",
    )(mixed_p, mixed_s, x, w_o, ln_g, ln_b, wr_t, br)


def _dispatch_kernel(eidx_ref, pos_ref, meta_ref):
    t = eidx_ref.shape[1]
    e0, e1 = eidx_ref[0:1, :], eidx_ref[1:2, :]
    rows = lax.broadcasted_iota(jnp.int32, (N_EXPERTS, t), 0)
    hit0, hit1 = rows == e0, rows == e1
    oh = jnp.where(hit0 | hit1, 1.0, 0.0)
    cnt = jnp.sum(oh, axis=1, keepdims=True)
    padded = jnp.floor((cnt + (MOE_TILE - 1)) * (1.0 / MOE_TILE)) * MOE_TILE
    erow = lax.broadcasted_iota(jnp.int32, (N_EXPERTS, 1), 0)
    off = jnp.zeros((N_EXPERTS, 1), F32)
    run = jnp.zeros((1, 1), F32)
    for e in range(N_EXPERTS):
        off = jnp.where(erow == e, run, off)
        run = run + padded[e:e + 1, :]
    c = PREFIX_CHUNK
    before = jnp.where(lax.broadcasted_iota(jnp.int32, (c, c), 0) < lax.broadcasted_iota(jnp.int32, (c, c), 1),
                       1.0, 0.0).astype(BF16)
    pos_ref[...] = jnp.zeros(pos_ref.shape, jnp.int32)
    carry = off
    for j in range(t // c):
        cs = slice(j * c, (j + 1) * c)
        ohc = oh[:, cs]
        slot = _dot(ohc.astype(BF16), before) + carry
        pos_ref[0:1, cs] = jnp.sum(jnp.where(hit0[:, cs], slot, 0.0), axis=0, keepdims=True).astype(jnp.int32)
        pos_ref[1:2, cs] = jnp.sum(jnp.where(hit1[:, cs], slot, 0.0), axis=0, keepdims=True).astype(jnp.int32)
        carry = carry + jnp.sum(ohc, axis=1, keepdims=True)
    start = lax.broadcasted_iota(jnp.int32, (N_EXPERTS, LANES), 1).astype(F32) * MOE_TILE
    te = jnp.minimum(jnp.sum(jnp.where(off + padded <= start, 1.0, 0.0), axis=0, keepdims=True), N_EXPERTS - 1.0)
    mine = lax.broadcasted_iota(jnp.int32, (N_EXPERTS, LANES), 0).astype(F32) == te
    end_valid = jnp.sum(jnp.where(mine, off + cnt, 0.0), axis=0, keepdims=True)
    nvalid = jnp.clip(end_valid - start[0:1, :], 0.0, MOE_TILE)
    ordinal = jnp.zeros((N_EXPERTS, 1), F32)
    seen = jnp.zeros((1, 1), F32)
    for e in range(N_EXPERTS):
        ordinal = jnp.where(erow == e, seen, ordinal)
        seen = seen + jnp.where(cnt[e:e + 1, :] > 0.0, 1.0, 0.0)
    nxt = erow.astype(F32)
    later = jnp.full((1, 1), -1.0, F32)
    for e in reversed(range(N_EXPERTS)):
        nxt = jnp.where((erow == e) & (later >= 0.0), later, nxt)
        later = jnp.where(cnt[e:e + 1, :] > 0.0, float(e), later)
    parity = jnp.sum(jnp.where(mine, ordinal - 2.0 * jnp.floor(ordinal * 0.5), 0.0), axis=0, keepdims=True)
    nxt_tile = jnp.sum(jnp.where(mine, nxt, 0.0), axis=0, keepdims=True)
    r8 = lax.broadcasted_iota(jnp.int32, meta_ref.shape, 0)
    meta = jnp.zeros(meta_ref.shape, F32)
    for r, v in enumerate([te, nvalid, run * (1.0 / MOE_TILE), parity, nxt_tile]):
        meta = jnp.where(r8 == r, v, meta)
    meta_ref[...] = meta.astype(jnp.int32)


def _dispatch_call(eidx):
    t = eidx.shape[1]
    assert t % PREFIX_CHUNK == 0
    return pl.pallas_call(
        _dispatch_kernel,
        out_shape=[jax.ShapeDtypeStruct((SUBLANES, t), jnp.int32), jax.ShapeDtypeStruct((SUBLANES, LANES), jnp.int32)],
        compiler_params=pltpu.CompilerParams(vmem_limit_bytes=32 * MIB),
        name="moe_dispatch",
    )(eidx)


def _row_copy(src, i, dst, j, sem):
    return pltpu.make_async_copy(src.at[pl.ds(i, 1), :], dst.at[pl.ds(j, 1), :], sem)


def _scatter_kernel(p0_ref, p1_ref, x_ref, init_hbm, xs_hbm, sem, *, rows):
    del init_hbm
    base = pl.program_id(0) * rows

    def body(r, carry):
        t = base + r
        _row_copy(x_ref, r, xs_hbm, p0_ref[t], sem).start()
        _row_copy(x_ref, r, xs_hbm, p1_ref[t], sem).start()
        return carry

    lax.fori_loop(0, rows, body, 0, unroll=8)
    for _ in range(2):
        pltpu.make_async_copy(x_ref, xs_hbm.at[pl.ds(0, rows), :], sem).wait()


def _scatter_call(pos0, pos1, x1, init, rows):
    t, d = x1.shape
    return pl.pallas_call(
        functools.partial(_scatter_kernel, rows=rows),
        grid_spec=pltpu.PrefetchScalarGridSpec(
            num_scalar_prefetch=2, grid=(t // rows,),
            in_specs=[pl.BlockSpec((rows, d), lambda i, p0, p1: (i, 0)), pl.BlockSpec(memory_space=pl.ANY)],
            out_specs=pl.BlockSpec(memory_space=pl.ANY),
            scratch_shapes=[pltpu.SemaphoreType.DMA(())]),
        out_shape=jax.ShapeDtypeStruct(init.shape, init.dtype),
        input_output_aliases={3: 0},
        compiler_params=pltpu.CompilerParams(dimension_semantics=("arbitrary",), vmem_limit_bytes=32 * MIB,
                                             disable_bounds_checks=True),
        name="moe_scatter",
    )(pos0, pos1, x1, init)


def _experts_kernel(te_ref, nv_ref, nu_ref, slot_ref, nxt_ref, xs_ref, wg_hbm, wu_hbm, wd_hbm, y_ref,
                    wg_f, wu_f, wd_f, wg_bf, wu_bf, wd_bf, sems, *, l):
    i = pl.program_id(0)
    nv = nv_ref[i]
    e = te_ref[i]
    s = slot_ref[i]
    fresh = (i == 0) | (e != te_ref[jnp.maximum(i - 1, 0)])

    def weight_copies(expert, slot):
        return [pltpu.make_async_copy(w.at[l, expert], buf.at[slot], sems.at[slot])
                for w, buf in ((wg_hbm, wg_f), (wu_hbm, wu_f), (wd_hbm, wd_f))]

    @pl.when(fresh & (nv > 0))
    def _():
        @pl.when(i == 0)
        def _():
            for cp in weight_copies(e, s):
                cp.start()

        for cp in weight_copies(e, s):
            cp.wait()
        wg_bf[...] = wg_f[s].astype(BF16)
        wu_bf[...] = wu_f[s].astype(BF16)
        wd_bf[...] = wd_f[s].astype(BF16)

        @pl.when(nxt_ref[i] != e)
        def _():
            for cp in weight_copies(nxt_ref[i], 1 - s):
                cp.start()

    @pl.when(nv > 0)
    def _():
        row = lax.broadcasted_iota(jnp.int32, xs_ref.shape, 0)
        x = jnp.where(row < nv, xs_ref[...], 0.0).astype(BF16)
        hg = _dot(x, wg_bf[...])
        hu = _dot(x, wu_bf[...])
        h = hg * _sigmoid(hg) * hu
        y_ref[...] = _dot(h.astype(BF16), wd_bf[...])

    @pl.when(nv == 0)
    def _():
        y_ref[...] = jnp.zeros(y_ref.shape, F32)


def _experts_call(meta, xs, w_gate, w_up, w_down, l):
    n_slots, d = xs.shape
    f = w_gate.shape[-1]
    n_tiles = n_slots // MOE_TILE
    any_spec = pl.BlockSpec(memory_space=pl.ANY)
    return pl.pallas_call(
        functools.partial(_experts_kernel, l=l),
        grid_spec=pltpu.PrefetchScalarGridSpec(
            num_scalar_prefetch=5, grid=(n_tiles,),
            in_specs=[pl.BlockSpec((MOE_TILE, d), lambda i, te, nv, nu, sl, nx: (jnp.minimum(i, nu[0] - 1), 0)),
                      any_spec, any_spec, any_spec],
            out_specs=pl.BlockSpec((MOE_TILE, d), lambda i, te, nv, nu, sl, nx: (i, 0)),
            scratch_shapes=[pltpu.VMEM((2, d, f), F32), pltpu.VMEM((2, d, f), F32), pltpu.VMEM((2, f, d), F32),
                            pltpu.VMEM((d, f), BF16), pltpu.VMEM((d, f), BF16), pltpu.VMEM((f, d), BF16),
                            pltpu.SemaphoreType.DMA((2,))]),
        out_shape=jax.ShapeDtypeStruct((n_slots, d), F32),
        compiler_params=_cparams(("arbitrary",), 56),
        name="moe_experts",
    )(*meta, xs, w_gate, w_up, w_down)


def _combine_kernel(p0_ref, p1_ref, y_hbm, x1_ref, gate_ref, g_ref, b_ref, out_a, out_b, buf, sems,
                    *, tm, alpha, n_prompt_tiles):
    i = pl.program_id(0)
    slot = i % 2

    def issue(tile, s):
        base = tile * tm

        def body(r, carry):
            t = base + r
            _row_copy(y_hbm, p0_ref[t], buf.at[s, 0], r, sems.at[s]).start()
            _row_copy(y_hbm, p1_ref[t], buf.at[s, 1], r, sems.at[s]).start()
            return carry

        lax.fori_loop(0, tm, body, 0, unroll=8)

    @pl.when(i == 0)
    def _():
        issue(0, 0)

    @pl.when(i + 1 < pl.num_programs(0))
    def _():
        issue(i + 1, 1 - slot)

    for k in range(2):
        pltpu.make_async_copy(y_hbm.at[pl.ds(0, tm), :], buf.at[slot, k], sems.at[slot]).wait()
    gate = gate_ref[...]
    moe = gate[:, 0:1] * buf[slot, 0] + gate[:, 1:2] * buf[slot, 1]
    x2 = _layer_norm(alpha * x1_ref[...] + moe, g_ref[...], b_ref[...])
    if n_prompt_tiles is None:
        out_a[...] = x2
        out_b[...] = x2.astype(BF16)
    else:
        @pl.when(i < n_prompt_tiles)
        def _():
            out_a[...] = x2

        @pl.when(i >= n_prompt_tiles)
        def _():
            out_b[...] = x2


def _combine_call(pos0, pos1, y, x1, gate, ln_g, ln_b, l, tm, alpha, prompt_rows=None):
    t, d = x1.shape
    row = lambda i, p0, p1: (i, 0)
    vec = pl.BlockSpec((None, 1, d), lambda i, p0, p1: (l, 0, 0))
    if prompt_rows is None:
        n_p = None
        out_specs = [pl.BlockSpec((tm, d), row), pl.BlockSpec((tm, d), row)]
        out_shape = [jax.ShapeDtypeStruct((t, d), F32), jax.ShapeDtypeStruct((t, d), BF16)]
    else:
        n_p = prompt_rows // tm
        out_specs = [pl.BlockSpec((tm, d), lambda i, p0, p1: (jnp.minimum(i, n_p - 1), 0)),
                     pl.BlockSpec((tm, d), lambda i, p0, p1: (jnp.maximum(i - n_p, 0), 0))]
        out_shape = [jax.ShapeDtypeStruct((prompt_rows, d), F32), jax.ShapeDtypeStruct((t - prompt_rows, d), F32)]
    return pl.pallas_call(
        functools.partial(_combine_kernel, tm=tm, alpha=alpha, n_prompt_tiles=n_p),
        grid_spec=pltpu.PrefetchScalarGridSpec(
            num_scalar_prefetch=2, grid=(t // tm,),
            in_specs=[pl.BlockSpec(memory_space=pl.ANY),
                      pl.BlockSpec((tm, d), row), pl.BlockSpec((tm, LANES), row), vec, vec],
            out_specs=out_specs,
            scratch_shapes=[pltpu.VMEM((2, 2, tm, d), F32), pltpu.SemaphoreType.DMA((2,))]),
        out_shape=out_shape,
        compiler_params=pltpu.CompilerParams(dimension_semantics=("arbitrary",), vmem_limit_bytes=48 * MIB,
                                             disable_bounds_checks=True),
        name="moe_combine_ln",
    )(pos0, pos1, y, x1, gate, ln_g, ln_b)


def _rope_tables(pos):
    half = ROPE_DIM // 2
    inv = jnp.float32(ROPE_THETA) ** (-jnp.arange(half, dtype=F32) / half)
    ang = pos.astype(F32)[:, None] * inv[None, :]
    cos, sin = jnp.cos(ang), jnp.sin(ang)
    z = jnp.zeros_like(cos)
    return (jnp.concatenate([cos, cos, z, z], 1),
            jnp.concatenate([-sin, z, z, z], 1),
            jnp.concatenate([z, sin, z, z], 1))


def kernel(x_prompt, x_sample, cache_kv_latent, cache_k_rope, state_conv, w_in, q_norm_g, w_q_b, kv_norm_g, w_kv_b,
           conv_w, conv_b, conv_ln_g, conv_ln_b, w_conv_pw, w_o, ln1_g, ln1_b, w_gate, w_up, w_down, ln2_g, ln2_b,
           w_router, b_router):
    nb, seq, d = x_prompt.shape
    db, ds, _ = x_sample.shape
    depth = w_in.shape[0]
    past = cache_kv_latent.shape[2]
    d_conv = conv_w.shape[-1]
    rq, rkv = q_norm_g.shape[-1], kv_norm_g.shape[-1]
    tp, tsmp = nb * seq, db * ds
    t_all = tp + tsmp
    splits = (2 * d_conv, 2 * d_conv + rq, 2 * d_conv + rq + rkv, 2 * d_conv + rq + rkv + ROPE_DIM)
    alpha = (2.0 * depth) ** 0.25
    scale = math.log2(math.e) / math.sqrt(QK_NOPE_DIM + ROPE_DIM)
    tm = 512
    tq = 512
    n_tiles = 2 * t_all // MOE_TILE + N_EXPERTS
    assert t_all % MOE_TILE == 0 and n_tiles <= LANES
    assert d == N_HEADS * V_HEAD_DIM and w_kv_b.shape[-1] == N_HEADS * HEAD_PAD
    assert tp % tm == 0 and tsmp % tm == 0 and seq % tq == 0 and tq % CHUNK == 0
    assert past % CHUNK == 0 and ds <= CHUNK and ds >= CONV_WIDTH - 1

    pos = jnp.concatenate([jnp.tile(jnp.arange(seq, dtype=jnp.int32), nb),
                           jnp.tile(past + jnp.arange(ds, dtype=jnp.int32), db)])
    tabs = _rope_tables(pos)
    lane_chunk = jnp.arange(LANES, dtype=jnp.int32)[None, :] - ROPE_DIM
    row_chunk = jnp.where(jnp.arange(t_all) < tp, pos // CHUNK, LANES)[:, None]
    qmask = jnp.where((lane_chunk >= 0) & (lane_chunk > row_chunk) & (row_chunk < LANES), NEG_BIG, 0.0).astype(F32)
    kmask = jnp.where(lane_chunk == row_chunk, 1.0, 0.0).astype(F32)
    tabs_k, tabs_q = tabs + (kmask,), tabs + (qmask,)

    w_in_t = jnp.swapaxes(w_in, 1, 2)
    cache_pe_t = jnp.swapaxes(cache_k_rope, 2, 3)
    wq = w_q_b.reshape(depth, rq, N_HEADS, QK_NOPE_DIM + ROPE_DIM)
    wq = jnp.pad(wq, ((0, 0), (0, 0), (0, 0), (0, HEAD_PAD - QK_NOPE_DIM - ROPE_DIM)))
    wq = wq.reshape(depth, rq, N_HEADS * HEAD_PAD).astype(BF16)
    vec3 = lambda a: a.reshape(depth, 1, a.shape[-1])
    q_norm_g3, kv_norm_g3 = vec3(q_norm_g), vec3(kv_norm_g)
    conv_b3, conv_ln_g3, conv_ln_b3 = vec3(conv_b), vec3(conv_ln_g), vec3(conv_ln_b)
    ln1_g3, ln1_b3, ln2_g3, ln2_b3 = vec3(ln1_g), vec3(ln1_b), vec3(ln2_g), vec3(ln2_b)
    wr_t = w_router.T
    w_o_bf = w_o.astype(BF16)
    br = b_router.reshape(N_EXPERTS, 1)

    x = jnp.concatenate([x_prompt.reshape(tp, d), x_sample.reshape(tsmp, d)], axis=0)
    x_bf = x.astype(BF16)

    spare = jnp.zeros((n_tiles * MOE_TILE, d), F32)
    keep = CONV_WIDTH - 1
    outs = [[] for _ in range(6)]
    for l in range(depth):
        u = _glu_call(x_bf, w_in_t, l, d_conv, tm, 512)
        qn, ckv_p, ckv_s, kpe_p, kpe_s, kpad = _latent_call(x_bf, w_in_t, q_norm_g3, kv_norm_g3, tabs_k, l, splits,
                                                             rq, rkv, tm, tp)
        gates = _gates_call(x_bf, w_in_t, l, splits[3], tm, 1024)
        ga_p = _conv_prompt_call(u, gates, conv_w, conv_b3, conv_ln_g3, conv_ln_b3, w_conv_pw, l, nb, seq, 512, 128)
        ga_s = _conv_sample_call(u, gates, state_conv, conv_w, conv_b3, conv_ln_g3, conv_ln_b3, w_conv_pw,
                                 l, tp, db, ds, ds)
        q = _q_call(qn, wq, tabs_q, l, tm, scale)
        k, vt = _kv_call(ckv_p, kpad, w_kv_b, l, tp, tm)
        mixed_p = _attn_prompt_call(q, k, vt, ga_p, gates, nb, seq, tq, tq)
        mixed_s = _attn_sample_call(q, cache_kv_latent, cache_pe_t, ckv_s, kpe_s, w_kv_b, ga_s, gates, l, tp, db, ds)
        x1, gate, eidx = _wo_call(mixed_p, mixed_s, x, w_o_bf, ln1_g3, ln1_b3, wr_t, br, l, tm, alpha)
        pos_rows, meta = _dispatch_call(eidx)
        pos0, pos1 = pos_rows[0], pos_rows[1]
        tile_meta = (meta[0, :n_tiles], meta[1, :n_tiles], meta[2, :1], meta[3, :n_tiles], meta[4, :n_tiles])
        xs = _scatter_call(pos0, pos1, x1, spare, tm)
        y = _experts_call(tile_meta, xs, w_gate, w_up, w_down, l)
        if l + 1 < depth:
            x, x_bf = _combine_call(pos0, pos1, y, x1, gate, ln2_g3, ln2_b3, l, MOE_TILE, alpha)
        else:
            y_prompt, y_sample = _combine_call(pos0, pos1, y, x1, gate, ln2_g3, ln2_b3, l, MOE_TILE, alpha, tp)
        spare = y
        outs[0].append(ckv_p.reshape(nb, seq, rkv))
        outs[1].append(kpe_p.reshape(nb, seq, ROPE_DIM))
        outs[2].append(u[:tp].reshape(nb, seq, d_conv)[:, seq - keep:])
        outs[3].append(ckv_s.reshape(db, ds, rkv))
        outs[4].append(kpe_s.reshape(db, ds, ROPE_DIM))
        outs[5].append(u[tp:].reshape(db, ds, d_conv)[:, ds - keep:])

    return (y_prompt.reshape(nb, seq, d), y_sample.reshape(db, ds, d)) + tuple(jnp.stack(o) for o in outs)
```

```python
import functools
import math

import jax
import jax.numpy as jnp
from jax import lax
from jax.experimental import pallas as pl
from jax.experimental.pallas import tpu as pltpu

F32 = jnp.float32
BF16 = jnp.bfloat16

CHUNK = 64
CONV_WIDTH = 31
N_HEADS = 16
QK_NOPE_DIM = 128
ROPE_DIM = 64
V_HEAD_DIM = 128
ROPE_THETA = 10000.0
N_EXPERTS = 16
N_EXPERT_GROUPS = 4
EXPERTS_PER_GROUP = N_EXPERTS // N_EXPERT_GROUPS
LN_EPS = 1e-5
RMS_EPS = 1e-6

LANES = 128
SUBLANES = 8
MOE_TILE = 256
PREFIX_CHUNK = 512
HEAD_PAD = 256
CONV_HALO = 32
CONV_OFF = CONV_HALO - (CONV_WIDTH - 1)
NEG_BIG = -1e30
MIB = 1024 * 1024


def _cparams(sem, vmem_mib):
    return pltpu.CompilerParams(dimension_semantics=sem, vmem_limit_bytes=vmem_mib * MIB)


def _dot(a, b):
    return jnp.dot(a, b, preferred_element_type=F32)


def _dot_nt(a, b):
    return lax.dot_general(a, b, (((1,), (1,)), ((), ())), preferred_element_type=F32)


def _layer_norm(y, g, b):
    mu = jnp.mean(y, axis=-1, keepdims=True)
    d = y - mu
    var = jnp.mean(d * d, axis=-1, keepdims=True)
    return d * lax.rsqrt(var + LN_EPS) * g + b


def _rms_norm(y, g):
    return y * lax.rsqrt(jnp.mean(y * y, axis=-1, keepdims=True) + RMS_EPS) * g


def _sigmoid(x):
    return 0.5 * jnp.tanh(0.5 * x) + 0.5


def _rope128(v, c, s1, s2):
    return v * c + pltpu.roll(v, 96, axis=1) * s1 + pltpu.roll(v, 32, axis=1) * s2


def _glu_kernel(x_ref, wa_ref, wg_ref, u_ref, wa_bf, wg_bf):
    @pl.when(pl.program_id(1) == 0)
    def _():
        wa_bf[...] = wa_ref[...].astype(BF16)
        wg_bf[...] = wg_ref[...].astype(BF16)

    x = x_ref[...]
    a = _dot_nt(x, wa_bf[...])
    g = _dot_nt(x, wg_bf[...])
    u_ref[...] = a * _sigmoid(g)


def _glu_call(x_bf, w_in_t, l, d_conv, tm, tn):
    t, d = x_bf.shape
    nj = d_conv // tn
    return pl.pallas_call(
        _glu_kernel,
        grid=(nj, t // tm),
        in_specs=[pl.BlockSpec((tm, d), lambda j, i: (i, 0)),
                  pl.BlockSpec((None, tn, d), lambda j, i: (l, j, 0)),
                  pl.BlockSpec((None, tn, d), lambda j, i: (l, j + nj, 0))],
        out_specs=pl.BlockSpec((tm, tn), lambda j, i: (i, j)),
        out_shape=jax.ShapeDtypeStruct((t, d_conv), F32),
        scratch_shapes=[pltpu.VMEM((tn, d), BF16), pltpu.VMEM((tn, d), BF16)],
        compiler_params=_cparams(("arbitrary", "arbitrary"), 48),
        name="glu",
    )(x_bf, w_in_t, w_in_t)


def _latent_kernel(x_ref, wq_ref, wc_ref, wk_ref, qg_ref, cg_ref, c_ref, s1_ref, s2_ref, kmask_ref,
                   qn_ref, ckvp_ref, ckvs_ref, kpep_ref, kpes_ref, kpad_ref, wq_bf, wc_bf, wk_bf, *, n_prompt_tiles):
    i = pl.program_id(0)

    @pl.when(i == 0)
    def _():
        wq_bf[...] = wq_ref[...].astype(BF16)
        wc_bf[...] = wc_ref[...].astype(BF16)
        wk_bf[...] = wk_ref[...].astype(BF16)

    x = x_ref[...]
    qn_ref[...] = _rms_norm(_dot_nt(x, wq_bf[...]), qg_ref[...]).astype(BF16)
    ckv = _rms_norm(_dot_nt(x, wc_bf[...]), cg_ref[...])
    kr = _dot_nt(x, wk_bf[...])
    k = _rope128(kr, c_ref[...], s1_ref[...], s2_ref[...])
    lane = lax.broadcasted_iota(jnp.int32, k.shape, 1)
    k = jnp.where(lane < ROPE_DIM, k, 0.0)
    kpad_ref[...] = (k + kmask_ref[...]).astype(BF16)

    @pl.when(i < n_prompt_tiles)
    def _():
        ckvp_ref[...] = ckv
        kpep_ref[...] = k[:, :ROPE_DIM]

    @pl.when(i >= n_prompt_tiles)
    def _():
        ckvs_ref[...] = ckv
        kpes_ref[...] = k[:, :ROPE_DIM]


def _latent_call(x_bf, w_in_t, q_norm_g, kv_norm_g, tabs, l, splits, rq, rkv, tm, prompt_rows):
    t, d = x_bf.shape
    assert splits[0] % rq == 0 and splits[1] % rkv == 0 and splits[2] % LANES == 0
    n_p = prompt_rows // tm
    row = lambda i: (i, 0)
    prow = lambda i: (jnp.minimum(i, n_p - 1), 0)
    srow = lambda i: (jnp.maximum(i - n_p, 0), 0)
    return pl.pallas_call(
        functools.partial(_latent_kernel, n_prompt_tiles=n_p),
        grid=(t // tm,),
        in_specs=[pl.BlockSpec((tm, d), row),
                  pl.BlockSpec((None, rq, d), lambda i: (l, splits[0] // rq, 0)),
                  pl.BlockSpec((None, rkv, d), lambda i: (l, splits[1] // rkv, 0)),
                  pl.BlockSpec((None, LANES, d), lambda i: (l, splits[2] // LANES, 0)),
                  pl.BlockSpec((None, 1, rq), lambda i: (l, 0, 0)),
                  pl.BlockSpec((None, 1, rkv), lambda i: (l, 0, 0)),
                  pl.BlockSpec((tm, LANES), row),
                  pl.BlockSpec((tm, LANES), row),
                  pl.BlockSpec((tm, LANES), row),
                  pl.BlockSpec((tm, LANES), row)],
        out_specs=[pl.BlockSpec((tm, rq), row),
                   pl.BlockSpec((tm, rkv), prow), pl.BlockSpec((tm, rkv), srow),
                   pl.BlockSpec((tm, ROPE_DIM), prow), pl.BlockSpec((tm, ROPE_DIM), srow),
                   pl.BlockSpec((tm, LANES), row)],
        out_shape=[jax.ShapeDtypeStruct((t, rq), BF16),
                   jax.ShapeDtypeStruct((prompt_rows, rkv), F32), jax.ShapeDtypeStruct((t - prompt_rows, rkv), F32),
                   jax.ShapeDtypeStruct((prompt_rows, ROPE_DIM), F32),
                   jax.ShapeDtypeStruct((t - prompt_rows, ROPE_DIM), F32),
                   jax.ShapeDtypeStruct((t, LANES), BF16)],
        scratch_shapes=[pltpu.VMEM((rq, d), BF16), pltpu.VMEM((rkv, d), BF16), pltpu.VMEM((LANES, d), BF16)],
        compiler_params=_cparams(("arbitrary",), 48),
        name="latent",
    )(x_bf, w_in_t, w_in_t, w_in_t, q_norm_g, kv_norm_g, *tabs)


def _gates_kernel(x_ref, w_hbm, o_ref, w_f32, w_bf, sem, *, l, row0, tn):
    j = pl.program_id(0)

    def block_copy(jj):
        return pltpu.make_async_copy(w_hbm.at[l, pl.ds(pl.multiple_of(row0 + jj * tn, SUBLANES), tn), :], w_f32, sem)

    @pl.when(pl.program_id(1) == 0)
    def _():
        @pl.when(j == 0)
        def _():
            block_copy(0).start()

        block_copy(j).wait()
        w_bf[...] = w_f32[...].astype(BF16)

        @pl.when(j + 1 < pl.num_programs(0))
        def _():
            block_copy(j + 1).start()

    o_ref[...] = _sigmoid(_dot_nt(x_ref[...], w_bf[...])).astype(BF16)


def _gates_call(x_bf, w_in_t, l, row0, tm, tn):
    t, d = x_bf.shape
    n = w_in_t.shape[1] - row0
    assert n % tn == 0 and row0 % SUBLANES == 0
    return pl.pallas_call(
        functools.partial(_gates_kernel, l=l, row0=row0, tn=tn),
        grid=(n // tn, t // tm),
        in_specs=[pl.BlockSpec((tm, d), lambda j, i: (i, 0)),
                  pl.BlockSpec(memory_space=pl.ANY)],
        out_specs=pl.BlockSpec((tm, tn), lambda j, i: (i, j)),
        out_shape=jax.ShapeDtypeStruct((t, n), BF16),
        scratch_shapes=[pltpu.VMEM((tn, d), F32), pltpu.VMEM((tn, d), BF16), pltpu.SemaphoreType.DMA(())],
        compiler_params=_cparams(("arbitrary", "arbitrary"), 48),
        name="gates",
    )(x_bf, w_in_t)


def _conv_tail(ext_ref, y_ref, cw_ref, cb_ref, lg_ref, lb_ref, wpw_bf, gate_ref, ga_ref, ts, rows):
    c_dim = y_ref.shape[1]
    for c in range(c_dim // LANES):
        cs = slice(c * LANES, (c + 1) * LANES)

        def rbody(r, carry, cs=cs):
            r0 = pl.multiple_of(r * rows, rows)
            win = ext_ref[pl.ds(r0, rows + CONV_HALO), cs]
            acc = jnp.zeros((rows, LANES), F32)
            for b in range(SUBLANES):
                taps = [o for o in range(CONV_OFF, CONV_OFF + CONV_WIDTH) if o % SUBLANES == b]
                n_win = rows + CONV_HALO
                shifted = win if b == 0 else pltpu.roll(win, n_win - b, axis=0)
                for o in taps:
                    acc = acc + shifted[o - b:o - b + rows, :] * cw_ref[o - CONV_OFF:o - CONV_OFF + 1, cs]
            y_ref[pl.ds(r0, rows), cs] = acc
            return carry

        lax.fori_loop(0, ts // rows, rbody, 0)

    y = _layer_norm(y_ref[...] + cb_ref[...], lg_ref[...], lb_ref[...])
    z = (y * _sigmoid(y)).astype(BF16)
    a_out = _dot(z, wpw_bf[...])
    ga_ref[...] = (gate_ref[...].astype(F32) * a_out).astype(BF16)


def _conv_prompt_kernel(u_ref, halo_ref, cw_ref, cb_ref, lg_ref, lb_ref, wpw_ref, gate_ref, ga_ref,
                        ext_ref, y_ref, wpw_bf, *, ts, rows):
    b, i = pl.program_id(0), pl.program_id(1)

    @pl.when((b == 0) & (i == 0))
    def _():
        wpw_bf[...] = wpw_ref[...].astype(BF16)

    @pl.when(i == 0)
    def _():
        ext_ref[0:CONV_HALO, :] = jnp.zeros((CONV_HALO, ext_ref.shape[1]), F32)

    @pl.when(i > 0)
    def _():
        ext_ref[0:CONV_HALO, :] = halo_ref[...]

    ext_ref[CONV_HALO:CONV_HALO + ts, :] = u_ref[...]
    _conv_tail(ext_ref, y_ref, cw_ref, cb_ref, lg_ref, lb_ref, wpw_bf, gate_ref, ga_ref, ts, rows)


def _conv_sample_kernel(u_ref, hist_ref, cw_ref, cb_ref, lg_ref, lb_ref, wpw_ref, gate_ref, ga_ref,
                        ext_ref, y_ref, wpw_bf, *, ts, rows):
    @pl.when(pl.program_id(0) == 0)
    def _():
        wpw_bf[...] = wpw_ref[...].astype(BF16)

    ext_ref[0:CONV_HALO, :] = jnp.zeros((CONV_HALO, ext_ref.shape[1]), F32)
    ext_ref[CONV_OFF:CONV_HALO, :] = hist_ref[...]
    ext_ref[CONV_HALO:CONV_HALO + ts, :] = u_ref[...]
    _conv_tail(ext_ref, y_ref, cw_ref, cb_ref, lg_ref, lb_ref, wpw_bf, gate_ref, ga_ref, ts, rows)


def _conv_weight_specs(l, c_dim, d, nargs):
    z = (lambda *a: (l, 0, 0))
    return [pl.BlockSpec((None, CONV_WIDTH, c_dim), z),
            pl.BlockSpec((None, 1, c_dim), z),
            pl.BlockSpec((None, 1, c_dim), z),
            pl.BlockSpec((None, 1, c_dim), z),
            pl.BlockSpec((None, c_dim, d), z)]


def _conv_prompt_call(u, gates, conv_w, conv_b, ln_g, ln_b, w_pw, l, nb, seq, ts, rows):
    c_dim = u.shape[1]
    d = w_pw.shape[-1]
    ns = seq // ts
    hb = ts // CONV_HALO
    tile = lambda b, i: (b * ns + i, 0)
    return pl.pallas_call(
        functools.partial(_conv_prompt_kernel, ts=ts, rows=rows),
        grid=(nb, ns),
        in_specs=[pl.BlockSpec((ts, c_dim), tile),
                  pl.BlockSpec((CONV_HALO, c_dim), lambda b, i: (jnp.maximum((b * ns + i) * hb - 1, 0), 0))]
        + _conv_weight_specs(l, c_dim, d, 2)
        + [pl.BlockSpec((ts, d), tile)],
        out_specs=pl.BlockSpec((ts, d), tile),
        out_shape=jax.ShapeDtypeStruct((nb * seq, d), BF16),
        scratch_shapes=[pltpu.VMEM((CONV_HALO + ts, c_dim), F32), pltpu.VMEM((ts, c_dim), F32),
                        pltpu.VMEM((c_dim, d), BF16)],
        compiler_params=_cparams(("arbitrary", "arbitrary"), 56),
        name="conv_prompt",
    )(u, u, conv_w, conv_b, ln_g, ln_b, w_pw, gates)


def _conv_sample_call(u, gates, state_conv, conv_w, conv_b, ln_g, ln_b, w_pw, l, row0, nb, ts, rows):
    c_dim = u.shape[1]
    d = w_pw.shape[-1]
    t0 = row0 // ts
    tile = lambda b: (t0 + b, 0)
    return pl.pallas_call(
        functools.partial(_conv_sample_kernel, ts=ts, rows=rows),
        grid=(nb,),
        in_specs=[pl.BlockSpec((ts, c_dim), tile),
                  pl.BlockSpec((None, None, CONV_WIDTH - 1, c_dim), lambda b: (l, b, 0, 0))]
        + _conv_weight_specs(l, c_dim, d, 1)
        + [pl.BlockSpec((ts, d), tile)],
        out_specs=pl.BlockSpec((ts, d), lambda b: (b, 0)),
        out_shape=jax.ShapeDtypeStruct((nb * ts, d), BF16),
        scratch_shapes=[pltpu.VMEM((CONV_HALO + ts, c_dim), F32), pltpu.VMEM((ts, c_dim), F32),
                        pltpu.VMEM((c_dim, d), BF16)],
        compiler_params=_cparams(("arbitrary",), 56),
        name="conv_sample",
    )(u, state_conv, conv_w, conv_b, ln_g, ln_b, w_pw, gates)


def _q_kernel(qn_ref, w_ref, c_ref, s1_ref, s2_ref, qmask_ref, q_ref, *, scale):
    qn = qn_ref[...]
    c, s1, s2, qmask = c_ref[...], s1_ref[...], s2_ref[...], qmask_ref[...]
    for h in range(N_HEADS):
        qh = _dot(qn, w_ref[:, h * HEAD_PAD:(h + 1) * HEAD_PAD])
        q_ref[:, h * HEAD_PAD:h * HEAD_PAD + LANES] = (qh[:, :LANES] * scale).astype(BF16)
        q_ref[:, h * HEAD_PAD + LANES:(h + 1) * HEAD_PAD] = (
            _rope128(qh[:, LANES:], c, s1, s2) * scale + qmask).astype(BF16)


def _q_call(qn, w_q, tabs, l, tm, scale):
    t, r = qn.shape
    n = w_q.shape[-1]
    row = lambda i: (i, 0)
    return pl.pallas_call(
        functools.partial(_q_kernel, scale=scale),
        grid=(t // tm,),
        in_specs=[pl.BlockSpec((tm, r), row),
                  pl.BlockSpec((None, r, n), lambda i: (l, 0, 0)),
                  pl.BlockSpec((tm, LANES), row), pl.BlockSpec((tm, LANES), row), pl.BlockSpec((tm, LANES), row),
                  pl.BlockSpec((tm, LANES), row)],
        out_specs=pl.BlockSpec((tm, n), row),
        out_shape=jax.ShapeDtypeStruct((t, n), BF16),
        compiler_params=_cparams(("arbitrary",), 48),
        name="q_proj",
    )(qn, w_q, *tabs)


def _kv_kernel(lat_ref, kpad_ref, w_ref, k_ref, vt_ref, w_bf):
    @pl.when(pl.program_id(0) == 0)
    def _():
        w_bf[...] = w_ref[...].astype(BF16)

    lat = lat_ref[...].astype(BF16)
    kpad = kpad_ref[...]
    for h in range(N_HEADS):
        kv = _dot(lat, w_bf[:, h * HEAD_PAD:(h + 1) * HEAD_PAD])
        k_ref[:, h * HEAD_PAD:h * HEAD_PAD + LANES] = kv[:, :LANES].astype(BF16)
        k_ref[:, h * HEAD_PAD + LANES:(h + 1) * HEAD_PAD] = kpad
        vt_ref[h * V_HEAD_DIM:(h + 1) * V_HEAD_DIM, :] = kv[:, LANES:].T.astype(BF16)


def _kv_call(ckv, kpad, w_kv_b, l, rows, tm):
    r = ckv.shape[1]
    n = w_kv_b.shape[-1]
    row = lambda i: (i, 0)
    return pl.pallas_call(
        _kv_kernel,
        grid=(rows // tm,),
        in_specs=[pl.BlockSpec((tm, r), row),
                  pl.BlockSpec((tm, LANES), row),
                  pl.BlockSpec((None, r, n), lambda i: (l, 0, 0))],
        out_specs=[pl.BlockSpec((tm, N_HEADS * HEAD_PAD), row),
                   pl.BlockSpec((N_HEADS * V_HEAD_DIM, tm), lambda i: (0, i))],
        out_shape=[jax.ShapeDtypeStruct((rows, N_HEADS * HEAD_PAD), BF16),
                   jax.ShapeDtypeStruct((N_HEADS * V_HEAD_DIM, rows), BF16)],
        scratch_shapes=[pltpu.VMEM((r, n), BF16)],
        compiler_params=_cparams(("arbitrary",), 48),
        name="kv_proj",
    )(ckv, kpad, w_kv_b)


def _attn_prompt_kernel(qi_ref, kj_ref, q_ref, k_ref, vt_ref, ga_ref, gb_ref, o_ref, *scratch,
                        n_steps, tq, tk, unroll, n_buf, heads):
    per_head = 2 * n_buf + 3

    def head_refs(h):
        sc = scratch[h * per_head:(h + 1) * per_head]
        return [(sc[2 * j], sc[2 * j + 1]) for j in range(n_buf)], sc[2 * n_buf], sc[2 * n_buf + 1], sc[2 * n_buf + 2]

    def stage_scores(h, t, s_ref, mx_ref):
        q = q_ref[pl.ds(pl.multiple_of(qi_ref[t] * tq, tq), tq), h * HEAD_PAD:(h + 1) * HEAD_PAD]
        k = k_ref[pl.ds(pl.multiple_of(kj_ref[t] * tk, tk), tk), h * HEAD_PAD:(h + 1) * HEAD_PAD]
        s = _dot_nt(k, q)
        s_ref[...] = s
        mx_ref[...] = jnp.max(s, axis=0, keepdims=True)

    def step(h, t, j):
        bufs, acc_ref, m_ref, l_ref = head_refs(h)
        s_cur, mx_cur = bufs[j]
        stage_scores(h, t + n_buf - 1, *bufs[(j - 1) % n_buf])
        cols = pl.ds(pl.multiple_of(qi_ref[t] * tq, tq), tq)
        m_old = m_ref[:, cols]
        m_new = jnp.maximum(m_old, mx_cur[...])
        alpha = jnp.exp2(m_old - m_new)
        p = jnp.exp2(s_cur[...] - m_new)
        l_ref[:, cols] = alpha * l_ref[:, cols] + jnp.sum(p, axis=0, keepdims=True)
        keys = pl.ds(pl.multiple_of(kj_ref[t] * tk, tk), tk)
        pv = _dot(vt_ref[h * V_HEAD_DIM:(h + 1) * V_HEAD_DIM, keys], p.astype(BF16))
        acc_ref[:, cols] = alpha * acc_ref[:, cols] + pv
        m_ref[:, cols] = m_new

    for h in range(heads):
        bufs, acc_ref, m_ref, l_ref = head_refs(h)
        m_ref[...] = jnp.full(m_ref.shape, NEG_BIG, F32)
        l_ref[...] = jnp.zeros(l_ref.shape, F32)
        acc_ref[...] = jnp.zeros(acc_ref.shape, F32)
        for j in range(n_buf - 1):
            stage_scores(h, j, *bufs[j])

    def body(u, carry):
        for j in range(unroll):
            for h in range(heads):
                step(h, unroll * u + j, j % n_buf)
        return carry

    lax.fori_loop(0, n_steps // unroll, body, 0)

    def finish(qi, carry):
        rows = pl.ds(pl.multiple_of(qi * tq, tq), tq)
        for h in range(heads):
            _, acc_ref, _, l_ref = head_refs(h)
            cols = slice(h * V_HEAD_DIM, (h + 1) * V_HEAD_DIM)
            o = (acc_ref[:, rows] / l_ref[:, rows]).T
            o_ref[rows, cols] = (ga_ref[rows, cols].astype(F32) + gb_ref[rows, cols].astype(F32) * o).astype(BF16)
        return carry

    lax.fori_loop(0, o_ref.shape[0] // tq, finish, 0)


def _attn_prompt_call(q, k, vt, ga, gates, nb, seq, tq, tk):
    d = N_HEADS * V_HEAD_DIM
    nq = seq // tq
    n_buf, unroll, heads = 2, 4, 1
    assert nq >= 2 and seq // CHUNK <= LANES - ROPE_DIM and seq % tk == 0
    tiles = [(qi, kj) for qi in range(nq) for kj in range(-(-(qi + 1) * tq // tk))]
    masked_tile = (0, -(-tq // tk))
    assert (masked_tile[1] + 1) * tk <= seq
    tiles += [masked_tile] * (-len(tiles) % unroll)
    n_steps = len(tiles)
    tiles += [(0, 0)] * (n_buf - 1)
    qi_tab = jnp.asarray([t[0] for t in tiles], jnp.int32)
    kj_tab = jnp.asarray([t[1] for t in tiles], jnp.int32)
    bh = lambda b, h, qt, kt: (b, h)
    return pl.pallas_call(
        functools.partial(_attn_prompt_kernel, n_steps=n_steps, tq=tq, tk=tk, unroll=unroll, n_buf=n_buf,
                          heads=heads),
        grid_spec=pltpu.PrefetchScalarGridSpec(
            num_scalar_prefetch=2, grid=(nb, N_HEADS // heads),
            in_specs=[pl.BlockSpec((seq, heads * HEAD_PAD), bh),
                      pl.BlockSpec((seq, heads * HEAD_PAD), bh),
                      pl.BlockSpec((heads * V_HEAD_DIM, seq), lambda b, h, qt, kt: (h, b)),
                      pl.BlockSpec((seq, heads * V_HEAD_DIM), bh),
                      pl.BlockSpec((seq, heads * V_HEAD_DIM), lambda b, h, qt, kt: (b, N_HEADS // heads + h))],
            out_specs=pl.BlockSpec((seq, heads * V_HEAD_DIM), bh),
            scratch_shapes=([pltpu.VMEM((tk, tq), F32), pltpu.VMEM((1, tq), F32)] * n_buf
                            + [pltpu.VMEM((V_HEAD_DIM, seq), F32), pltpu.VMEM((1, seq), F32),
                               pltpu.VMEM((1, seq), F32)]) * heads),
        out_shape=jax.ShapeDtypeStruct((nb * seq, d), BF16),
        compiler_params=_cparams(("arbitrary", "arbitrary"), 56),
        name="attn_prompt",
    )(qi_tab, kj_tab, q, k, vt, ga, gates)


def _attn_sample_kernel(q_ref, latp_ref, kpep_ref, latn_ref, kpen_ref, w_ref, ga_ref, gb_ref, o_ref,
                        w_bf, qlat_ref, qpe_ref, kpp_ref, kpn_ref, olat_ref, *, ds):
    @pl.when(pl.program_id(0) == 0)
    def _():
        w_bf[...] = w_ref[...].astype(BF16)
        kpp_ref[...] = jnp.zeros(kpp_ref.shape, BF16)
        kpn_ref[...] = jnp.zeros(kpn_ref.shape, BF16)

    kpp_ref[:ROPE_DIM, :] = kpep_ref[...].astype(BF16)
    kpn_ref[:, :ROPE_DIM] = kpen_ref[...].astype(BF16)
    for h in range(N_HEADS):
        rows = slice(h * ds, (h + 1) * ds)
        qn = q_ref[:, h * HEAD_PAD:h * HEAD_PAD + LANES]
        qlat_ref[rows, :] = _dot_nt(qn, w_bf[:, h * HEAD_PAD:h * HEAD_PAD + LANES]).astype(BF16)
        qpe_ref[rows, :] = q_ref[:, h * HEAD_PAD + LANES:(h + 1) * HEAD_PAD]

    lat_p = latp_ref[...].astype(BF16)
    lat_n = latn_ref[...].astype(BF16)
    qlat, qpe = qlat_ref[...], qpe_ref[...]
    s_p = _dot_nt(qlat, lat_p) + _dot(qpe, kpp_ref[...])
    s_n = _dot_nt(qlat, lat_n) + _dot_nt(qpe, kpn_ref[...])
    m = jnp.maximum(jnp.max(s_p, axis=1, keepdims=True), jnp.max(s_n, axis=1, keepdims=True))
    p_p = jnp.exp2(s_p - m)
    p_n = jnp.exp2(s_n - m)
    den = jnp.sum(p_p, axis=1, keepdims=True) + jnp.sum(p_n, axis=1, keepdims=True)
    o_lat = _dot(p_p.astype(BF16), lat_p) + _dot(p_n.astype(BF16), lat_n)
    olat_ref[...] = (o_lat / den).astype(BF16)
    for h in range(N_HEADS):
        cols = slice(h * V_HEAD_DIM, (h + 1) * V_HEAD_DIM)
        o = _dot(olat_ref[h * ds:(h + 1) * ds, :], w_bf[:, h * HEAD_PAD + LANES:(h + 1) * HEAD_PAD])
        o_ref[:, cols] = (ga_ref[:, cols].astype(F32) + gb_ref[:, cols].astype(F32) * o).astype(BF16)


def _attn_sample_call(q, cache_lat, cache_pe, ckv_s, kpe_s, w_kv_b, ga_s, gates, l, row0, nb, ds):
    past, r = cache_lat.shape[2], cache_lat.shape[3]
    d = N_HEADS * V_HEAD_DIM
    t0 = row0 // ds
    tile = lambda b: (t0 + b, 0)
    return pl.pallas_call(
        functools.partial(_attn_sample_kernel, ds=ds),
        grid=(nb,),
        in_specs=[pl.BlockSpec((ds, N_HEADS * HEAD_PAD), tile),
                  pl.BlockSpec((None, None, past, r), lambda b: (l, b, 0, 0)),
                  pl.BlockSpec((None, None, ROPE_DIM, past), lambda b: (l, b, 0, 0)),
                  pl.BlockSpec((ds, r), lambda b: (b, 0)),
                  pl.BlockSpec((ds, ROPE_DIM), lambda b: (b, 0)),
                  pl.BlockSpec((None, r, N_HEADS * HEAD_PAD), lambda b: (l, 0, 0)),
                  pl.BlockSpec((ds, d), lambda b: (b, 0)),
                  pl.BlockSpec((ds, d), lambda b: (t0 + b, 1))],
        out_specs=pl.BlockSpec((ds, d), lambda b: (b, 0)),
        out_shape=jax.ShapeDtypeStruct((nb * ds, d), BF16),
        scratch_shapes=[pltpu.VMEM((r, N_HEADS * HEAD_PAD), BF16),
                        pltpu.VMEM((N_HEADS * ds, r), BF16),
                        pltpu.VMEM((N_HEADS * ds, LANES), BF16),
                        pltpu.VMEM((LANES, past), BF16),
                        pltpu.VMEM((ds, LANES), BF16),
                        pltpu.VMEM((N_HEADS * ds, r), BF16)],
        compiler_params=_cparams(("arbitrary",), 56),
        name="attn_sample",
    )(q, cache_lat, cache_pe, ckv_s, kpe_s, w_kv_b, ga_s, gates)


def _route(scores, bias):
    sel = [s + b for s, b in zip(scores, bias)]
    n = EXPERTS_PER_GROUP
    grp = []
    for g in range(N_EXPERT_GROUPS):
        v = sel[g * n:(g + 1) * n]
        best = None
        for i in range(n):
            for j in range(i + 1, n):
                pair = v[i] + v[j]
                best = pair if best is None else jnp.maximum(best, pair)
        grp.append(best)
    g_idx = jnp.zeros_like(grp[0], dtype=jnp.int32)
    g_best = grp[0]
    for g in range(1, N_EXPERT_GROUPS):
        better = grp[g] > g_best
        g_idx = jnp.where(better, g, g_idx)
        g_best = jnp.where(better, grp[g], g_best)

    def pick(rows_by_group):
        out = rows_by_group[0]
        for g in range(1, N_EXPERT_GROUPS):
            out = jnp.where(g_idx == g, rows_by_group[g], out)
        return out

    in_sel = [pick([sel[g * n + j] for g in range(N_EXPERT_GROUPS)]) for j in range(n)]
    in_sc = [pick([scores[g * n + j] for g in range(N_EXPERT_GROUPS)]) for j in range(n)]

    def argmax_first(vals, excluded):
        idx = None
        best = None
        for j in range(n):
            v = vals[j] if excluded is None else jnp.where(excluded == j, -jnp.inf, vals[j])
            if best is None:
                best, idx = v, jnp.zeros_like(g_idx)
            else:
                better = v > best
                idx = jnp.where(better, j, idx)
                best = jnp.where(better, v, best)
        return idx

    l0 = argmax_first(in_sel, None)
    l1 = argmax_first(in_sel, l0)

    def take(vals, idx):
        out = vals[0]
        for j in range(1, n):
            out = jnp.where(idx == j, vals[j], out)
        return out

    w0, w1 = take(in_sc, l0), take(in_sc, l1)
    tot = w0 + w1
    return g_idx * n + l0, g_idx * n + l1, w0 / tot, w1 / tot


def _wo_kernel(mixp_ref, mixs_ref, x_ref, w_bf, g_ref, b_ref, wr_ref, br_ref, x1_ref, gate_ref, eidx_ref,
               *, alpha, n_prompt_tiles):
    mix = jnp.where(pl.program_id(0) < n_prompt_tiles, mixp_ref[...], mixs_ref[...])
    y = alpha * x_ref[...] + _dot(mix, w_bf[...])
    x1 = _layer_norm(y, g_ref[...], b_ref[...])
    x1_ref[...] = x1
    logits = lax.dot_general(wr_ref[...], x1, (((1,), (1,)), ((), ())), preferred_element_type=F32,
                             precision=lax.Precision.HIGHEST)
    sc = jax.nn.sigmoid(logits)
    br = br_ref[...]
    e0, e1, g0, g1 = _route([sc[e:e + 1, :] for e in range(N_EXPERTS)],
                            [br[e:e + 1, :] for e in range(N_EXPERTS)])
    rows = lax.broadcasted_iota(jnp.int32, (LANES, sc.shape[1]), 0)
    gate_ref[...] = jnp.where(rows == 0, g0, jnp.where(rows == 1, g1, 0.0)).T
    rows8 = lax.broadcasted_iota(jnp.int32, eidx_ref.shape, 0)
    eidx_ref[...] = jnp.where(rows8 == 0, e0, jnp.where(rows8 == 1, e1, 0))


def _wo_call(mixed_p, mixed_s, x, w_o, ln_g, ln_b, wr_t, br, l, tm, alpha):
    t, d = x.shape
    n_p = mixed_p.shape[0] // tm
    row = lambda i: (i, 0)
    vec = pl.BlockSpec((None, 1, d), lambda i: (l, 0, 0))
    return pl.pallas_call(
        functools.partial(_wo_kernel, alpha=alpha, n_prompt_tiles=n_p),
        grid=(t // tm,),
        in_specs=[pl.BlockSpec((tm, d), lambda i: (jnp.minimum(i, n_p - 1), 0)),
                  pl.BlockSpec((tm, d), lambda i: (jnp.maximum(i - n_p, 0), 0)),
                  pl.BlockSpec((tm, d), row),
                  pl.BlockSpec((None, d, d), lambda i: (l, 0, 0), pipeline_mode=pl.Buffered(1)),
                  vec, vec,
                  pl.BlockSpec((N_EXPERTS, d), lambda i: (0, 0)),
                  pl.BlockSpec((N_EXPERTS, 1), lambda i: (0, 0))],
        out_specs=[pl.BlockSpec((tm, d), row), pl.BlockSpec((tm, LANES), row),
                   pl.BlockSpec((SUBLANES, tm), lambda i: (0, i))],
        out_shape=[jax.ShapeDtypeStruct((t, d), F32), jax.ShapeDtypeStruct((t, LANES), F32),
                   jax.ShapeDtypeStruct((SUBLANES, t), jnp.int32)],
        compiler_params=_cparams(("arbitrary",), 56),
        name="wo_ln_router",
    )(mixed_p, mixed_s, x, w_o, ln_g, ln_b, wr_t, br)


def _dispatch_kernel(eidx_ref, pos_ref, meta_ref):
    t = eidx_ref.shape[1]
    e0, e1 = eidx_ref[0:1, :], eidx_ref[1:2, :]
    rows = lax.broadcasted_iota(jnp.int32, (N_EXPERTS, t), 0)
    hit0, hit1 = rows == e0, rows == e1
    oh = jnp.where(hit0 | hit1, 1.0, 0.0)
    cnt = jnp.sum(oh, axis=1, keepdims=True)
    padded = jnp.floor((cnt + (MOE_TILE - 1)) * (1.0 / MOE_TILE)) * MOE_TILE
    erow = lax.broadcasted_iota(jnp.int32, (N_EXPERTS, 1), 0)
    off = jnp.zeros((N_EXPERTS, 1), F32)
    run = jnp.zeros((1, 1), F32)
    for e in range(N_EXPERTS):
        off = jnp.where(erow == e, run, off)
        run = run + padded[e:e + 1, :]
    c = PREFIX_CHUNK
    before = jnp.where(lax.broadcasted_iota(jnp.int32, (c, c), 0) < lax.broadcasted_iota(jnp.int32, (c, c), 1),
                       1.0, 0.0).astype(BF16)
    pos_ref[...] = jnp.zeros(pos_ref.shape, jnp.int32)
    carry = off
    for j in range(t // c):
        cs = slice(j * c, (j + 1) * c)
        ohc = oh[:, cs]
        slot = _dot(ohc.astype(BF16), before) + carry
        pos_ref[0:1, cs] = jnp.sum(jnp.where(hit0[:, cs], slot, 0.0), axis=0, keepdims=True).astype(jnp.int32)
        pos_ref[1:2, cs] = jnp.sum(jnp.where(hit1[:, cs], slot, 0.0), axis=0, keepdims=True).astype(jnp.int32)
        carry = carry + jnp.sum(ohc, axis=1, keepdims=True)
    start = lax.broadcasted_iota(jnp.int32, (N_EXPERTS, LANES), 1).astype(F32) * MOE_TILE
    te = jnp.minimum(jnp.sum(jnp.where(off + padded <= start, 1.0, 0.0), axis=0, keepdims=True), N_EXPERTS - 1.0)
    mine = lax.broadcasted_iota(jnp.int32, (N_EXPERTS, LANES), 0).astype(F32) == te
    end_valid = jnp.sum(jnp.where(mine, off + cnt, 0.0), axis=0, keepdims=True)
    nvalid = jnp.clip(end_valid - start[0:1, :], 0.0, MOE_TILE)
    ordinal = jnp.zeros((N_EXPERTS, 1), F32)
    seen = jnp.zeros((1, 1), F32)
    for e in range(N_EXPERTS):
        ordinal = jnp.where(erow == e, seen, ordinal)
        seen = seen + jnp.where(cnt[e:e + 1, :] > 0.0, 1.0, 0.0)
    nxt = erow.astype(F32)
    later = jnp.full((1, 1), -1.0, F32)
    for e in reversed(range(N_EXPERTS)):
        nxt = jnp.where((erow == e) & (later >= 0.0), later, nxt)
        later = jnp.where(cnt[e:e + 1, :] > 0.0, float(e), later)
    parity = jnp.sum(jnp.where(mine, ordinal - 2.0 * jnp.floor(ordinal * 0.5), 0.0), axis=0, keepdims=True)
    nxt_tile = jnp.sum(jnp.where(mine, nxt, 0.0), axis=0, keepdims=True)
    r8 = lax.broadcasted_iota(jnp.int32, meta_ref.shape, 0)
    meta = jnp.zeros(meta_ref.shape, F32)
    for r, v in enumerate([te, nvalid, run * (1.0 / MOE_TILE), parity, nxt_tile]):
        meta = jnp.where(r8 == r, v, meta)
    meta_ref[...] = meta.astype(jnp.int32)


def _dispatch_call(eidx):
    t = eidx.shape[1]
    assert t % PREFIX_CHUNK == 0
    return pl.pallas_call(
        _dispatch_kernel,
        out_shape=[jax.ShapeDtypeStruct((SUBLANES, t), jnp.int32), jax.ShapeDtypeStruct((SUBLANES, LANES), jnp.int32)],
        compiler_params=pltpu.CompilerParams(vmem_limit_bytes=32 * MIB),
        name="moe_dispatch",
    )(eidx)


def _row_copy(src, i, dst, j, sem):
    return pltpu.make_async_copy(src.at[pl.ds(i, 1), :], dst.at[pl.ds(j, 1), :], sem)


def _scatter_kernel(p0_ref, p1_ref, x_ref, init_hbm, xs_hbm, sem, *, rows):
    del init_hbm
    base = pl.program_id(0) * rows

    def body(r, carry):
        t = base + r
        _row_copy(x_ref, r, xs_hbm, p0_ref[t], sem).start()
        _row_copy(x_ref, r, xs_hbm, p1_ref[t], sem).start()
        return carry

    lax.fori_loop(0, rows, body, 0, unroll=8)
    for _ in range(2):
        pltpu.make_async_copy(x_ref, xs_hbm.at[pl.ds(0, rows), :], sem).wait()


def _scatter_call(pos0, pos1, x1, init, rows):
    t, d = x1.shape
    return pl.pallas_call(
        functools.partial(_scatter_kernel, rows=rows),
        grid_spec=pltpu.PrefetchScalarGridSpec(
            num_scalar_prefetch=2, grid=(t // rows,),
            in_specs=[pl.BlockSpec((rows, d), lambda i, p0, p1: (i, 0)), pl.BlockSpec(memory_space=pl.ANY)],
            out_specs=pl.BlockSpec(memory_space=pl.ANY),
            scratch_shapes=[pltpu.SemaphoreType.DMA(())]),
        out_shape=jax.ShapeDtypeStruct(init.shape, init.dtype),
        input_output_aliases={3: 0},
        compiler_params=pltpu.CompilerParams(dimension_semantics=("arbitrary",), vmem_limit_bytes=32 * MIB,
                                             disable_bounds_checks=True),
        name="moe_scatter",
    )(pos0, pos1, x1, init)


def _experts_kernel(te_ref, nv_ref, nu_ref, slot_ref, nxt_ref, xs_ref, wg_hbm, wu_hbm, wd_hbm, y_ref,
                    wg_f, wu_f, wd_f, wg_bf, wu_bf, wd_bf, sems, *, l):
    i = pl.program_id(0)
    nv = nv_ref[i]
    e = te_ref[i]
    s = slot_ref[i]
    fresh = (i == 0) | (e != te_ref[jnp.maximum(i - 1, 0)])

    def weight_copies(expert, slot):
        return [pltpu.make_async_copy(w.at[l, expert], buf.at[slot], sems.at[slot])
                for w, buf in ((wg_hbm, wg_f), (wu_hbm, wu_f), (wd_hbm, wd_f))]

    @pl.when(fresh & (nv > 0))
    def _():
        @pl.when(i == 0)
        def _():
            for cp in weight_copies(e, s):
                cp.start()

        for cp in weight_copies(e, s):
            cp.wait()
        wg_bf[...] = wg_f[s].astype(BF16)
        wu_bf[...] = wu_f[s].astype(BF16)
        wd_bf[...] = wd_f[s].astype(BF16)

        @pl.when(nxt_ref[i] != e)
        def _():
            for cp in weight_copies(nxt_ref[i], 1 - s):
                cp.start()

    @pl.when(nv > 0)
    def _():
        row = lax.broadcasted_iota(jnp.int32, xs_ref.shape, 0)
        x = jnp.where(row < nv, xs_ref[...], 0.0).astype(BF16)
        hg = _dot(x, wg_bf[...])
        hu = _dot(x, wu_bf[...])
        h = hg * _sigmoid(hg) * hu
        y_ref[...] = _dot(h.astype(BF16), wd_bf[...])

    @pl.when(nv == 0)
    def _():
        y_ref[...] = jnp.zeros(y_ref.shape, F32)


def _experts_call(meta, xs, w_gate, w_up, w_down, l):
    n_slots, d = xs.shape
    f = w_gate.shape[-1]
    n_tiles = n_slots // MOE_TILE
    any_spec = pl.BlockSpec(memory_space=pl.ANY)
    return pl.pallas_call(
        functools.partial(_experts_kernel, l=l),
        grid_spec=pltpu.PrefetchScalarGridSpec(
            num_scalar_prefetch=5, grid=(n_tiles,),
            in_specs=[pl.BlockSpec((MOE_TILE, d), lambda i, te, nv, nu, sl, nx: (jnp.minimum(i, nu[0] - 1), 0)),
                      any_spec, any_spec, any_spec],
            out_specs=pl.BlockSpec((MOE_TILE, d), lambda i, te, nv, nu, sl, nx: (i, 0)),
            scratch_shapes=[pltpu.VMEM((2, d, f), F32), pltpu.VMEM((2, d, f), F32), pltpu.VMEM((2, f, d), F32),
                            pltpu.VMEM((d, f), BF16), pltpu.VMEM((d, f), BF16), pltpu.VMEM((f, d), BF16),
                            pltpu.SemaphoreType.DMA((2,))]),
        out_shape=jax.ShapeDtypeStruct((n_slots, d), F32),
        compiler_params=_cparams(("arbitrary",), 56),
        name="moe_experts",
    )(*meta, xs, w_gate, w_up, w_down)


def _combine_kernel(p0_ref, p1_ref, y_hbm, x1_ref, gate_ref, g_ref, b_ref, out_a, out_b, buf, sems,
                    *, tm, alpha, n_prompt_tiles):
    i = pl.program_id(0)
    slot = i % 2

    def issue(tile, s):
        base = tile * tm

        def body(r, carry):
            t = base + r
            _row_copy(y_hbm, p0_ref[t], buf.at[s, 0], r, sems.at[s]).start()
            _row_copy(y_hbm, p1_ref[t], buf.at[s, 1], r, sems.at[s]).start()
            return carry

        lax.fori_loop(0, tm, body, 0, unroll=8)

    @pl.when(i == 0)
    def _():
        issue(0, 0)

    @pl.when(i + 1 < pl.num_programs(0))
    def _():
        issue(i + 1, 1 - slot)

    for k in range(2):
        pltpu.make_async_copy(y_hbm.at[pl.ds(0, tm), :], buf.at[slot, k], sems.at[slot]).wait()
    gate = gate_ref[...]
    moe = gate[:, 0:1] * buf[slot, 0] + gate[:, 1:2] * buf[slot, 1]
    x2 = _layer_norm(alpha * x1_ref[...] + moe, g_ref[...], b_ref[...])
    if n_prompt_tiles is None:
        out_a[...] = x2
        out_b[...] = x2.astype(BF16)
    else:
        @pl.when(i < n_prompt_tiles)
        def _():
            out_a[...] = x2

        @pl.when(i >= n_prompt_tiles)
        def _():
            out_b[...] = x2


def _combine_call(pos0, pos1, y, x1, gate, ln_g, ln_b, l, tm, alpha, prompt_rows=None):
    t, d = x1.shape
    row = lambda i, p0, p1: (i, 0)
    vec = pl.BlockSpec((None, 1, d), lambda i, p0, p1: (l, 0, 0))
    if prompt_rows is None:
        n_p = None
        out_specs = [pl.BlockSpec((tm, d), row), pl.BlockSpec((tm, d), row)]
        out_shape = [jax.ShapeDtypeStruct((t, d), F32), jax.ShapeDtypeStruct((t, d), BF16)]
    else:
        n_p = prompt_rows // tm
        out_specs = [pl.BlockSpec((tm, d), lambda i, p0, p1: (jnp.minimum(i, n_p - 1), 0)),
                     pl.BlockSpec((tm, d), lambda i, p0, p1: (jnp.maximum(i - n_p, 0), 0))]
        out_shape = [jax.ShapeDtypeStruct((prompt_rows, d), F32), jax.ShapeDtypeStruct((t - prompt_rows, d), F32)]
    return pl.pallas_call(
        functools.partial(_combine_kernel, tm=tm, alpha=alpha, n_prompt_tiles=n_p),
        grid_spec=pltpu.PrefetchScalarGridSpec(
            num_scalar_prefetch=2, grid=(t // tm,),
            in_specs=[pl.BlockSpec(memory_space=pl.ANY),
                      pl.BlockSpec((tm, d), row), pl.BlockSpec((tm, LANES), row), vec, vec],
            out_specs=out_specs,
            scratch_shapes=[pltpu.VMEM((2, 2, tm, d), F32), pltpu.SemaphoreType.DMA((2,))]),
        out_shape=out_shape,
        compiler_params=pltpu.CompilerParams(dimension_semantics=("arbitrary",), vmem_limit_bytes=48 * MIB,
                                             disable_bounds_checks=True),
        name="moe_combine_ln",
    )(pos0, pos1, y, x1, gate, ln_g, ln_b)


def _rope_tables(pos):
    half = ROPE_DIM // 2
    inv = jnp.float32(ROPE_THETA) ** (-jnp.arange(half, dtype=F32) / half)
    ang = pos.astype(F32)[:, None] * inv[None, :]
    cos, sin = jnp.cos(ang), jnp.sin(ang)
    z = jnp.zeros_like(cos)
    return (jnp.concatenate([cos, cos, z, z], 1),
            jnp.concatenate([-sin, z, z, z], 1),
            jnp.concatenate([z, sin, z, z], 1))


def kernel(x_prompt, x_sample, cache_kv_latent, cache_k_rope, state_conv, w_in, q_norm_g, w_q_b, kv_norm_g, w_kv_b,
           conv_w, conv_b, conv_ln_g, conv_ln_b, w_conv_pw, w_o, ln1_g, ln1_b, w_gate, w_up, w_down, ln2_g, ln2_b,
           w_router, b_router):
    nb, seq, d = x_prompt.shape
    db, ds, _ = x_sample.shape
    depth = w_in.shape[0]
    past = cache_kv_latent.shape[2]
    d_conv = conv_w.shape[-1]
    rq, rkv = q_norm_g.shape[-1], kv_norm_g.shape[-1]
    tp, tsmp = nb * seq, db * ds
    t_all = tp + tsmp
    splits = (2 * d_conv, 2 * d_conv + rq, 2 * d_conv + rq + rkv, 2 * d_conv + rq + rkv + ROPE_DIM)
    alpha = (2.0 * depth) ** 0.25
    scale = math.log2(math.e) / math.sqrt(QK_NOPE_DIM + ROPE_DIM)
    tm = 512
    tq = 512
    n_tiles = 2 * t_all // MOE_TILE + N_EXPERTS
    assert t_all % MOE_TILE == 0 and n_tiles <= LANES
    assert d == N_HEADS * V_HEAD_DIM and w_kv_b.shape[-1] == N_HEADS * HEAD_PAD
    assert tp % tm == 0 and tsmp % tm == 0 and seq % tq == 0 and tq % CHUNK == 0
    assert past % CHUNK == 0 and ds <= CHUNK and ds >= CONV_WIDTH - 1

    pos = jnp.concatenate([jnp.tile(jnp.arange(seq, dtype=jnp.int32), nb),
                           jnp.tile(past + jnp.arange(ds, dtype=jnp.int32), db)])
    tabs = _rope_tables(pos)
    lane_chunk = jnp.arange(LANES, dtype=jnp.int32)[None, :] - ROPE_DIM
    row_chunk = jnp.where(jnp.arange(t_all) < tp, pos // CHUNK, LANES)[:, None]
    qmask = jnp.where((lane_chunk >= 0) & (lane_chunk > row_chunk) & (row_chunk < LANES), NEG_BIG, 0.0).astype(F32)
    kmask = jnp.where(lane_chunk == row_chunk, 1.0, 0.0).astype(F32)
    tabs_k, tabs_q = tabs + (kmask,), tabs + (qmask,)

    w_in_t = jnp.swapaxes(w_in, 1, 2)
    cache_pe_t = jnp.swapaxes(cache_k_rope, 2, 3)
    wq = w_q_b.reshape(depth, rq, N_HEADS, QK_NOPE_DIM + ROPE_DIM)
    wq = jnp.pad(wq, ((0, 0), (0, 0), (0, 0), (0, HEAD_PAD - QK_NOPE_DIM - ROPE_DIM)))
    wq = wq.reshape(depth, rq, N_HEADS * HEAD_PAD).astype(BF16)
    vec3 = lambda a: a.reshape(depth, 1, a.shape[-1])
    q_norm_g3, kv_norm_g3 = vec3(q_norm_g), vec3(kv_norm_g)
    conv_b3, conv_ln_g3, conv_ln_b3 = vec3(conv_b), vec3(conv_ln_g), vec3(conv_ln_b)
    ln1_g3, ln1_b3, ln2_g3, ln2_b3 = vec3(ln1_g), vec3(ln1_b), vec3(ln2_g), vec3(ln2_b)
    wr_t = w_router.T
    w_o_bf = w_o.astype(BF16)
    br = b_router.reshape(N_EXPERTS, 1)

    x = jnp.concatenate([x_prompt.reshape(tp, d), x_sample.reshape(tsmp, d)], axis=0)
    x_bf = x.astype(BF16)

    spare = jnp.zeros((n_tiles * MOE_TILE, d), F32)
    keep = CONV_WIDTH - 1
    outs = [[] for _ in range(6)]
    for l in range(depth):
        u = _glu_call(x_bf, w_in_t, l, d_conv, tm, 512)
        qn, ckv_p, ckv_s, kpe_p, kpe_s, kpad = _latent_call(x_bf, w_in_t, q_norm_g3, kv_norm_g3, tabs_k, l, splits,
                                                             rq, rkv, tm, tp)
        gates = _gates_call(x_bf, w_in_t, l, splits[3], tm, 2048)
        ga_p = _conv_prompt_call(u, gates, conv_w, conv_b3, conv_ln_g3, conv_ln_b3, w_conv_pw, l, nb, seq, 512, 128)
        ga_s = _conv_sample_call(u, gates, state_conv, conv_w, conv_b3, conv_ln_g3, conv_ln_b3, w_conv_pw,
                                 l, tp, db, ds, ds)
        q = _q_call(qn, wq, tabs_q, l, tm, scale)
        k, vt = _kv_call(ckv_p, kpad, w_kv_b, l, tp, tm)
        mixed_p = _attn_prompt_call(q, k, vt, ga_p, gates, nb, seq, tq, 2 * tq)
        mixed_s = _attn_sample_call(q, cache_kv_latent, cache_pe_t, ckv_s, kpe_s, w_kv_b, ga_s, gates, l, tp, db, ds)
        x1, gate, eidx = _wo_call(mixed_p, mixed_s, x, w_o_bf, ln1_g3, ln1_b3, wr_t, br, l, tm, alpha)
        pos_rows, meta = _dispatch_call(eidx)
        pos0, pos1 = pos_rows[0], pos_rows[1]
        tile_meta = (meta[0, :n_tiles], meta[1, :n_tiles], meta[2, :1], meta[3, :n_tiles], meta[4, :n_tiles])
        xs = _scatter_call(pos0, pos1, x1, spare, tm)
        y = _experts_call(tile_meta, xs, w_gate, w_up, w_down, l)
        if l + 1 < depth:
            x, x_bf = _combine_call(pos0, pos1, y, x1, gate, ln2_g3, ln2_b3, l, MOE_TILE, alpha)
        else:
            y_prompt, y_sample = _combine_call(pos0, pos1, y, x1, gate, ln2_g3, ln2_b3, l, MOE_TILE, alpha, tp)
        spare = y
        outs[0].append(ckv_p.reshape(nb, seq, rkv))
        outs[1].append(kpe_p.reshape(nb, seq, ROPE_DIM))
        outs[2].append(u[:tp].reshape(nb, seq, d_conv)[:, seq - keep:])
        outs[3].append(ckv_s.reshape(db, ds, rkv))
        outs[4].append(kpe_s.reshape(db, ds, ROPE_DIM))
        outs[5].append(u[tp:].reshape(db, ds, d_conv)[:, ds - keep:])

    return (y_prompt.reshape(nb, seq, d), y_sample.reshape(db, ds, d)) + tuple(jnp.stack(o) for o in outs)
```

```python
import functools
import math

import jax
import jax.numpy as jnp
from jax import lax
from jax.experimental import pallas as pl
from jax.experimental.pallas import tpu as pltpu

F32 = jnp.float32
BF16 = jnp.bfloat16

CHUNK = 64
CONV_WIDTH = 31
N_HEADS = 16
QK_NOPE_DIM = 128
ROPE_DIM = 64
V_HEAD_DIM = 128
ROPE_THETA = 10000.0
N_EXPERTS = 16
N_EXPERT_GROUPS = 4
EXPERTS_PER_GROUP = N_EXPERTS // N_EXPERT_GROUPS
LN_EPS = 1e-5
RMS_EPS = 1e-6

LANES = 128
SUBLANES = 8
MOE_TILE = 256
PREFIX_CHUNK = 512
HEAD_PAD = 256
CONV_HALO = 32
CONV_OFF = CONV_HALO - (CONV_WIDTH - 1)
NEG_BIG = -1e30
MIB = 1024 * 1024


def _cparams(sem, vmem_mib):
    return pltpu.CompilerParams(dimension_semantics=sem, vmem_limit_bytes=vmem_mib * MIB)


def _dot(a, b):
    return jnp.dot(a, b, preferred_element_type=F32)


def _dot_nt(a, b):
    return lax.dot_general(a, b, (((1,), (1,)), ((), ())), preferred_element_type=F32)


def _layer_norm(y, g, b):
    mu = jnp.mean(y, axis=-1, keepdims=True)
    d = y - mu
    var = jnp.mean(d * d, axis=-1, keepdims=True)
    return d * lax.rsqrt(var + LN_EPS) * g + b


def _rms_norm(y, g):
    return y * lax.rsqrt(jnp.mean(y * y, axis=-1, keepdims=True) + RMS_EPS) * g


def _sigmoid(x):
    return 0.5 * jnp.tanh(0.5 * x) + 0.5


def _rope128(v, c, s1, s2):
    return v * c + pltpu.roll(v, 96, axis=1) * s1 + pltpu.roll(v, 32, axis=1) * s2


def _glu_kernel(x_ref, wa_ref, wg_ref, u_ref, wa_bf, wg_bf):
    @pl.when(pl.program_id(1) == 0)
    def _():
        wa_bf[...] = wa_ref[...].astype(BF16)
        wg_bf[...] = wg_ref[...].astype(BF16)

    x = x_ref[...]
    a = _dot_nt(x, wa_bf[...])
    g = _dot_nt(x, wg_bf[...])
    u_ref[...] = a * _sigmoid(g)


def _glu_call(x_bf, w_in_t, l, d_conv, tm, tn):
    t, d = x_bf.shape
    nj = d_conv // tn
    return pl.pallas_call(
        _glu_kernel,
        grid=(nj, t // tm),
        in_specs=[pl.BlockSpec((tm, d), lambda j, i: (i, 0)),
                  pl.BlockSpec((None, tn, d), lambda j, i: (l, j, 0)),
                  pl.BlockSpec((None, tn, d), lambda j, i: (l, j + nj, 0))],
        out_specs=pl.BlockSpec((tm, tn), lambda j, i: (i, j)),
        out_shape=jax.ShapeDtypeStruct((t, d_conv), F32),
        scratch_shapes=[pltpu.VMEM((tn, d), BF16), pltpu.VMEM((tn, d), BF16)],
        compiler_params=_cparams(("arbitrary", "arbitrary"), 48),
        name="glu",
    )(x_bf, w_in_t, w_in_t)


def _latent_kernel(x_ref, wq_ref, wc_ref, wk_ref, qg_ref, cg_ref, c_ref, s1_ref, s2_ref, kmask_ref,
                   qn_ref, ckvp_ref, ckvs_ref, kpep_ref, kpes_ref, kpad_ref, wq_bf, wc_bf, wk_bf, *, n_prompt_tiles):
    i = pl.program_id(0)

    @pl.when(i == 0)
    def _():
        wq_bf[...] = wq_ref[...].astype(BF16)
        wc_bf[...] = wc_ref[...].astype(BF16)
        wk_bf[...] = wk_ref[...].astype(BF16)

    x = x_ref[...]
    qn_ref[...] = _rms_norm(_dot_nt(x, wq_bf[...]), qg_ref[...]).astype(BF16)
    ckv = _rms_norm(_dot_nt(x, wc_bf[...]), cg_ref[...])
    kr = _dot_nt(x, wk_bf[...])
    k = _rope128(kr, c_ref[...], s1_ref[...], s2_ref[...])
    lane = lax.broadcasted_iota(jnp.int32, k.shape, 1)
    k = jnp.where(lane < ROPE_DIM, k, 0.0)
    kpad_ref[...] = (k + kmask_ref[...]).astype(BF16)

    @pl.when(i < n_prompt_tiles)
    def _():
        ckvp_ref[...] = ckv
        kpep_ref[...] = k[:, :ROPE_DIM]

    @pl.when(i >= n_prompt_tiles)
    def _():
        ckvs_ref[...] = ckv
        kpes_ref[...] = k[:, :ROPE_DIM]


def _latent_call(x_bf, w_in_t, q_norm_g, kv_norm_g, tabs, l, splits, rq, rkv, tm, prompt_rows):
    t, d = x_bf.shape
    assert splits[0] % rq == 0 and splits[1] % rkv == 0 and splits[2] % LANES == 0
    n_p = prompt_rows // tm
    row = lambda i: (i, 0)
    prow = lambda i: (jnp.minimum(i, n_p - 1), 0)
    srow = lambda i: (jnp.maximum(i - n_p, 0), 0)
    return pl.pallas_call(
        functools.partial(_latent_kernel, n_prompt_tiles=n_p),
        grid=(t // tm,),
        in_specs=[pl.BlockSpec((tm, d), row),
                  pl.BlockSpec((None, rq, d), lambda i: (l, splits[0] // rq, 0)),
                  pl.BlockSpec((None, rkv, d), lambda i: (l, splits[1] // rkv, 0)),
                  pl.BlockSpec((None, LANES, d), lambda i: (l, splits[2] // LANES, 0)),
                  pl.BlockSpec((None, 1, rq), lambda i: (l, 0, 0)),
                  pl.BlockSpec((None, 1, rkv), lambda i: (l, 0, 0)),
                  pl.BlockSpec((tm, LANES), row),
                  pl.BlockSpec((tm, LANES), row),
                  pl.BlockSpec((tm, LANES), row),
                  pl.BlockSpec((tm, LANES), row)],
        out_specs=[pl.BlockSpec((tm, rq), row),
                   pl.BlockSpec((tm, rkv), prow), pl.BlockSpec((tm, rkv), srow),
                   pl.BlockSpec((tm, ROPE_DIM), prow), pl.BlockSpec((tm, ROPE_DIM), srow),
                   pl.BlockSpec((tm, LANES), row)],
        out_shape=[jax.ShapeDtypeStruct((t, rq), BF16),
                   jax.ShapeDtypeStruct((prompt_rows, rkv), F32), jax.ShapeDtypeStruct((t - prompt_rows, rkv), F32),
                   jax.ShapeDtypeStruct((prompt_rows, ROPE_DIM), F32),
                   jax.ShapeDtypeStruct((t - prompt_rows, ROPE_DIM), F32),
                   jax.ShapeDtypeStruct((t, LANES), BF16)],
        scratch_shapes=[pltpu.VMEM((rq, d), BF16), pltpu.VMEM((rkv, d), BF16), pltpu.VMEM((LANES, d), BF16)],
        compiler_params=_cparams(("arbitrary",), 48),
        name="latent",
    )(x_bf, w_in_t, w_in_t, w_in_t, q_norm_g, kv_norm_g, *tabs)


def _gates_kernel(x_ref, w_hbm, o_ref, w_f32, w_bf, sem, *, l, row0, tn):
    j = pl.program_id(0)

    def block_copy(jj):
        return pltpu.make_async_copy(w_hbm.at[l, pl.ds(pl.multiple_of(row0 + jj * tn, SUBLANES), tn), :], w_f32, sem)

    @pl.when(pl.program_id(1) == 0)
    def _():
        @pl.when(j == 0)
        def _():
            block_copy(0).start()

        block_copy(j).wait()
        w_bf[...] = w_f32[...].astype(BF16)

        @pl.when(j + 1 < pl.num_programs(0))
        def _():
            block_copy(j + 1).start()

    o_ref[...] = _sigmoid(_dot_nt(x_ref[...], w_bf[...])).astype(BF16)


def _gates_call(x_bf, w_in_t, l, row0, tm, tn):
    t, d = x_bf.shape
    n = w_in_t.shape[1] - row0
    assert n % tn == 0 and row0 % SUBLANES == 0
    return pl.pallas_call(
        functools.partial(_gates_kernel, l=l, row0=row0, tn=tn),
        grid=(n // tn, t // tm),
        in_specs=[pl.BlockSpec((tm, d), lambda j, i: (i, 0)),
                  pl.BlockSpec(memory_space=pl.ANY)],
        out_specs=pl.BlockSpec((tm, tn), lambda j, i: (i, j)),
        out_shape=jax.ShapeDtypeStruct((t, n), BF16),
        scratch_shapes=[pltpu.VMEM((tn, d), F32), pltpu.VMEM((tn, d), BF16), pltpu.SemaphoreType.DMA(())],
        compiler_params=_cparams(("arbitrary", "arbitrary"), 48),
        name="gates",
    )(x_bf, w_in_t)


def _conv_tail(ext_ref, y_ref, cw_ref, cb_ref, lg_ref, lb_ref, wpw_bf, gate_ref, ga_ref, ts, rows):
    c_dim = y_ref.shape[1]
    for c in range(c_dim // LANES):
        cs = slice(c * LANES, (c + 1) * LANES)

        def rbody(r, carry, cs=cs):
            r0 = pl.multiple_of(r * rows, rows)
            win = ext_ref[pl.ds(r0, rows + CONV_HALO), cs]
            acc = jnp.zeros((rows, LANES), F32)
            for b in range(SUBLANES):
                taps = [o for o in range(CONV_OFF, CONV_OFF + CONV_WIDTH) if o % SUBLANES == b]
                n_win = rows + CONV_HALO
                shifted = win if b == 0 else pltpu.roll(win, n_win - b, axis=0)
                for o in taps:
                    acc = acc + shifted[o - b:o - b + rows, :] * cw_ref[o - CONV_OFF:o - CONV_OFF + 1, cs]
            y_ref[pl.ds(r0, rows), cs] = acc
            return carry

        lax.fori_loop(0, ts // rows, rbody, 0)

    y = _layer_norm(y_ref[...] + cb_ref[...], lg_ref[...], lb_ref[...])
    z = (y * _sigmoid(y)).astype(BF16)
    a_out = _dot(z, wpw_bf[...])
    ga_ref[...] = (gate_ref[...].astype(F32) * a_out).astype(BF16)


def _conv_prompt_kernel(u_ref, halo_ref, cw_ref, cb_ref, lg_ref, lb_ref, wpw_ref, gate_ref, ga_ref,
                        ext_ref, y_ref, wpw_bf, *, ts, rows):
    b, i = pl.program_id(0), pl.program_id(1)

    @pl.when((b == 0) & (i == 0))
    def _():
        wpw_bf[...] = wpw_ref[...].astype(BF16)

    @pl.when(i == 0)
    def _():
        ext_ref[0:CONV_HALO, :] = jnp.zeros((CONV_HALO, ext_ref.shape[1]), F32)

    @pl.when(i > 0)
    def _():
        ext_ref[0:CONV_HALO, :] = halo_ref[...]

    ext_ref[CONV_HALO:CONV_HALO + ts, :] = u_ref[...]
    _conv_tail(ext_ref, y_ref, cw_ref, cb_ref, lg_ref, lb_ref, wpw_bf, gate_ref, ga_ref, ts, rows)


def _conv_sample_kernel(u_ref, hist_ref, cw_ref, cb_ref, lg_ref, lb_ref, wpw_ref, gate_ref, ga_ref,
                        ext_ref, y_ref, wpw_bf, *, ts, rows):
    @pl.when(pl.program_id(0) == 0)
    def _():
        wpw_bf[...] = wpw_ref[...].astype(BF16)

    ext_ref[0:CONV_HALO, :] = jnp.zeros((CONV_HALO, ext_ref.shape[1]), F32)
    ext_ref[CONV_OFF:CONV_HALO, :] = hist_ref[...]
    ext_ref[CONV_HALO:CONV_HALO + ts, :] = u_ref[...]
    _conv_tail(ext_ref, y_ref, cw_ref, cb_ref, lg_ref, lb_ref, wpw_bf, gate_ref, ga_ref, ts, rows)


def _conv_weight_specs(l, c_dim, d, nargs):
    z = (lambda *a: (l, 0, 0))
    return [pl.BlockSpec((None, CONV_WIDTH, c_dim), z),
            pl.BlockSpec((None, 1, c_dim), z),
            pl.BlockSpec((None, 1, c_dim), z),
            pl.BlockSpec((None, 1, c_dim), z),
            pl.BlockSpec((None, c_dim, d), z)]


def _conv_prompt_call(u, gates, conv_w, conv_b, ln_g, ln_b, w_pw, l, nb, seq, ts, rows):
    c_dim = u.shape[1]
    d = w_pw.shape[-1]
    ns = seq // ts
    hb = ts // CONV_HALO
    tile = lambda b, i: (b * ns + i, 0)
    return pl.pallas_call(
        functools.partial(_conv_prompt_kernel, ts=ts, rows=rows),
        grid=(nb, ns),
        in_specs=[pl.BlockSpec((ts, c_dim), tile),
                  pl.BlockSpec((CONV_HALO, c_dim), lambda b, i: (jnp.maximum((b * ns + i) * hb - 1, 0), 0))]
        + _conv_weight_specs(l, c_dim, d, 2)
        + [pl.BlockSpec((ts, d), tile)],
        out_specs=pl.BlockSpec((ts, d), tile),
        out_shape=jax.ShapeDtypeStruct((nb * seq, d), BF16),
        scratch_shapes=[pltpu.VMEM((CONV_HALO + ts, c_dim), F32), pltpu.VMEM((ts, c_dim), F32),
                        pltpu.VMEM((c_dim, d), BF16)],
        compiler_params=_cparams(("arbitrary", "arbitrary"), 56),
        name="conv_prompt",
    )(u, u, conv_w, conv_b, ln_g, ln_b, w_pw, gates)


def _conv_sample_call(u, gates, state_conv, conv_w, conv_b, ln_g, ln_b, w_pw, l, row0, nb, ts, rows):
    c_dim = u.shape[1]
    d = w_pw.shape[-1]
    t0 = row0 // ts
    tile = lambda b: (t0 + b, 0)
    return pl.pallas_call(
        functools.partial(_conv_sample_kernel, ts=ts, rows=rows),
        grid=(nb,),
        in_specs=[pl.BlockSpec((ts, c_dim), tile),
                  pl.BlockSpec((None, None, CONV_WIDTH - 1, c_dim), lambda b: (l, b, 0, 0))]
        + _conv_weight_specs(l, c_dim, d, 1)
        + [pl.BlockSpec((ts, d), tile)],
        out_specs=pl.BlockSpec((ts, d), lambda b: (b, 0)),
        out_shape=jax.ShapeDtypeStruct((nb * ts, d), BF16),
        scratch_shapes=[pltpu.VMEM((CONV_HALO + ts, c_dim), F32), pltpu.VMEM((ts, c_dim), F32),
                        pltpu.VMEM((c_dim, d), BF16)],
        compiler_params=_cparams(("arbitrary",), 56),
        name="conv_sample",
    )(u, state_conv, conv_w, conv_b, ln_g, ln_b, w_pw, gates)


def _q_kernel(qn_ref, w_ref, c_ref, s1_ref, s2_ref, qmask_ref, q_ref, *, scale):
    qn = qn_ref[...]
    c, s1, s2, qmask = c_ref[...], s1_ref[...], s2_ref[...], qmask_ref[...]
    for h in range(N_HEADS):
        qh = _dot(qn, w_ref[:, h * HEAD_PAD:(h + 1) * HEAD_PAD])
        q_ref[:, h * HEAD_PAD:h * HEAD_PAD + LANES] = (qh[:, :LANES] * scale).astype(BF16)
        q_ref[:, h * HEAD_PAD + LANES:(h + 1) * HEAD_PAD] = (
            _rope128(qh[:, LANES:], c, s1, s2) * scale + qmask).astype(BF16)


def _q_call(qn, w_q, tabs, l, tm, scale):
    t, r = qn.shape
    n = w_q.shape[-1]
    row = lambda i: (i, 0)
    return pl.pallas_call(
        functools.partial(_q_kernel, scale=scale),
        grid=(t // tm,),
        in_specs=[pl.BlockSpec((tm, r), row),
                  pl.BlockSpec((None, r, n), lambda i: (l, 0, 0)),
                  pl.BlockSpec((tm, LANES), row), pl.BlockSpec((tm, LANES), row), pl.BlockSpec((tm, LANES), row),
                  pl.BlockSpec((tm, LANES), row)],
        out_specs=pl.BlockSpec((tm, n), row),
        out_shape=jax.ShapeDtypeStruct((t, n), BF16),
        compiler_params=_cparams(("arbitrary",), 48),
        name="q_proj",
    )(qn, w_q, *tabs)


def _kv_kernel(lat_ref, kpad_ref, w_ref, k_ref, vt_ref, w_bf):
    @pl.when(pl.program_id(0) == 0)
    def _():
        w_bf[...] = w_ref[...].astype(BF16)

    lat = lat_ref[...].astype(BF16)
    kpad = kpad_ref[...]
    for h in range(N_HEADS):
        kv = _dot(lat, w_bf[:, h * HEAD_PAD:(h + 1) * HEAD_PAD])
        k_ref[:, h * HEAD_PAD:h * HEAD_PAD + LANES] = kv[:, :LANES].astype(BF16)
        k_ref[:, h * HEAD_PAD + LANES:(h + 1) * HEAD_PAD] = kpad
        vt_ref[h * V_HEAD_DIM:(h + 1) * V_HEAD_DIM, :] = kv[:, LANES:].T.astype(BF16)


def _kv_call(ckv, kpad, w_kv_b, l, rows, tm):
    r = ckv.shape[1]
    n = w_kv_b.shape[-1]
    row = lambda i: (i, 0)
    return pl.pallas_call(
        _kv_kernel,
        grid=(rows // tm,),
        in_specs=[pl.BlockSpec((tm, r), row),
                  pl.BlockSpec((tm, LANES), row),
                  pl.BlockSpec((None, r, n), lambda i: (l, 0, 0))],
        out_specs=[pl.BlockSpec((tm, N_HEADS * HEAD_PAD), row),
                   pl.BlockSpec((N_HEADS * V_HEAD_DIM, tm), lambda i: (0, i))],
        out_shape=[jax.ShapeDtypeStruct((rows, N_HEADS * HEAD_PAD), BF16),
                   jax.ShapeDtypeStruct((N_HEADS * V_HEAD_DIM, rows), BF16)],
        scratch_shapes=[pltpu.VMEM((r, n), BF16)],
        compiler_params=_cparams(("arbitrary",), 48),
        name="kv_proj",
    )(ckv, kpad, w_kv_b)


def _attn_prompt_kernel(qi_ref, kj_ref, q_ref, k_ref, vt_ref, ga_ref, gb_ref, o_ref, *scratch,
                        n_steps, tq, tk, unroll, n_buf, heads):
    per_head = 2 * n_buf + 3

    def head_refs(h):
        sc = scratch[h * per_head:(h + 1) * per_head]
        return [(sc[2 * j], sc[2 * j + 1]) for j in range(n_buf)], sc[2 * n_buf], sc[2 * n_buf + 1], sc[2 * n_buf + 2]

    def stage_scores(h, t, s_ref, mx_ref):
        q = q_ref[pl.ds(pl.multiple_of(qi_ref[t] * tq, tq), tq), h * HEAD_PAD:(h + 1) * HEAD_PAD]
        k = k_ref[pl.ds(pl.multiple_of(kj_ref[t] * tk, tk), tk), h * HEAD_PAD:(h + 1) * HEAD_PAD]
        s = _dot_nt(k, q)
        s_ref[...] = s
        mx_ref[...] = jnp.max(s, axis=0, keepdims=True)

    def step(h, t, j):
        bufs, acc_ref, m_ref, l_ref = head_refs(h)
        s_cur, mx_cur = bufs[j]
        stage_scores(h, t + n_buf - 1, *bufs[(j - 1) % n_buf])
        cols = pl.ds(pl.multiple_of(qi_ref[t] * tq, tq), tq)
        m_old = m_ref[:, cols]
        m_new = jnp.maximum(m_old, mx_cur[...])
        alpha = jnp.exp2(m_old - m_new)
        p = jnp.exp2(s_cur[...] - m_new)
        l_ref[:, cols] = alpha * l_ref[:, cols] + jnp.sum(p, axis=0, keepdims=True)
        keys = pl.ds(pl.multiple_of(kj_ref[t] * tk, tk), tk)
        pv = _dot(vt_ref[h * V_HEAD_DIM:(h + 1) * V_HEAD_DIM, keys], p.astype(BF16))
        acc_ref[:, cols] = alpha * acc_ref[:, cols] + pv
        m_ref[:, cols] = m_new

    for h in range(heads):
        bufs, acc_ref, m_ref, l_ref = head_refs(h)
        m_ref[...] = jnp.full(m_ref.shape, NEG_BIG, F32)
        l_ref[...] = jnp.zeros(l_ref.shape, F32)
        acc_ref[...] = jnp.zeros(acc_ref.shape, F32)
        for j in range(n_buf - 1):
            stage_scores(h, j, *bufs[j])

    def body(u, carry):
        for j in range(unroll):
            for h in range(heads):
                step(h, unroll * u + j, j % n_buf)
        return carry

    lax.fori_loop(0, n_steps // unroll, body, 0)

    def finish(qi, carry):
        rows = pl.ds(pl.multiple_of(qi * tq, tq), tq)
        for h in range(heads):
            _, acc_ref, _, l_ref = head_refs(h)
            cols = slice(h * V_HEAD_DIM, (h + 1) * V_HEAD_DIM)
            o = (acc_ref[:, rows] / l_ref[:, rows]).T
            o_ref[rows, cols] = (ga_ref[rows, cols].astype(F32) + gb_ref[rows, cols].astype(F32) * o).astype(BF16)
        return carry

    lax.fori_loop(0, o_ref.shape[0] // tq, finish, 0)


def _attn_prompt_call(q, k, vt, ga, gates, nb, seq, tq, tk):
    d = N_HEADS * V_HEAD_DIM
    nq = seq // tq
    n_buf, unroll, heads = 2, 4, 1
    assert nq >= 2 and seq // CHUNK <= LANES - ROPE_DIM and seq % tk == 0
    tiles = [(qi, kj) for qi in range(nq) for kj in range(-(-(qi + 1) * tq // tk))]
    masked_tile = (0, -(-tq // tk))
    assert (masked_tile[1] + 1) * tk <= seq
    tiles += [masked_tile] * (-len(tiles) % unroll)
    n_steps = len(tiles)
    tiles += [(0, 0)] * (n_buf - 1)
    qi_tab = jnp.asarray([t[0] for t in tiles], jnp.int32)
    kj_tab = jnp.asarray([t[1] for t in tiles], jnp.int32)
    bh = lambda b, h, qt, kt: (b, h)
    return pl.pallas_call(
        functools.partial(_attn_prompt_kernel, n_steps=n_steps, tq=tq, tk=tk, unroll=unroll, n_buf=n_buf,
                          heads=heads),
        grid_spec=pltpu.PrefetchScalarGridSpec(
            num_scalar_prefetch=2, grid=(nb, N_HEADS // heads),
            in_specs=[pl.BlockSpec((seq, heads * HEAD_PAD), bh),
                      pl.BlockSpec((seq, heads * HEAD_PAD), bh),
                      pl.BlockSpec((heads * V_HEAD_DIM, seq), lambda b, h, qt, kt: (h, b)),
                      pl.BlockSpec((seq, heads * V_HEAD_DIM), bh),
                      pl.BlockSpec((seq, heads * V_HEAD_DIM), lambda b, h, qt, kt: (b, N_HEADS // heads + h))],
            out_specs=pl.BlockSpec((seq, heads * V_HEAD_DIM), bh),
            scratch_shapes=([pltpu.VMEM((tk, tq), F32), pltpu.VMEM((1, tq), F32)] * n_buf
                            + [pltpu.VMEM((V_HEAD_DIM, seq), F32), pltpu.VMEM((1, seq), F32),
                               pltpu.VMEM((1, seq), F32)]) * heads),
        out_shape=jax.ShapeDtypeStruct((nb * seq, d), BF16),
        compiler_params=_cparams(("arbitrary", "arbitrary"), 56),
        name="attn_prompt",
    )(qi_tab, kj_tab, q, k, vt, ga, gates)


def _attn_sample_kernel(q_ref, latp_ref, kpep_ref, latn_ref, kpen_ref, w_ref, ga_ref, gb_ref, o_ref,
                        w_bf, qlat_ref, qpe_ref, kpp_ref, kpn_ref, olat_ref, *, ds):
    @pl.when(pl.program_id(0) == 0)
    def _():
        w_bf[...] = w_ref[...].astype(BF16)
        kpp_ref[...] = jnp.zeros(kpp_ref.shape, BF16)
        kpn_ref[...] = jnp.zeros(kpn_ref.shape, BF16)

    kpp_ref[:ROPE_DIM, :] = kpep_ref[...].astype(BF16)
    kpn_ref[:, :ROPE_DIM] = kpen_ref[...].astype(BF16)
    for h in range(N_HEADS):
        rows = slice(h * ds, (h + 1) * ds)
        qn = q_ref[:, h * HEAD_PAD:h * HEAD_PAD + LANES]
        qlat_ref[rows, :] = _dot_nt(qn, w_bf[:, h * HEAD_PAD:h * HEAD_PAD + LANES]).astype(BF16)
        qpe_ref[rows, :] = q_ref[:, h * HEAD_PAD + LANES:(h + 1) * HEAD_PAD]

    lat_p = latp_ref[...].astype(BF16)
    lat_n = latn_ref[...].astype(BF16)
    qlat, qpe = qlat_ref[...], qpe_ref[...]
    s_p = _dot_nt(qlat, lat_p) + _dot(qpe, kpp_ref[...])
    s_n = _dot_nt(qlat, lat_n) + _dot_nt(qpe, kpn_ref[...])
    m = jnp.maximum(jnp.max(s_p, axis=1, keepdims=True), jnp.max(s_n, axis=1, keepdims=True))
    p_p = jnp.exp2(s_p - m)
    p_n = jnp.exp2(s_n - m)
    den = jnp.sum(p_p, axis=1, keepdims=True) + jnp.sum(p_n, axis=1, keepdims=True)
    o_lat = _dot(p_p.astype(BF16), lat_p) + _dot(p_n.astype(BF16), lat_n)
    olat_ref[...] = (o_lat / den).astype(BF16)
    for h in range(N_HEADS):
        cols = slice(h * V_HEAD_DIM, (h + 1) * V_HEAD_DIM)
        o = _dot(olat_ref[h * ds:(h + 1) * ds, :], w_bf[:, h * HEAD_PAD + LANES:(h + 1) * HEAD_PAD])
        o_ref[:, cols] = (ga_ref[:, cols].astype(F32) + gb_ref[:, cols].astype(F32) * o).astype(BF16)


def _attn_sample_call(q, cache_lat, cache_pe, ckv_s, kpe_s, w_kv_b, ga_s, gates, l, row0, nb, ds):
    past, r = cache_lat.shape[2], cache_lat.shape[3]
    d = N_HEADS * V_HEAD_DIM
    t0 = row0 // ds
    tile = lambda b: (t0 + b, 0)
    return pl.pallas_call(
        functools.partial(_attn_sample_kernel, ds=ds),
        grid=(nb,),
        in_specs=[pl.BlockSpec((ds, N_HEADS * HEAD_PAD), tile),
                  pl.BlockSpec((None, None, past, r), lambda b: (l, b, 0, 0)),
                  pl.BlockSpec((None, None, ROPE_DIM, past), lambda b: (l, b, 0, 0)),
                  pl.BlockSpec((ds, r), lambda b: (b, 0)),
                  pl.BlockSpec((ds, ROPE_DIM), lambda b: (b, 0)),
                  pl.BlockSpec((None, r, N_HEADS * HEAD_PAD), lambda b: (l, 0, 0)),
                  pl.BlockSpec((ds, d), lambda b: (b, 0)),
                  pl.BlockSpec((ds, d), lambda b: (t0 + b, 1))],
        out_specs=pl.BlockSpec((ds, d), lambda b: (b, 0)),
        out_shape=jax.ShapeDtypeStruct((nb * ds, d), BF16),
        scratch_shapes=[pltpu.VMEM((r, N_HEADS * HEAD_PAD), BF16),
                        pltpu.VMEM((N_HEADS * ds, r), BF16),
                        pltpu.VMEM((N_HEADS * ds, LANES), BF16),
                        pltpu.VMEM((LANES, past), BF16),
                        pltpu.VMEM((ds, LANES), BF16),
                        pltpu.VMEM((N_HEADS * ds, r), BF16)],
        compiler_params=_cparams(("arbitrary",), 56),
        name="attn_sample",
    )(q, cache_lat, cache_pe, ckv_s, kpe_s, w_kv_b, ga_s, gates)


def _route(scores, bias):
    sel = [s + b for s, b in zip(scores, bias)]
    n = EXPERTS_PER_GROUP
    grp = []
    for g in range(N_EXPERT_GROUPS):
        v = sel[g * n:(g + 1) * n]
        best = None
        for i in range(n):
            for j in range(i + 1, n):
                pair = v[i] + v[j]
                best = pair if best is None else jnp.maximum(best, pair)
        grp.append(best)
    g_idx = jnp.zeros_like(grp[0], dtype=jnp.int32)
    g_best = grp[0]
    for g in range(1, N_EXPERT_GROUPS):
        better = grp[g] > g_best
        g_idx = jnp.where(better, g, g_idx)
        g_best = jnp.where(better, grp[g], g_best)

    def pick(rows_by_group):
        out = rows_by_group[0]
        for g in range(1, N_EXPERT_GROUPS):
            out = jnp.where(g_idx == g, rows_by_group[g], out)
        return out

    in_sel = [pick([sel[g * n + j] for g in range(N_EXPERT_GROUPS)]) for j in range(n)]
    in_sc = [pick([scores[g * n + j] for g in range(N_EXPERT_GROUPS)]) for j in range(n)]

    def argmax_first(vals, excluded):
        idx = None
        best = None
        for j in range(n):
            v = vals[j] if excluded is None else jnp.where(excluded == j, -jnp.inf, vals[j])
            if best is None:
                best, idx = v, jnp.zeros_like(g_idx)
            else:
                better = v > best
                idx = jnp.where(better, j, idx)
                best = jnp.where(better, v, best)
        return idx

    l0 = argmax_first(in_sel, None)
    l1 = argmax_first(in_sel, l0)

    def take(vals, idx):
        out = vals[0]
        for j in range(1, n):
            out = jnp.where(idx == j, vals[j], out)
        return out

    w0, w1 = take(in_sc, l0), take(in_sc, l1)
    tot = w0 + w1
    return g_idx * n + l0, g_idx * n + l1, w0 / tot, w1 / tot


def _wo_kernel(mixp_ref, mixs_ref, x_ref, w_bf, g_ref, b_ref, wrh_ref, wrl_ref, br_ref, x1_ref, gate_ref, eidx_ref,
               *, alpha, n_prompt_tiles):
    mix = jnp.where(pl.program_id(0) < n_prompt_tiles, mixp_ref[...], mixs_ref[...])
    y = alpha * x_ref[...] + _dot(mix, w_bf[...])
    x1 = _layer_norm(y, g_ref[...], b_ref[...])
    x1_ref[...] = x1
    x_hi = x1.astype(BF16)
    x_lo = (x1 - x_hi.astype(F32)).astype(BF16)
    logits = _dot_nt(wrh_ref[...], x_hi) + (_dot_nt(wrh_ref[...], x_lo) + _dot_nt(wrl_ref[...], x_hi))
    sc = jax.nn.sigmoid(logits)
    br = br_ref[...]
    e0, e1, g0, g1 = _route([sc[e:e + 1, :] for e in range(N_EXPERTS)],
                            [br[e:e + 1, :] for e in range(N_EXPERTS)])
    rows = lax.broadcasted_iota(jnp.int32, (LANES, sc.shape[1]), 0)
    gate_ref[...] = jnp.where(rows == 0, g0, jnp.where(rows == 1, g1, 0.0)).T
    rows8 = lax.broadcasted_iota(jnp.int32, eidx_ref.shape, 0)
    eidx_ref[...] = jnp.where(rows8 == 0, e0, jnp.where(rows8 == 1, e1, 0))


def _wo_call(mixed_p, mixed_s, x, w_o, ln_g, ln_b, wr_hi, wr_lo, br, l, tm, alpha):
    t, d = x.shape
    n_p = mixed_p.shape[0] // tm
    row = lambda i: (i, 0)
    vec = pl.BlockSpec((None, 1, d), lambda i: (l, 0, 0))
    return pl.pallas_call(
        functools.partial(_wo_kernel, alpha=alpha, n_prompt_tiles=n_p),
        grid=(t // tm,),
        in_specs=[pl.BlockSpec((tm, d), lambda i: (jnp.minimum(i, n_p - 1), 0)),
                  pl.BlockSpec((tm, d), lambda i: (jnp.maximum(i - n_p, 0), 0)),
                  pl.BlockSpec((tm, d), row),
                  pl.BlockSpec((None, d, d), lambda i: (l, 0, 0), pipeline_mode=pl.Buffered(1)),
                  vec, vec,
                  pl.BlockSpec((N_EXPERTS, d), lambda i: (0, 0)),
                  pl.BlockSpec((N_EXPERTS, d), lambda i: (0, 0)),
                  pl.BlockSpec((N_EXPERTS, 1), lambda i: (0, 0))],
        out_specs=[pl.BlockSpec((tm, d), row), pl.BlockSpec((tm, LANES), row),
                   pl.BlockSpec((SUBLANES, tm), lambda i: (0, i))],
        out_shape=[jax.ShapeDtypeStruct((t, d), F32), jax.ShapeDtypeStruct((t, LANES), F32),
                   jax.ShapeDtypeStruct((SUBLANES, t), jnp.int32)],
        compiler_params=_cparams(("arbitrary",), 56),
        name="wo_ln_router",
    )(mixed_p, mixed_s, x, w_o, ln_g, ln_b, wr_hi, wr_lo, br)


def _dispatch_kernel(eidx_ref, pos_ref, meta_ref):
    t = eidx_ref.shape[1]
    e0, e1 = eidx_ref[0:1, :], eidx_ref[1:2, :]
    rows = lax.broadcasted_iota(jnp.int32, (N_EXPERTS, t), 0)
    hit0, hit1 = rows == e0, rows == e1
    oh = jnp.where(hit0 | hit1, 1.0, 0.0)
    cnt = jnp.sum(oh, axis=1, keepdims=True)
    padded = jnp.floor((cnt + (MOE_TILE - 1)) * (1.0 / MOE_TILE)) * MOE_TILE
    erow = lax.broadcasted_iota(jnp.int32, (N_EXPERTS, 1), 0)
    off = jnp.zeros((N_EXPERTS, 1), F32)
    run = jnp.zeros((1, 1), F32)
    for e in range(N_EXPERTS):
        off = jnp.where(erow == e, run, off)
        run = run + padded[e:e + 1, :]
    c = PREFIX_CHUNK
    before = jnp.where(lax.broadcasted_iota(jnp.int32, (c, c), 0) < lax.broadcasted_iota(jnp.int32, (c, c), 1),
                       1.0, 0.0).astype(BF16)
    pos_ref[...] = jnp.zeros(pos_ref.shape, jnp.int32)
    carry = off
    for j in range(t // c):
        cs = slice(j * c, (j + 1) * c)
        ohc = oh[:, cs]
        slot = _dot(ohc.astype(BF16), before) + carry
        pos_ref[0:1, cs] = jnp.sum(jnp.where(hit0[:, cs], slot, 0.0), axis=0, keepdims=True).astype(jnp.int32)
        pos_ref[1:2, cs] = jnp.sum(jnp.where(hit1[:, cs], slot, 0.0), axis=0, keepdims=True).astype(jnp.int32)
        carry = carry + jnp.sum(ohc, axis=1, keepdims=True)
    start = lax.broadcasted_iota(jnp.int32, (N_EXPERTS, LANES), 1).astype(F32) * MOE_TILE
    te = jnp.minimum(jnp.sum(jnp.where(off + padded <= start, 1.0, 0.0), axis=0, keepdims=True), N_EXPERTS - 1.0)
    mine = lax.broadcasted_iota(jnp.int32, (N_EXPERTS, LANES), 0).astype(F32) == te
    end_valid = jnp.sum(jnp.where(mine, off + cnt, 0.0), axis=0, keepdims=True)
    nvalid = jnp.clip(end_valid - start[0:1, :], 0.0, MOE_TILE)
    ordinal = jnp.zeros((N_EXPERTS, 1), F32)
    seen = jnp.zeros((1, 1), F32)
    for e in range(N_EXPERTS):
        ordinal = jnp.where(erow == e, seen, ordinal)
        seen = seen + jnp.where(cnt[e:e + 1, :] > 0.0, 1.0, 0.0)
    nxt = erow.astype(F32)
    later = jnp.full((1, 1), -1.0, F32)
    for e in reversed(range(N_EXPERTS)):
        nxt = jnp.where((erow == e) & (later >= 0.0), later, nxt)
        later = jnp.where(cnt[e:e + 1, :] > 0.0, float(e), later)
    parity = jnp.sum(jnp.where(mine, ordinal - 2.0 * jnp.floor(ordinal * 0.5), 0.0), axis=0, keepdims=True)
    nxt_tile = jnp.sum(jnp.where(mine, nxt, 0.0), axis=0, keepdims=True)
    r8 = lax.broadcasted_iota(jnp.int32, meta_ref.shape, 0)
    meta = jnp.zeros(meta_ref.shape, F32)
    for r, v in enumerate([te, nvalid, run * (1.0 / MOE_TILE), parity, nxt_tile]):
        meta = jnp.where(r8 == r, v, meta)
    meta_ref[...] = meta.astype(jnp.int32)


def _dispatch_call(eidx):
    t = eidx.shape[1]
    assert t % PREFIX_CHUNK == 0
    return pl.pallas_call(
        _dispatch_kernel,
        out_shape=[jax.ShapeDtypeStruct((SUBLANES, t), jnp.int32), jax.ShapeDtypeStruct((SUBLANES, LANES), jnp.int32)],
        compiler_params=pltpu.CompilerParams(vmem_limit_bytes=32 * MIB),
        name="moe_dispatch",
    )(eidx)


def _row_copy(src, i, dst, j, sem):
    return pltpu.make_async_copy(src.at[pl.ds(i, 1), :], dst.at[pl.ds(j, 1), :], sem)


def _scatter_kernel(p0_ref, p1_ref, x_ref, init_hbm, xs_hbm, sem, *, rows):
    del init_hbm
    base = pl.program_id(0) * rows

    def body(r, carry):
        t = base + r
        _row_copy(x_ref, r, xs_hbm, p0_ref[t], sem).start()
        _row_copy(x_ref, r, xs_hbm, p1_ref[t], sem).start()
        return carry

    lax.fori_loop(0, rows, body, 0, unroll=8)
    for _ in range(2):
        pltpu.make_async_copy(x_ref, xs_hbm.at[pl.ds(0, rows), :], sem).wait()


def _scatter_call(pos0, pos1, x1, init, rows):
    t, d = x1.shape
    return pl.pallas_call(
        functools.partial(_scatter_kernel, rows=rows),
        grid_spec=pltpu.PrefetchScalarGridSpec(
            num_scalar_prefetch=2, grid=(t // rows,),
            in_specs=[pl.BlockSpec((rows, d), lambda i, p0, p1: (i, 0)), pl.BlockSpec(memory_space=pl.ANY)],
            out_specs=pl.BlockSpec(memory_space=pl.ANY),
            scratch_shapes=[pltpu.SemaphoreType.DMA(())]),
        out_shape=jax.ShapeDtypeStruct(init.shape, init.dtype),
        input_output_aliases={3: 0},
        compiler_params=pltpu.CompilerParams(dimension_semantics=("arbitrary",), vmem_limit_bytes=32 * MIB,
                                             disable_bounds_checks=True),
        name="moe_scatter",
    )(pos0, pos1, x1, init)


def _experts_kernel(te_ref, nv_ref, nu_ref, slot_ref, nxt_ref, xs_ref, wg_hbm, wu_hbm, wd_hbm, y_ref,
                    wg_f, wu_f, wd_f, wg_bf, wu_bf, wd_bf, sems, *, l):
    i = pl.program_id(0)
    nv = nv_ref[i]
    e = te_ref[i]
    s = slot_ref[i]
    fresh = (i == 0) | (e != te_ref[jnp.maximum(i - 1, 0)])

    def weight_copies(expert, slot):
        return [pltpu.make_async_copy(w.at[l, expert], buf.at[slot], sems.at[slot])
                for w, buf in ((wg_hbm, wg_f), (wu_hbm, wu_f), (wd_hbm, wd_f))]

    @pl.when(fresh & (nv > 0))
    def _():
        @pl.when(i == 0)
        def _():
            for cp in weight_copies(e, s):
                cp.start()

        for cp in weight_copies(e, s):
            cp.wait()
        wg_bf[...] = wg_f[s].astype(BF16)
        wu_bf[...] = wu_f[s].astype(BF16)
        wd_bf[...] = wd_f[s].astype(BF16)

        @pl.when(nxt_ref[i] != e)
        def _():
            for cp in weight_copies(nxt_ref[i], 1 - s):
                cp.start()

    @pl.when(nv > 0)
    def _():
        row = lax.broadcasted_iota(jnp.int32, xs_ref.shape, 0)
        x = jnp.where(row < nv, xs_ref[...], 0.0).astype(BF16)
        hg = _dot(x, wg_bf[...])
        hu = _dot(x, wu_bf[...])
        h = hg * _sigmoid(hg) * hu
        y_ref[...] = _dot(h.astype(BF16), wd_bf[...])

    @pl.when(nv == 0)
    def _():
        y_ref[...] = jnp.zeros(y_ref.shape, F32)


def _experts_call(meta, xs, w_gate, w_up, w_down, l):
    n_slots, d = xs.shape
    f = w_gate.shape[-1]
    n_tiles = n_slots // MOE_TILE
    any_spec = pl.BlockSpec(memory_space=pl.ANY)
    return pl.pallas_call(
        functools.partial(_experts_kernel, l=l),
        grid_spec=pltpu.PrefetchScalarGridSpec(
            num_scalar_prefetch=5, grid=(n_tiles,),
            in_specs=[pl.BlockSpec((MOE_TILE, d), lambda i, te, nv, nu, sl, nx: (jnp.minimum(i, nu[0] - 1), 0)),
                      any_spec, any_spec, any_spec],
            out_specs=pl.BlockSpec((MOE_TILE, d), lambda i, te, nv, nu, sl, nx: (i, 0)),
            scratch_shapes=[pltpu.VMEM((2, d, f), F32), pltpu.VMEM((2, d, f), F32), pltpu.VMEM((2, f, d), F32),
                            pltpu.VMEM((d, f), BF16), pltpu.VMEM((d, f), BF16), pltpu.VMEM((f, d), BF16),
                            pltpu.SemaphoreType.DMA((2,))]),
        out_shape=jax.ShapeDtypeStruct((n_slots, d), F32),
        compiler_params=_cparams(("arbitrary",), 56),
        name="moe_experts",
    )(*meta, xs, w_gate, w_up, w_down)


def _combine_kernel(p0_ref, p1_ref, y_hbm, x1_ref, gate_ref, g_ref, b_ref, out_a, out_b, buf, sems,
                    *, tm, alpha, n_prompt_tiles):
    i = pl.program_id(0)
    slot = i % 2

    def issue(tile, s):
        base = tile * tm

        def body(r, carry):
            t = base + r
            _row_copy(y_hbm, p0_ref[t], buf.at[s, 0], r, sems.at[s]).start()
            _row_copy(y_hbm, p1_ref[t], buf.at[s, 1], r, sems.at[s]).start()
            return carry

        lax.fori_loop(0, tm, body, 0, unroll=8)

    @pl.when(i == 0)
    def _():
        issue(0, 0)

    @pl.when(i + 1 < pl.num_programs(0))
    def _():
        issue(i + 1, 1 - slot)

    for k in range(2):
        pltpu.make_async_copy(y_hbm.at[pl.ds(0, tm), :], buf.at[slot, k], sems.at[slot]).wait()
    gate = gate_ref[...]
    moe = gate[:, 0:1] * buf[slot, 0] + gate[:, 1:2] * buf[slot, 1]
    x2 = _layer_norm(alpha * x1_ref[...] + moe, g_ref[...], b_ref[...])
    if n_prompt_tiles is None:
        out_a[...] = x2
        out_b[...] = x2.astype(BF16)
    else:
        @pl.when(i < n_prompt_tiles)
        def _():
            out_a[...] = x2

        @pl.when(i >= n_prompt_tiles)
        def _():
            out_b[...] = x2


def _combine_call(pos0, pos1, y, x1, gate, ln_g, ln_b, l, tm, alpha, prompt_rows=None):
    t, d = x1.shape
    row = lambda i, p0, p1: (i, 0)
    vec = pl.BlockSpec((None, 1, d), lambda i, p0, p1: (l, 0, 0))
    if prompt_rows is None:
        n_p = None
        out_specs = [pl.BlockSpec((tm, d), row), pl.BlockSpec((tm, d), row)]
        out_shape = [jax.ShapeDtypeStruct((t, d), F32), jax.ShapeDtypeStruct((t, d), BF16)]
    else:
        n_p = prompt_rows // tm
        out_specs = [pl.BlockSpec((tm, d), lambda i, p0, p1: (jnp.minimum(i, n_p - 1), 0)),
                     pl.BlockSpec((tm, d), lambda i, p0, p1: (jnp.maximum(i - n_p, 0), 0))]
        out_shape = [jax.ShapeDtypeStruct((prompt_rows, d), F32), jax.ShapeDtypeStruct((t - prompt_rows, d), F32)]
    return pl.pallas_call(
        functools.partial(_combine_kernel, tm=tm, alpha=alpha, n_prompt_tiles=n_p),
        grid_spec=pltpu.PrefetchScalarGridSpec(
            num_scalar_prefetch=2, grid=(t // tm,),
            in_specs=[pl.BlockSpec(memory_space=pl.ANY),
                      pl.BlockSpec((tm, d), row), pl.BlockSpec((tm, LANES), row), vec, vec],
            out_specs=out_specs,
            scratch_shapes=[pltpu.VMEM((2, 2, tm, d), F32), pltpu.SemaphoreType.DMA((2,))]),
        out_shape=out_shape,
        compiler_params=pltpu.CompilerParams(dimension_semantics=("arbitrary",), vmem_limit_bytes=48 * MIB,
                                             disable_bounds_checks=True),
        name="moe_combine_ln",
    )(pos0, pos1, y, x1, gate, ln_g, ln_b)


def _rope_tables(pos):
    half = ROPE_DIM // 2
    inv = jnp.float32(ROPE_THETA) ** (-jnp.arange(half, dtype=F32) / half)
    ang = pos.astype(F32)[:, None] * inv[None, :]
    cos, sin = jnp.cos(ang), jnp.sin(ang)
    z = jnp.zeros_like(cos)
    return (jnp.concatenate([cos, cos, z, z], 1),
            jnp.concatenate([-sin, z, z, z], 1),
            jnp.concatenate([z, sin, z, z], 1))


def kernel(x_prompt, x_sample, cache_kv_latent, cache_k_rope, state_conv, w_in, q_norm_g, w_q_b, kv_norm_g, w_kv_b,
           conv_w, conv_b, conv_ln_g, conv_ln_b, w_conv_pw, w_o, ln1_g, ln1_b, w_gate, w_up, w_down, ln2_g, ln2_b,
           w_router, b_router):
    nb, seq, d = x_prompt.shape
    db, ds, _ = x_sample.shape
    depth = w_in.shape[0]
    past = cache_kv_latent.shape[2]
    d_conv = conv_w.shape[-1]
    rq, rkv = q_norm_g.shape[-1], kv_norm_g.shape[-1]
    tp, tsmp = nb * seq, db * ds
    t_all = tp + tsmp
    splits = (2 * d_conv, 2 * d_conv + rq, 2 * d_conv + rq + rkv, 2 * d_conv + rq + rkv + ROPE_DIM)
    alpha = (2.0 * depth) ** 0.25
    scale = math.log2(math.e) / math.sqrt(QK_NOPE_DIM + ROPE_DIM)
    tm = 512
    tq = 512
    n_tiles = 2 * t_all // MOE_TILE + N_EXPERTS
    assert t_all % MOE_TILE == 0 and n_tiles <= LANES
    assert d == N_HEADS * V_HEAD_DIM and w_kv_b.shape[-1] == N_HEADS * HEAD_PAD
    assert tp % tm == 0 and tsmp % tm == 0 and seq % tq == 0 and tq % CHUNK == 0
    assert past % CHUNK == 0 and ds <= CHUNK and ds >= CONV_WIDTH - 1

    pos = jnp.concatenate([jnp.tile(jnp.arange(seq, dtype=jnp.int32), nb),
                           jnp.tile(past + jnp.arange(ds, dtype=jnp.int32), db)])
    tabs = _rope_tables(pos)
    lane_chunk = jnp.arange(LANES, dtype=jnp.int32)[None, :] - ROPE_DIM
    row_chunk = jnp.where(jnp.arange(t_all) < tp, pos // CHUNK, LANES)[:, None]
    qmask = jnp.where((lane_chunk >= 0) & (lane_chunk > row_chunk) & (row_chunk < LANES), NEG_BIG, 0.0).astype(F32)
    kmask = jnp.where(lane_chunk == row_chunk, 1.0, 0.0).astype(F32)
    tabs_k, tabs_q = tabs + (kmask,), tabs + (qmask,)

    w_in_t = jnp.swapaxes(w_in, 1, 2)
    cache_pe_t = jnp.swapaxes(cache_k_rope, 2, 3)
    wq = w_q_b.reshape(depth, rq, N_HEADS, QK_NOPE_DIM + ROPE_DIM)
    wq = jnp.pad(wq, ((0, 0), (0, 0), (0, 0), (0, HEAD_PAD - QK_NOPE_DIM - ROPE_DIM)))
    wq = wq.reshape(depth, rq, N_HEADS * HEAD_PAD).astype(BF16)
    vec3 = lambda a: a.reshape(depth, 1, a.shape[-1])
    q_norm_g3, kv_norm_g3 = vec3(q_norm_g), vec3(kv_norm_g)
    conv_b3, conv_ln_g3, conv_ln_b3 = vec3(conv_b), vec3(conv_ln_g), vec3(conv_ln_b)
    ln1_g3, ln1_b3, ln2_g3, ln2_b3 = vec3(ln1_g), vec3(ln1_b), vec3(ln2_g), vec3(ln2_b)
    wr_t = w_router.T
    wr_hi = wr_t.astype(BF16)
    wr_lo = (wr_t - wr_hi.astype(F32)).astype(BF16)
    w_o_bf = w_o.astype(BF16)
    br = b_router.reshape(N_EXPERTS, 1)

    x = jnp.concatenate([x_prompt.reshape(tp, d), x_sample.reshape(tsmp, d)], axis=0)
    x_bf = x.astype(BF16)

    spare = jnp.zeros((n_tiles * MOE_TILE, d), F32)
    keep = CONV_WIDTH - 1
    outs = [[] for _ in range(6)]
    for l in range(depth):
        u = _glu_call(x_bf, w_in_t, l, d_conv, tm, 512)
        qn, ckv_p, ckv_s, kpe_p, kpe_s, kpad = _latent_call(x_bf, w_in_t, q_norm_g3, kv_norm_g3, tabs_k, l, splits,
                                                             rq, rkv, tm, tp)
        gates = _gates_call(x_bf, w_in_t, l, splits[3], tm, 2048)
        ga_p = _conv_prompt_call(u, gates, conv_w, conv_b3, conv_ln_g3, conv_ln_b3, w_conv_pw, l, nb, seq, 512, 128)
        ga_s = _conv_sample_call(u, gates, state_conv, conv_w, conv_b3, conv_ln_g3, conv_ln_b3, w_conv_pw,
                                 l, tp, db, ds, ds)
        q = _q_call(qn, wq, tabs_q, l, tm, scale)
        k, vt = _kv_call(ckv_p, kpad, w_kv_b, l, tp, tm)
        mixed_p = _attn_prompt_call(q, k, vt, ga_p, gates, nb, seq, tq, 2 * tq)
        mixed_s = _attn_sample_call(q, cache_kv_latent, cache_pe_t, ckv_s, kpe_s, w_kv_b, ga_s, gates, l, tp, db, ds)
        x1, gate, eidx = _wo_call(mixed_p, mixed_s, x, w_o_bf, ln1_g3, ln1_b3, wr_hi, wr_lo, br, l, tm, alpha)
        pos_rows, meta = _dispatch_call(eidx)
        pos0, pos1 = pos_rows[0], pos_rows[1]
        tile_meta = (meta[0, :n_tiles], meta[1, :n_tiles], meta[2, :1], meta[3, :n_tiles], meta[4, :n_tiles])
        xs = _scatter_call(pos0, pos1, x1, spare, tm)
        y = _experts_call(tile_meta, xs, w_gate, w_up, w_down, l)
        if l + 1 < depth:
            x, x_bf = _combine_call(pos0, pos1, y, x1, gate, ln2_g3, ln2_b3, l, MOE_TILE, alpha)
        else:
            y_prompt, y_sample = _combine_call(pos0, pos1, y, x1, gate, ln2_g3, ln2_b3, l, MOE_TILE, alpha, tp)
        spare = y
        outs[0].append(ckv_p.reshape(nb, seq, rkv))
        outs[1].append(kpe_p.reshape(nb, seq, ROPE_DIM))
        outs[2].append(u[:tp].reshape(nb, seq, d_conv)[:, seq - keep:])
        outs[3].append(ckv_s.reshape(db, ds, rkv))
        outs[4].append(kpe_s.reshape(db, ds, ROPE_DIM))
        outs[5].append(u[tp:].reshape(db, ds, d_conv)[:, ds - keep:])

    return (y_prompt.reshape(nb, seq, d), y_sample.reshape(db, ds, d)) + tuple(jnp.stack(o) for o in outs)
```

```python
import functools
import math

import jax
import jax.numpy as jnp
from jax import lax
from jax.experimental import pallas as pl
from jax.experimental.pallas import tpu as pltpu

F32 = jnp.float32
BF16 = jnp.bfloat16

CHUNK = 64
CONV_WIDTH = 31
N_HEADS = 16
QK_NOPE_DIM = 128
ROPE_DIM = 64
V_HEAD_DIM = 128
ROPE_THETA = 10000.0
N_EXPERTS = 16
N_EXPERT_GROUPS = 4
EXPERTS_PER_GROUP = N_EXPERTS // N_EXPERT_GROUPS
LN_EPS = 1e-5
RMS_EPS = 1e-6

LANES = 128
SUBLANES = 8
MOE_TILE = 256
PREFIX_CHUNK = 512
HEAD_PAD = 256
CONV_HALO = 32
CONV_OFF = CONV_HALO - (CONV_WIDTH - 1)
NEG_BIG = -1e30
MIB = 1024 * 1024


def _cparams(sem, vmem_mib):
    return pltpu.CompilerParams(dimension_semantics=sem, vmem_limit_bytes=vmem_mib * MIB)


def _dot(a, b):
    return jnp.dot(a, b, preferred_element_type=F32)


def _dot_nt(a, b):
    return lax.dot_general(a, b, (((1,), (1,)), ((), ())), preferred_element_type=F32)


def _layer_norm(y, g, b):
    mu = jnp.mean(y, axis=-1, keepdims=True)
    d = y - mu
    var = jnp.mean(d * d, axis=-1, keepdims=True)
    return d * lax.rsqrt(var + LN_EPS) * g + b


def _rms_norm(y, g):
    return y * lax.rsqrt(jnp.mean(y * y, axis=-1, keepdims=True) + RMS_EPS) * g


def _sigmoid(x):
    return 0.5 * jnp.tanh(0.5 * x) + 0.5


def _rope128(v, c, s1, s2):
    return v * c + pltpu.roll(v, 96, axis=1) * s1 + pltpu.roll(v, 32, axis=1) * s2


def _glu_kernel(x_ref, wa_ref, wg_ref, u_ref, wa_bf, wg_bf):
    @pl.when(pl.program_id(1) == 0)
    def _():
        wa_bf[...] = wa_ref[...].astype(BF16)
        wg_bf[...] = wg_ref[...].astype(BF16)

    x = x_ref[...]
    a = _dot_nt(x, wa_bf[...])
    g = _dot_nt(x, wg_bf[...])
    u_ref[...] = a * _sigmoid(g)


def _glu_call(x_bf, w_in_t, l, d_conv, tm, tn):
    t, d = x_bf.shape
    nj = d_conv // tn
    return pl.pallas_call(
        _glu_kernel,
        grid=(nj, t // tm),
        in_specs=[pl.BlockSpec((tm, d), lambda j, i: (i, 0)),
                  pl.BlockSpec((None, tn, d), lambda j, i: (l, j, 0)),
                  pl.BlockSpec((None, tn, d), lambda j, i: (l, j + nj, 0))],
        out_specs=pl.BlockSpec((tm, tn), lambda j, i: (i, j)),
        out_shape=jax.ShapeDtypeStruct((t, d_conv), F32),
        scratch_shapes=[pltpu.VMEM((tn, d), BF16), pltpu.VMEM((tn, d), BF16)],
        compiler_params=_cparams(("arbitrary", "arbitrary"), 48),
        name="glu",
    )(x_bf, w_in_t, w_in_t)


def _latent_kernel(x_ref, wq_ref, wc_ref, wk_ref, qg_ref, cg_ref, c_ref, s1_ref, s2_ref, kmask_ref,
                   qn_ref, ckvp_ref, ckvs_ref, kpep_ref, kpes_ref, kpad_ref, wq_bf, wc_bf, wk_bf, *, n_prompt_tiles):
    i = pl.program_id(0)

    @pl.when(i == 0)
    def _():
        wq_bf[...] = wq_ref[...].astype(BF16)
        wc_bf[...] = wc_ref[...].astype(BF16)
        wk_bf[...] = wk_ref[...].astype(BF16)

    x = x_ref[...]
    qn_ref[...] = _rms_norm(_dot_nt(x, wq_bf[...]), qg_ref[...]).astype(BF16)
    ckv = _rms_norm(_dot_nt(x, wc_bf[...]), cg_ref[...])
    kr = _dot_nt(x, wk_bf[...])
    k = _rope128(kr, c_ref[...], s1_ref[...], s2_ref[...])
    lane = lax.broadcasted_iota(jnp.int32, k.shape, 1)
    k = jnp.where(lane < ROPE_DIM, k, 0.0)
    kpad_ref[...] = (k + kmask_ref[...]).astype(BF16)

    @pl.when(i < n_prompt_tiles)
    def _():
        ckvp_ref[...] = ckv
        kpep_ref[...] = k[:, :ROPE_DIM]

    @pl.when(i >= n_prompt_tiles)
    def _():
        ckvs_ref[...] = ckv
        kpes_ref[...] = k[:, :ROPE_DIM]


def _latent_call(x_bf, w_in_t, q_norm_g, kv_norm_g, tabs, l, splits, rq, rkv, tm, prompt_rows):
    t, d = x_bf.shape
    assert splits[0] % rq == 0 and splits[1] % rkv == 0 and splits[2] % LANES == 0
    n_p = prompt_rows // tm
    row = lambda i: (i, 0)
    prow = lambda i: (jnp.minimum(i, n_p - 1), 0)
    srow = lambda i: (jnp.maximum(i - n_p, 0), 0)
    return pl.pallas_call(
        functools.partial(_latent_kernel, n_prompt_tiles=n_p),
        grid=(t // tm,),
        in_specs=[pl.BlockSpec((tm, d), row),
                  pl.BlockSpec((None, rq, d), lambda i: (l, splits[0] // rq, 0)),
                  pl.BlockSpec((None, rkv, d), lambda i: (l, splits[1] // rkv, 0)),
                  pl.BlockSpec((None, LANES, d), lambda i: (l, splits[2] // LANES, 0)),
                  pl.BlockSpec((None, 1, rq), lambda i: (l, 0, 0)),
                  pl.BlockSpec((None, 1, rkv), lambda i: (l, 0, 0)),
                  pl.BlockSpec((tm, LANES), row),
                  pl.BlockSpec((tm, LANES), row),
                  pl.BlockSpec((tm, LANES), row),
                  pl.BlockSpec((tm, LANES), row)],
        out_specs=[pl.BlockSpec((tm, rq), row),
                   pl.BlockSpec((tm, rkv), prow), pl.BlockSpec((tm, rkv), srow),
                   pl.BlockSpec((tm, ROPE_DIM), prow), pl.BlockSpec((tm, ROPE_DIM), srow),
                   pl.BlockSpec((tm, LANES), row)],
        out_shape=[jax.ShapeDtypeStruct((t, rq), BF16),
                   jax.ShapeDtypeStruct((prompt_rows, rkv), F32), jax.ShapeDtypeStruct((t - prompt_rows, rkv), F32),
                   jax.ShapeDtypeStruct((prompt_rows, ROPE_DIM), F32),
                   jax.ShapeDtypeStruct((t - prompt_rows, ROPE_DIM), F32),
                   jax.ShapeDtypeStruct((t, LANES), BF16)],
        scratch_shapes=[pltpu.VMEM((rq, d), BF16), pltpu.VMEM((rkv, d), BF16), pltpu.VMEM((LANES, d), BF16)],
        compiler_params=_cparams(("arbitrary",), 48),
        name="latent",
    )(x_bf, w_in_t, w_in_t, w_in_t, q_norm_g, kv_norm_g, *tabs)


def _gates_kernel(x_ref, w_hbm, o_ref, w_f32, w_bf, sem, *, l, row0, tn):
    j = pl.program_id(0)

    def block_copy(jj):
        return pltpu.make_async_copy(w_hbm.at[l, pl.ds(pl.multiple_of(row0 + jj * tn, SUBLANES), tn), :], w_f32, sem)

    @pl.when(pl.program_id(1) == 0)
    def _():
        @pl.when(j == 0)
        def _():
            block_copy(0).start()

        block_copy(j).wait()
        w_bf[...] = w_f32[...].astype(BF16)

        @pl.when(j + 1 < pl.num_programs(0))
        def _():
            block_copy(j + 1).start()

    o_ref[...] = _sigmoid(_dot_nt(x_ref[...], w_bf[...])).astype(BF16)


def _gates_call(x_bf, w_in_t, l, row0, tm, tn):
    t, d = x_bf.shape
    n = w_in_t.shape[1] - row0
    assert n % tn == 0 and row0 % SUBLANES == 0
    return pl.pallas_call(
        functools.partial(_gates_kernel, l=l, row0=row0, tn=tn),
        grid=(n // tn, t // tm),
        in_specs=[pl.BlockSpec((tm, d), lambda j, i: (i, 0)),
                  pl.BlockSpec(memory_space=pl.ANY)],
        out_specs=pl.BlockSpec((tm, tn), lambda j, i: (i, j)),
        out_shape=jax.ShapeDtypeStruct((t, n), BF16),
        scratch_shapes=[pltpu.VMEM((tn, d), F32), pltpu.VMEM((tn, d), BF16), pltpu.SemaphoreType.DMA(())],
        compiler_params=_cparams(("arbitrary", "arbitrary"), 48),
        name="gates",
    )(x_bf, w_in_t)


def _conv_tail(ext_ref, y_ref, cw_ref, cb_ref, lg_ref, lb_ref, wpw_bf, gate_ref, ga_ref, ts, rows):
    c_dim = y_ref.shape[1]
    for c in range(c_dim // LANES):
        cs = slice(c * LANES, (c + 1) * LANES)

        def rbody(r, carry, cs=cs):
            r0 = pl.multiple_of(r * rows, rows)
            win = ext_ref[pl.ds(r0, rows + CONV_HALO), cs]
            acc = jnp.zeros((rows, LANES), F32)
            for b in range(SUBLANES):
                taps = [o for o in range(CONV_OFF, CONV_OFF + CONV_WIDTH) if o % SUBLANES == b]
                n_win = rows + CONV_HALO
                shifted = win if b == 0 else pltpu.roll(win, n_win - b, axis=0)
                for o in taps:
                    acc = acc + shifted[o - b:o - b + rows, :] * cw_ref[o - CONV_OFF:o - CONV_OFF + 1, cs]
            y_ref[pl.ds(r0, rows), cs] = acc
            return carry

        lax.fori_loop(0, ts // rows, rbody, 0)

    y = _layer_norm(y_ref[...] + cb_ref[...], lg_ref[...], lb_ref[...])
    z = (y * _sigmoid(y)).astype(BF16)
    a_out = _dot(z, wpw_bf[...])
    ga_ref[...] = (gate_ref[...].astype(F32) * a_out).astype(BF16)


def _conv_prompt_kernel(u_ref, halo_ref, cw_ref, cb_ref, lg_ref, lb_ref, wpw_ref, gate_ref, ga_ref,
                        ext_ref, y_ref, wpw_bf, *, ts, rows):
    b, i = pl.program_id(0), pl.program_id(1)

    @pl.when((b == 0) & (i == 0))
    def _():
        wpw_bf[...] = wpw_ref[...].astype(BF16)

    @pl.when(i == 0)
    def _():
        ext_ref[0:CONV_HALO, :] = jnp.zeros((CONV_HALO, ext_ref.shape[1]), F32)

    @pl.when(i > 0)
    def _():
        ext_ref[0:CONV_HALO, :] = halo_ref[...]

    ext_ref[CONV_HALO:CONV_HALO + ts, :] = u_ref[...]
    _conv_tail(ext_ref, y_ref, cw_ref, cb_ref, lg_ref, lb_ref, wpw_bf, gate_ref, ga_ref, ts, rows)


def _conv_sample_kernel(u_ref, hist_ref, cw_ref, cb_ref, lg_ref, lb_ref, wpw_ref, gate_ref, ga_ref,
                        ext_ref, y_ref, wpw_bf, *, ts, rows):
    @pl.when(pl.program_id(0) == 0)
    def _():
        wpw_bf[...] = wpw_ref[...].astype(BF16)

    ext_ref[0:CONV_HALO, :] = jnp.zeros((CONV_HALO, ext_ref.shape[1]), F32)
    ext_ref[CONV_OFF:CONV_HALO, :] = hist_ref[...]
    ext_ref[CONV_HALO:CONV_HALO + ts, :] = u_ref[...]
    _conv_tail(ext_ref, y_ref, cw_ref, cb_ref, lg_ref, lb_ref, wpw_bf, gate_ref, ga_ref, ts, rows)


def _conv_weight_specs(l, c_dim, d, nargs):
    z = (lambda *a: (l, 0, 0))
    return [pl.BlockSpec((None, CONV_WIDTH, c_dim), z),
            pl.BlockSpec((None, 1, c_dim), z),
            pl.BlockSpec((None, 1, c_dim), z),
            pl.BlockSpec((None, 1, c_dim), z),
            pl.BlockSpec((None, c_dim, d), z)]


def _conv_prompt_call(u, gates, conv_w, conv_b, ln_g, ln_b, w_pw, l, nb, seq, ts, rows):
    c_dim = u.shape[1]
    d = w_pw.shape[-1]
    ns = seq // ts
    hb = ts // CONV_HALO
    tile = lambda b, i: (b * ns + i, 0)
    return pl.pallas_call(
        functools.partial(_conv_prompt_kernel, ts=ts, rows=rows),
        grid=(nb, ns),
        in_specs=[pl.BlockSpec((ts, c_dim), tile),
                  pl.BlockSpec((CONV_HALO, c_dim), lambda b, i: (jnp.maximum((b * ns + i) * hb - 1, 0), 0))]
        + _conv_weight_specs(l, c_dim, d, 2)
        + [pl.BlockSpec((ts, d), tile)],
        out_specs=pl.BlockSpec((ts, d), tile),
        out_shape=jax.ShapeDtypeStruct((nb * seq, d), BF16),
        scratch_shapes=[pltpu.VMEM((CONV_HALO + ts, c_dim), F32), pltpu.VMEM((ts, c_dim), F32),
                        pltpu.VMEM((c_dim, d), BF16)],
        compiler_params=_cparams(("arbitrary", "arbitrary"), 56),
        name="conv_prompt",
    )(u, u, conv_w, conv_b, ln_g, ln_b, w_pw, gates)


def _conv_sample_call(u, gates, state_conv, conv_w, conv_b, ln_g, ln_b, w_pw, l, row0, nb, ts, rows):
    c_dim = u.shape[1]
    d = w_pw.shape[-1]
    t0 = row0 // ts
    tile = lambda b: (t0 + b, 0)
    return pl.pallas_call(
        functools.partial(_conv_sample_kernel, ts=ts, rows=rows),
        grid=(nb,),
        in_specs=[pl.BlockSpec((ts, c_dim), tile),
                  pl.BlockSpec((None, None, CONV_WIDTH - 1, c_dim), lambda b: (l, b, 0, 0))]
        + _conv_weight_specs(l, c_dim, d, 1)
        + [pl.BlockSpec((ts, d), tile)],
        out_specs=pl.BlockSpec((ts, d), lambda b: (b, 0)),
        out_shape=jax.ShapeDtypeStruct((nb * ts, d), BF16),
        scratch_shapes=[pltpu.VMEM((CONV_HALO + ts, c_dim), F32), pltpu.VMEM((ts, c_dim), F32),
                        pltpu.VMEM((c_dim, d), BF16)],
        compiler_params=_cparams(("arbitrary",), 56),
        name="conv_sample",
    )(u, state_conv, conv_w, conv_b, ln_g, ln_b, w_pw, gates)


def _q_kernel(qn_ref, w_ref, c_ref, s1_ref, s2_ref, qmask_ref, q_ref, *, scale):
    qn = qn_ref[...]
    c, s1, s2, qmask = c_ref[...], s1_ref[...], s2_ref[...], qmask_ref[...]
    for h in range(N_HEADS):
        qh = _dot(qn, w_ref[:, h * HEAD_PAD:(h + 1) * HEAD_PAD])
        q_ref[:, h * HEAD_PAD:h * HEAD_PAD + LANES] = (qh[:, :LANES] * scale).astype(BF16)
        q_ref[:, h * HEAD_PAD + LANES:(h + 1) * HEAD_PAD] = (
            _rope128(qh[:, LANES:], c, s1, s2) * scale + qmask).astype(BF16)


def _q_call(qn, w_q, tabs, l, tm, scale):
    t, r = qn.shape
    n = w_q.shape[-1]
    row = lambda i: (i, 0)
    return pl.pallas_call(
        functools.partial(_q_kernel, scale=scale),
        grid=(t // tm,),
        in_specs=[pl.BlockSpec((tm, r), row),
                  pl.BlockSpec((None, r, n), lambda i: (l, 0, 0)),
                  pl.BlockSpec((tm, LANES), row), pl.BlockSpec((tm, LANES), row), pl.BlockSpec((tm, LANES), row),
                  pl.BlockSpec((tm, LANES), row)],
        out_specs=pl.BlockSpec((tm, n), row),
        out_shape=jax.ShapeDtypeStruct((t, n), BF16),
        compiler_params=_cparams(("arbitrary",), 48),
        name="q_proj",
    )(qn, w_q, *tabs)


def _kv_kernel(lat_ref, kpad_ref, w_ref, k_ref, vt_ref, w_bf):
    @pl.when(pl.program_id(0) == 0)
    def _():
        w_bf[...] = w_ref[...].astype(BF16)

    lat = lat_ref[...].astype(BF16)
    kpad = kpad_ref[...]
    for h in range(N_HEADS):
        kv = _dot(lat, w_bf[:, h * HEAD_PAD:(h + 1) * HEAD_PAD])
        k_ref[:, h * HEAD_PAD:h * HEAD_PAD + LANES] = kv[:, :LANES].astype(BF16)
        k_ref[:, h * HEAD_PAD + LANES:(h + 1) * HEAD_PAD] = kpad
        vt_ref[h * V_HEAD_DIM:(h + 1) * V_HEAD_DIM, :] = kv[:, LANES:].T.astype(BF16)


def _kv_call(ckv, kpad, w_kv_b, l, rows, tm):
    r = ckv.shape[1]
    n = w_kv_b.shape[-1]
    row = lambda i: (i, 0)
    return pl.pallas_call(
        _kv_kernel,
        grid=(rows // tm,),
        in_specs=[pl.BlockSpec((tm, r), row),
                  pl.BlockSpec((tm, LANES), row),
                  pl.BlockSpec((None, r, n), lambda i: (l, 0, 0))],
        out_specs=[pl.BlockSpec((tm, N_HEADS * HEAD_PAD), row),
                   pl.BlockSpec((N_HEADS * V_HEAD_DIM, tm), lambda i: (0, i))],
        out_shape=[jax.ShapeDtypeStruct((rows, N_HEADS * HEAD_PAD), BF16),
                   jax.ShapeDtypeStruct((N_HEADS * V_HEAD_DIM, rows), BF16)],
        scratch_shapes=[pltpu.VMEM((r, n), BF16)],
        compiler_params=_cparams(("arbitrary",), 48),
        name="kv_proj",
    )(ckv, kpad, w_kv_b)


def _attn_prompt_kernel(qi_ref, kj_ref, q_ref, k_ref, vt_ref, ga_ref, gb_ref, o_ref, *scratch,
                        n_steps, tq, tk, unroll, n_buf, heads):
    per_head = 2 * n_buf + 3

    def head_refs(h):
        sc = scratch[h * per_head:(h + 1) * per_head]
        return [(sc[2 * j], sc[2 * j + 1]) for j in range(n_buf)], sc[2 * n_buf], sc[2 * n_buf + 1], sc[2 * n_buf + 2]

    def stage_scores(h, t, s_ref, mx_ref):
        q = q_ref[pl.ds(pl.multiple_of(qi_ref[t] * tq, tq), tq), h * HEAD_PAD:(h + 1) * HEAD_PAD]
        k = k_ref[pl.ds(pl.multiple_of(kj_ref[t] * tk, tk), tk), h * HEAD_PAD:(h + 1) * HEAD_PAD]
        s = _dot_nt(k, q)
        s_ref[...] = s
        mx_ref[...] = jnp.max(s, axis=0, keepdims=True)

    def step(h, t, j):
        bufs, acc_ref, m_ref, l_ref = head_refs(h)
        s_cur, mx_cur = bufs[j]
        stage_scores(h, t + n_buf - 1, *bufs[(j - 1) % n_buf])
        cols = pl.ds(pl.multiple_of(qi_ref[t] * tq, tq), tq)
        m_old = m_ref[:, cols]
        m_new = jnp.maximum(m_old, mx_cur[...])
        alpha = jnp.exp2(m_old - m_new)
        p = jnp.exp2(s_cur[...] - m_new)
        l_ref[:, cols] = alpha * l_ref[:, cols] + jnp.sum(p, axis=0, keepdims=True)
        keys = pl.ds(pl.multiple_of(kj_ref[t] * tk, tk), tk)
        pv = _dot(vt_ref[h * V_HEAD_DIM:(h + 1) * V_HEAD_DIM, keys], p.astype(BF16))
        acc_ref[:, cols] = alpha * acc_ref[:, cols] + pv
        m_ref[:, cols] = m_new

    for h in range(heads):
        bufs, acc_ref, m_ref, l_ref = head_refs(h)
        m_ref[...] = jnp.full(m_ref.shape, NEG_BIG, F32)
        l_ref[...] = jnp.zeros(l_ref.shape, F32)
        acc_ref[...] = jnp.zeros(acc_ref.shape, F32)
        for j in range(n_buf - 1):
            stage_scores(h, j, *bufs[j])

    def body(u, carry):
        for j in range(unroll):
            for h in range(heads):
                step(h, unroll * u + j, j % n_buf)
        return carry

    lax.fori_loop(0, n_steps // unroll, body, 0)

    def finish(qi, carry):
        rows = pl.ds(pl.multiple_of(qi * tq, tq), tq)
        for h in range(heads):
            _, acc_ref, _, l_ref = head_refs(h)
            cols = slice(h * V_HEAD_DIM, (h + 1) * V_HEAD_DIM)
            o = (acc_ref[:, rows] / l_ref[:, rows]).T
            o_ref[rows, cols] = (ga_ref[rows, cols].astype(F32) + gb_ref[rows, cols].astype(F32) * o).astype(BF16)
        return carry

    lax.fori_loop(0, o_ref.shape[0] // tq, finish, 0)


def _attn_prompt_call(q, k, vt, ga, gates, nb, seq, tq, tk):
    d = N_HEADS * V_HEAD_DIM
    nq = seq // tq
    n_buf, unroll, heads = 2, 4, 1
    assert nq >= 2 and seq // CHUNK <= LANES - ROPE_DIM and seq % tk == 0
    tiles = [(qi, kj) for qi in range(nq) for kj in range(-(-(qi + 1) * tq // tk))]
    masked_tile = (0, -(-tq // tk))
    assert (masked_tile[1] + 1) * tk <= seq
    tiles += [masked_tile] * (-len(tiles) % unroll)
    n_steps = len(tiles)
    tiles += [(0, 0)] * (n_buf - 1)
    qi_tab = jnp.asarray([t[0] for t in tiles], jnp.int32)
    kj_tab = jnp.asarray([t[1] for t in tiles], jnp.int32)
    bh = lambda b, h, qt, kt: (b, h)
    return pl.pallas_call(
        functools.partial(_attn_prompt_kernel, n_steps=n_steps, tq=tq, tk=tk, unroll=unroll, n_buf=n_buf,
                          heads=heads),
        grid_spec=pltpu.PrefetchScalarGridSpec(
            num_scalar_prefetch=2, grid=(nb, N_HEADS // heads),
            in_specs=[pl.BlockSpec((seq, heads * HEAD_PAD), bh),
                      pl.BlockSpec((seq, heads * HEAD_PAD), bh),
                      pl.BlockSpec((heads * V_HEAD_DIM, seq), lambda b, h, qt, kt: (h, b)),
                      pl.BlockSpec((seq, heads * V_HEAD_DIM), bh),
                      pl.BlockSpec((seq, heads * V_HEAD_DIM), lambda b, h, qt, kt: (b, N_HEADS // heads + h))],
            out_specs=pl.BlockSpec((seq, heads * V_HEAD_DIM), bh),
            scratch_shapes=([pltpu.VMEM((tk, tq), F32), pltpu.VMEM((1, tq), F32)] * n_buf
                            + [pltpu.VMEM((V_HEAD_DIM, seq), F32), pltpu.VMEM((1, seq), F32),
                               pltpu.VMEM((1, seq), F32)]) * heads),
        out_shape=jax.ShapeDtypeStruct((nb * seq, d), BF16),
        compiler_params=_cparams(("arbitrary", "arbitrary"), 56),
        name="attn_prompt",
    )(qi_tab, kj_tab, q, k, vt, ga, gates)


def _attn_sample_kernel(q_ref, latp_ref, kpep_ref, latn_ref, kpen_ref, w_ref, ga_ref, gb_ref, o_ref,
                        w_bf, qlat_ref, qpe_ref, kpp_ref, kpn_ref, olat_ref, *, ds):
    @pl.when(pl.program_id(0) == 0)
    def _():
        w_bf[...] = w_ref[...].astype(BF16)
        kpp_ref[...] = jnp.zeros(kpp_ref.shape, BF16)
        kpn_ref[...] = jnp.zeros(kpn_ref.shape, BF16)

    kpp_ref[:ROPE_DIM, :] = kpep_ref[...].astype(BF16)
    kpn_ref[:, :ROPE_DIM] = kpen_ref[...].astype(BF16)
    for h in range(N_HEADS):
        rows = slice(h * ds, (h + 1) * ds)
        qn = q_ref[:, h * HEAD_PAD:h * HEAD_PAD + LANES]
        qlat_ref[rows, :] = _dot_nt(qn, w_bf[:, h * HEAD_PAD:h * HEAD_PAD + LANES]).astype(BF16)
        qpe_ref[rows, :] = q_ref[:, h * HEAD_PAD + LANES:(h + 1) * HEAD_PAD]

    lat_p = latp_ref[...].astype(BF16)
    lat_n = latn_ref[...].astype(BF16)
    qlat, qpe = qlat_ref[...], qpe_ref[...]
    s_p = _dot_nt(qlat, lat_p) + _dot(qpe, kpp_ref[...])
    s_n = _dot_nt(qlat, lat_n) + _dot_nt(qpe, kpn_ref[...])
    m = jnp.maximum(jnp.max(s_p, axis=1, keepdims=True), jnp.max(s_n, axis=1, keepdims=True))
    p_p = jnp.exp2(s_p - m)
    p_n = jnp.exp2(s_n - m)
    den = jnp.sum(p_p, axis=1, keepdims=True) + jnp.sum(p_n, axis=1, keepdims=True)
    o_lat = _dot(p_p.astype(BF16), lat_p) + _dot(p_n.astype(BF16), lat_n)
    olat_ref[...] = (o_lat / den).astype(BF16)
    for h in range(N_HEADS):
        cols = slice(h * V_HEAD_DIM, (h + 1) * V_HEAD_DIM)
        o = _dot(olat_ref[h * ds:(h + 1) * ds, :], w_bf[:, h * HEAD_PAD + LANES:(h + 1) * HEAD_PAD])
        o_ref[:, cols] = (ga_ref[:, cols].astype(F32) + gb_ref[:, cols].astype(F32) * o).astype(BF16)


def _attn_sample_call(q, cache_lat, cache_pe, ckv_s, kpe_s, w_kv_b, ga_s, gates, l, row0, nb, ds):
    past, r = cache_lat.shape[2], cache_lat.shape[3]
    d = N_HEADS * V_HEAD_DIM
    t0 = row0 // ds
    tile = lambda b: (t0 + b, 0)
    return pl.pallas_call(
        functools.partial(_attn_sample_kernel, ds=ds),
        grid=(nb,),
        in_specs=[pl.BlockSpec((ds, N_HEADS * HEAD_PAD), tile),
                  pl.BlockSpec((None, None, past, r), lambda b: (l, b, 0, 0)),
                  pl.BlockSpec((None, None, ROPE_DIM, past), lambda b: (l, b, 0, 0)),
                  pl.BlockSpec((ds, r), lambda b: (b, 0)),
                  pl.BlockSpec((ds, ROPE_DIM), lambda b: (b, 0)),
                  pl.BlockSpec((None, r, N_HEADS * HEAD_PAD), lambda b: (l, 0, 0)),
                  pl.BlockSpec((ds, d), lambda b: (b, 0)),
                  pl.BlockSpec((ds, d), lambda b: (t0 + b, 1))],
        out_specs=pl.BlockSpec((ds, d), lambda b: (b, 0)),
        out_shape=jax.ShapeDtypeStruct((nb * ds, d), BF16),
        scratch_shapes=[pltpu.VMEM((r, N_HEADS * HEAD_PAD), BF16),
                        pltpu.VMEM((N_HEADS * ds, r), BF16),
                        pltpu.VMEM((N_HEADS * ds, LANES), BF16),
                        pltpu.VMEM((LANES, past), BF16),
                        pltpu.VMEM((ds, LANES), BF16),
                        pltpu.VMEM((N_HEADS * ds, r), BF16)],
        compiler_params=_cparams(("arbitrary",), 56),
        name="attn_sample",
    )(q, cache_lat, cache_pe, ckv_s, kpe_s, w_kv_b, ga_s, gates)


def _route(scores, bias):
    sel = [s + b for s, b in zip(scores, bias)]
    n = EXPERTS_PER_GROUP
    grp = []
    for g in range(N_EXPERT_GROUPS):
        v = sel[g * n:(g + 1) * n]
        best = None
        for i in range(n):
            for j in range(i + 1, n):
                pair = v[i] + v[j]
                best = pair if best is None else jnp.maximum(best, pair)
        grp.append(best)
    g_idx = jnp.zeros_like(grp[0], dtype=jnp.int32)
    g_best = grp[0]
    for g in range(1, N_EXPERT_GROUPS):
        better = grp[g] > g_best
        g_idx = jnp.where(better, g, g_idx)
        g_best = jnp.where(better, grp[g], g_best)

    def pick(rows_by_group):
        out = rows_by_group[0]
        for g in range(1, N_EXPERT_GROUPS):
            out = jnp.where(g_idx == g, rows_by_group[g], out)
        return out

    in_sel = [pick([sel[g * n + j] for g in range(N_EXPERT_GROUPS)]) for j in range(n)]
    in_sc = [pick([scores[g * n + j] for g in range(N_EXPERT_GROUPS)]) for j in range(n)]

    def argmax_first(vals, excluded):
        idx = None
        best = None
        for j in range(n):
            v = vals[j] if excluded is None else jnp.where(excluded == j, -jnp.inf, vals[j])
            if best is None:
                best, idx = v, jnp.zeros_like(g_idx)
            else:
                better = v > best
                idx = jnp.where(better, j, idx)
                best = jnp.where(better, v, best)
        return idx

    l0 = argmax_first(in_sel, None)
    l1 = argmax_first(in_sel, l0)

    def take(vals, idx):
        out = vals[0]
        for j in range(1, n):
            out = jnp.where(idx == j, vals[j], out)
        return out

    w0, w1 = take(in_sc, l0), take(in_sc, l1)
    tot = w0 + w1
    return g_idx * n + l0, g_idx * n + l1, w0 / tot, w1 / tot


def _wo_kernel(mixp_ref, mixs_ref, x_ref, w_bf, g_ref, b_ref, wrh_ref, wrl_ref, br_ref, x1_ref, gate_ref, eidx_ref,
               *, alpha, n_prompt_tiles):
    mix = jnp.where(pl.program_id(0) < n_prompt_tiles, mixp_ref[...], mixs_ref[...])
    y = alpha * x_ref[...] + _dot(mix, w_bf[...])
    x1 = _layer_norm(y, g_ref[...], b_ref[...])
    x1_ref[...] = x1
    x_hi = x1.astype(BF16)
    x_lo = (x1 - x_hi.astype(F32)).astype(BF16)
    logits = _dot_nt(wrh_ref[...], x_hi) + (_dot_nt(wrh_ref[...], x_lo) + _dot_nt(wrl_ref[...], x_hi))
    sc = jax.nn.sigmoid(logits)
    br = br_ref[...]
    e0, e1, g0, g1 = _route([sc[e:e + 1, :] for e in range(N_EXPERTS)],
                            [br[e:e + 1, :] for e in range(N_EXPERTS)])
    rows = lax.broadcasted_iota(jnp.int32, (LANES, sc.shape[1]), 0)
    gate_ref[...] = jnp.where(rows == 0, g0, jnp.where(rows == 1, g1, 0.0)).T
    rows8 = lax.broadcasted_iota(jnp.int32, eidx_ref.shape, 0)
    eidx_ref[...] = jnp.where(rows8 == 0, e0, jnp.where(rows8 == 1, e1, 0))


def _wo_call(mixed_p, mixed_s, x, w_o, ln_g, ln_b, wr_hi, wr_lo, br, l, tm, alpha):
    t, d = x.shape
    n_p = mixed_p.shape[0] // tm
    row = lambda i: (i, 0)
    vec = pl.BlockSpec((None, 1, d), lambda i: (l, 0, 0))
    return pl.pallas_call(
        functools.partial(_wo_kernel, alpha=alpha, n_prompt_tiles=n_p),
        grid=(t // tm,),
        in_specs=[pl.BlockSpec((tm, d), lambda i: (jnp.minimum(i, n_p - 1), 0)),
                  pl.BlockSpec((tm, d), lambda i: (jnp.maximum(i - n_p, 0), 0)),
                  pl.BlockSpec((tm, d), row),
                  pl.BlockSpec((None, d, d), lambda i: (l, 0, 0), pipeline_mode=pl.Buffered(1)),
                  vec, vec,
                  pl.BlockSpec((N_EXPERTS, d), lambda i: (0, 0)),
                  pl.BlockSpec((N_EXPERTS, d), lambda i: (0, 0)),
                  pl.BlockSpec((N_EXPERTS, 1), lambda i: (0, 0))],
        out_specs=[pl.BlockSpec((tm, d), row), pl.BlockSpec((tm, LANES), row),
                   pl.BlockSpec((SUBLANES, tm), lambda i: (0, i))],
        out_shape=[jax.ShapeDtypeStruct((t, d), F32), jax.ShapeDtypeStruct((t, LANES), F32),
                   jax.ShapeDtypeStruct((SUBLANES, t), jnp.int32)],
        compiler_params=_cparams(("arbitrary",), 56),
        name="wo_ln_router",
    )(mixed_p, mixed_s, x, w_o, ln_g, ln_b, wr_hi, wr_lo, br)


def _dispatch_kernel(eidx_ref, pos_ref, meta_ref):
    t = eidx_ref.shape[1]
    e0, e1 = eidx_ref[0:1, :], eidx_ref[1:2, :]
    rows = lax.broadcasted_iota(jnp.int32, (N_EXPERTS, t), 0)
    hit0, hit1 = rows == e0, rows == e1
    oh = jnp.where(hit0 | hit1, 1.0, 0.0)
    cnt = jnp.sum(oh, axis=1, keepdims=True)
    padded = jnp.floor((cnt + (MOE_TILE - 1)) * (1.0 / MOE_TILE)) * MOE_TILE
    erow = lax.broadcasted_iota(jnp.int32, (N_EXPERTS, 1), 0)
    off = jnp.zeros((N_EXPERTS, 1), F32)
    run = jnp.zeros((1, 1), F32)
    for e in range(N_EXPERTS):
        off = jnp.where(erow == e, run, off)
        run = run + padded[e:e + 1, :]
    c = PREFIX_CHUNK
    before = jnp.where(lax.broadcasted_iota(jnp.int32, (c, c), 0) < lax.broadcasted_iota(jnp.int32, (c, c), 1),
                       1.0, 0.0).astype(BF16)
    pos_ref[...] = jnp.zeros(pos_ref.shape, jnp.int32)
    carry = off
    for j in range(t // c):
        cs = slice(j * c, (j + 1) * c)
        ohc = oh[:, cs]
        slot = _dot(ohc.astype(BF16), before) + carry
        pos_ref[0:1, cs] = jnp.sum(jnp.where(hit0[:, cs], slot, 0.0), axis=0, keepdims=True).astype(jnp.int32)
        pos_ref[1:2, cs] = jnp.sum(jnp.where(hit1[:, cs], slot, 0.0), axis=0, keepdims=True).astype(jnp.int32)
        carry = carry + jnp.sum(ohc, axis=1, keepdims=True)
    start = lax.broadcasted_iota(jnp.int32, (N_EXPERTS, LANES), 1).astype(F32) * MOE_TILE
    te = jnp.minimum(jnp.sum(jnp.where(off + padded <= start, 1.0, 0.0), axis=0, keepdims=True), N_EXPERTS - 1.0)
    mine = lax.broadcasted_iota(jnp.int32, (N_EXPERTS, LANES), 0).astype(F32) == te
    end_valid = jnp.sum(jnp.where(mine, off + cnt, 0.0), axis=0, keepdims=True)
    nvalid = jnp.clip(end_valid - start[0:1, :], 0.0, MOE_TILE)
    ordinal = jnp.zeros((N_EXPERTS, 1), F32)
    seen = jnp.zeros((1, 1), F32)
    for e in range(N_EXPERTS):
        ordinal = jnp.where(erow == e, seen, ordinal)
        seen = seen + jnp.where(cnt[e:e + 1, :] > 0.0, 1.0, 0.0)
    nxt = erow.astype(F32)
    later = jnp.full((1, 1), -1.0, F32)
    for e in reversed(range(N_EXPERTS)):
        nxt = jnp.where((erow == e) & (later >= 0.0), later, nxt)
        later = jnp.where(cnt[e:e + 1, :] > 0.0, float(e), later)
    parity = jnp.sum(jnp.where(mine, ordinal - 2.0 * jnp.floor(ordinal * 0.5), 0.0), axis=0, keepdims=True)
    nxt_tile = jnp.sum(jnp.where(mine, nxt, 0.0), axis=0, keepdims=True)
    r8 = lax.broadcasted_iota(jnp.int32, meta_ref.shape, 0)
    meta = jnp.zeros(meta_ref.shape, F32)
    for r, v in enumerate([te, nvalid, run * (1.0 / MOE_TILE), parity, nxt_tile]):
        meta = jnp.where(r8 == r, v, meta)
    meta_ref[...] = meta.astype(jnp.int32)


def _dispatch_call(eidx):
    t = eidx.shape[1]
    assert t % PREFIX_CHUNK == 0
    return pl.pallas_call(
        _dispatch_kernel,
        out_shape=[jax.ShapeDtypeStruct((SUBLANES, t), jnp.int32), jax.ShapeDtypeStruct((SUBLANES, LANES), jnp.int32)],
        compiler_params=pltpu.CompilerParams(vmem_limit_bytes=32 * MIB),
        name="moe_dispatch",
    )(eidx)


def _row_copy(src, i, dst, j, sem):
    return pltpu.make_async_copy(src.at[pl.ds(i, 1), :], dst.at[pl.ds(j, 1), :], sem)


def _scatter_kernel(p0_ref, p1_ref, x_ref, init_hbm, xs_hbm, sem, *, rows):
    del init_hbm
    base = pl.program_id(0) * rows

    def body(r, carry):
        t = base + r
        _row_copy(x_ref, r, xs_hbm, p0_ref[t], sem).start()
        _row_copy(x_ref, r, xs_hbm, p1_ref[t], sem).start()
        return carry

    lax.fori_loop(0, rows, body, 0, unroll=8)
    for _ in range(2):
        pltpu.make_async_copy(x_ref, xs_hbm.at[pl.ds(0, rows), :], sem).wait()


def _scatter_call(pos0, pos1, x1, init, rows):
    t, d = x1.shape
    return pl.pallas_call(
        functools.partial(_scatter_kernel, rows=rows),
        grid_spec=pltpu.PrefetchScalarGridSpec(
            num_scalar_prefetch=2, grid=(t // rows,),
            in_specs=[pl.BlockSpec((rows, d), lambda i, p0, p1: (i, 0)), pl.BlockSpec(memory_space=pl.ANY)],
            out_specs=pl.BlockSpec(memory_space=pl.ANY),
            scratch_shapes=[pltpu.SemaphoreType.DMA(())]),
        out_shape=jax.ShapeDtypeStruct(init.shape, init.dtype),
        input_output_aliases={3: 0},
        compiler_params=pltpu.CompilerParams(dimension_semantics=("arbitrary",), vmem_limit_bytes=32 * MIB,
                                             disable_bounds_checks=True),
        name="moe_scatter",
    )(pos0, pos1, x1, init)


def _experts_kernel(te_ref, nv_ref, nu_ref, slot_ref, nxt_ref, xs_ref, wg_hbm, wu_hbm, wd_hbm, y_ref,
                    wg_f, wu_f, wd_f, wg_bf, wu_bf, wd_bf, sems, *, l):
    i = pl.program_id(0)
    nv = nv_ref[i]
    e = te_ref[i]
    s = slot_ref[i]
    fresh = (i == 0) | (e != te_ref[jnp.maximum(i - 1, 0)])

    def weight_copies(expert, slot):
        return [pltpu.make_async_copy(w.at[l, expert], buf.at[slot], sems.at[slot])
                for w, buf in ((wg_hbm, wg_f), (wu_hbm, wu_f), (wd_hbm, wd_f))]

    @pl.when(fresh & (nv > 0))
    def _():
        @pl.when(i == 0)
        def _():
            for cp in weight_copies(e, s):
                cp.start()

        for cp in weight_copies(e, s):
            cp.wait()
        wg_bf[...] = wg_f[s].astype(BF16)
        wu_bf[...] = wu_f[s].astype(BF16)
        wd_bf[...] = wd_f[s].astype(BF16)

        @pl.when(nxt_ref[i] != e)
        def _():
            for cp in weight_copies(nxt_ref[i], 1 - s):
                cp.start()

    @pl.when(nv > 0)
    def _():
        row = lax.broadcasted_iota(jnp.int32, xs_ref.shape, 0)
        x = jnp.where(row < nv, xs_ref[...], 0.0).astype(BF16)
        hg = _dot(x, wg_bf[...])
        hu = _dot(x, wu_bf[...])
        h = hg * _sigmoid(hg) * hu
        y_ref[...] = _dot(h.astype(BF16), wd_bf[...])

    @pl.when(nv == 0)
    def _():
        y_ref[...] = jnp.zeros(y_ref.shape, F32)


def _experts_call(meta, xs, w_gate, w_up, w_down, l):
    n_slots, d = xs.shape
    f = w_gate.shape[-1]
    n_tiles = n_slots // MOE_TILE
    any_spec = pl.BlockSpec(memory_space=pl.ANY)
    return pl.pallas_call(
        functools.partial(_experts_kernel, l=l),
        grid_spec=pltpu.PrefetchScalarGridSpec(
            num_scalar_prefetch=5, grid=(n_tiles,),
            in_specs=[pl.BlockSpec((MOE_TILE, d), lambda i, te, nv, nu, sl, nx: (jnp.minimum(i, nu[0] - 1), 0)),
                      any_spec, any_spec, any_spec],
            out_specs=pl.BlockSpec((MOE_TILE, d), lambda i, te, nv, nu, sl, nx: (i, 0)),
            scratch_shapes=[pltpu.VMEM((2, d, f), F32), pltpu.VMEM((2, d, f), F32), pltpu.VMEM((2, f, d), F32),
                            pltpu.VMEM((d, f), BF16), pltpu.VMEM((d, f), BF16), pltpu.VMEM((f, d), BF16),
                            pltpu.SemaphoreType.DMA((2,))]),
        out_shape=jax.ShapeDtypeStruct((n_slots, d), F32),
        compiler_params=_cparams(("arbitrary",), 56),
        name="moe_experts",
    )(*meta, xs, w_gate, w_up, w_down)


def _combine_kernel(p0_ref, p1_ref, y_hbm, x1_ref, gate_ref, g_ref, b_ref, out_a, out_b, buf, sems,
                    *, tm, alpha, n_prompt_tiles):
    i = pl.program_id(0)
    slot = i % 2

    def issue(tile, s):
        base = tile * tm

        def body(r, carry):
            t = base + r
            _row_copy(y_hbm, p0_ref[t], buf.at[s, 0], r, sems.at[s]).start()
            _row_copy(y_hbm, p1_ref[t], buf.at[s, 1], r, sems.at[s]).start()
            return carry

        lax.fori_loop(0, tm, body, 0, unroll=8)

    @pl.when(i == 0)
    def _():
        issue(0, 0)

    @pl.when(i + 1 < pl.num_programs(0))
    def _():
        issue(i + 1, 1 - slot)

    for k in range(2):
        pltpu.make_async_copy(y_hbm.at[pl.ds(0, tm), :], buf.at[slot, k], sems.at[slot]).wait()
    gate = gate_ref[...]
    moe = gate[:, 0:1] * buf[slot, 0] + gate[:, 1:2] * buf[slot, 1]
    x2 = _layer_norm(alpha * x1_ref[...] + moe, g_ref[...], b_ref[...])
    if n_prompt_tiles is None:
        out_a[...] = x2
        out_b[...] = x2.astype(BF16)
    else:
        @pl.when(i < n_prompt_tiles)
        def _():
            out_a[...] = x2

        @pl.when(i >= n_prompt_tiles)
        def _():
            out_b[...] = x2


def _combine_call(pos0, pos1, y, x1, gate, ln_g, ln_b, l, tm, alpha, prompt_rows=None):
    t, d = x1.shape
    row = lambda i, p0, p1: (i, 0)
    vec = pl.BlockSpec((None, 1, d), lambda i, p0, p1: (l, 0, 0))
    if prompt_rows is None:
        n_p = None
        out_specs = [pl.BlockSpec((tm, d), row), pl.BlockSpec((tm, d), row)]
        out_shape = [jax.ShapeDtypeStruct((t, d), F32), jax.ShapeDtypeStruct((t, d), BF16)]
    else:
        n_p = prompt_rows // tm
        out_specs = [pl.BlockSpec((tm, d), lambda i, p0, p1: (jnp.minimum(i, n_p - 1), 0)),
                     pl.BlockSpec((tm, d), lambda i, p0, p1: (jnp.maximum(i - n_p, 0), 0))]
        out_shape = [jax.ShapeDtypeStruct((prompt_rows, d), F32), jax.ShapeDtypeStruct((t - prompt_rows, d), F32)]
    return pl.pallas_call(
        functools.partial(_combine_kernel, tm=tm, alpha=alpha, n_prompt_tiles=n_p),
        grid_spec=pltpu.PrefetchScalarGridSpec(
            num_scalar_prefetch=2, grid=(t // tm,),
            in_specs=[pl.BlockSpec(memory_space=pl.ANY),
                      pl.BlockSpec((tm, d), row), pl.BlockSpec((tm, LANES), row), vec, vec],
            out_specs=out_specs,
            scratch_shapes=[pltpu.VMEM((2, 2, tm, d), F32), pltpu.SemaphoreType.DMA((2,))]),
        out_shape=out_shape,
        compiler_params=pltpu.CompilerParams(dimension_semantics=("arbitrary",), vmem_limit_bytes=48 * MIB,
                                             disable_bounds_checks=True),
        name="moe_combine_ln",
    )(pos0, pos1, y, x1, gate, ln_g, ln_b)


def _rope_tables(pos):
    half = ROPE_DIM // 2
    inv = jnp.float32(ROPE_THETA) ** (-jnp.arange(half, dtype=F32) / half)
    ang = pos.astype(F32)[:, None] * inv[None, :]
    cos, sin = jnp.cos(ang), jnp.sin(ang)
    z = jnp.zeros_like(cos)
    return (jnp.concatenate([cos, cos, z, z], 1),
            jnp.concatenate([-sin, z, z, z], 1),
            jnp.concatenate([z, sin, z, z], 1))


def kernel(x_prompt, x_sample, cache_kv_latent, cache_k_rope, state_conv, w_in, q_norm_g, w_q_b, kv_norm_g, w_kv_b,
           conv_w, conv_b, conv_ln_g, conv_ln_b, w_conv_pw, w_o, ln1_g, ln1_b, w_gate, w_up, w_down, ln2_g, ln2_b,
           w_router, b_router):
    nb, seq, d = x_prompt.shape
    db, ds, _ = x_sample.shape
    depth = w_in.shape[0]
    past = cache_kv_latent.shape[2]
    d_conv = conv_w.shape[-1]
    rq, rkv = q_norm_g.shape[-1], kv_norm_g.shape[-1]
    tp, tsmp = nb * seq, db * ds
    t_all = tp + tsmp
    splits = (2 * d_conv, 2 * d_conv + rq, 2 * d_conv + rq + rkv, 2 * d_conv + rq + rkv + ROPE_DIM)
    alpha = (2.0 * depth) ** 0.25
    scale = math.log2(math.e) / math.sqrt(QK_NOPE_DIM + ROPE_DIM)
    tm = 512
    tq = 512
    n_tiles = 2 * t_all // MOE_TILE + N_EXPERTS
    assert t_all % MOE_TILE == 0 and n_tiles <= LANES
    assert d == N_HEADS * V_HEAD_DIM and w_kv_b.shape[-1] == N_HEADS * HEAD_PAD
    assert tp % tm == 0 and tsmp % tm == 0 and seq % tq == 0 and tq % CHUNK == 0
    assert past % CHUNK == 0 and ds <= CHUNK and ds >= CONV_WIDTH - 1

    pos = jnp.concatenate([jnp.tile(jnp.arange(seq, dtype=jnp.int32), nb),
                           jnp.tile(past + jnp.arange(ds, dtype=jnp.int32), db)])
    tabs = _rope_tables(pos)
    lane_chunk = jnp.arange(LANES, dtype=jnp.int32)[None, :] - ROPE_DIM
    row_chunk = jnp.where(jnp.arange(t_all) < tp, pos // CHUNK, LANES)[:, None]
    qmask = jnp.where((lane_chunk >= 0) & (lane_chunk > row_chunk) & (row_chunk < LANES), NEG_BIG, 0.0).astype(F32)
    kmask = jnp.where(lane_chunk == row_chunk, 1.0, 0.0).astype(F32)
    tabs_k, tabs_q = tabs + (kmask,), tabs + (qmask,)

    w_in_t = jnp.swapaxes(w_in, 1, 2)
    cache_pe_t = jnp.swapaxes(cache_k_rope, 2, 3)
    wq = w_q_b.reshape(depth, rq, N_HEADS, QK_NOPE_DIM + ROPE_DIM)
    wq = jnp.pad(wq, ((0, 0), (0, 0), (0, 0), (0, HEAD_PAD - QK_NOPE_DIM - ROPE_DIM)))
    wq = wq.reshape(depth, rq, N_HEADS * HEAD_PAD).astype(BF16)
    vec3 = lambda a: a.reshape(depth, 1, a.shape[-1])
    q_norm_g3, kv_norm_g3 = vec3(q_norm_g), vec3(kv_norm_g)
    conv_b3, conv_ln_g3, conv_ln_b3 = vec3(conv_b), vec3(conv_ln_g), vec3(conv_ln_b)
    ln1_g3, ln1_b3, ln2_g3, ln2_b3 = vec3(ln1_g), vec3(ln1_b), vec3(ln2_g), vec3(ln2_b)
    wr_t = w_router.T
    wr_hi = wr_t.astype(BF16)
    wr_lo = (wr_t - wr_hi.astype(F32)).astype(BF16)
    w_o_bf = w_o.astype(BF16)
    br = b_router.reshape(N_EXPERTS, 1)

    x = jnp.concatenate([x_prompt.reshape(tp, d), x_sample.reshape(tsmp, d)], axis=0)
    x_bf = x.astype(BF16)

    spare = jnp.zeros((n_tiles * MOE_TILE, d), F32)
    keep = CONV_WIDTH - 1
    outs = [[] for _ in range(6)]
    for l in range(depth):
        u = _glu_call(x_bf, w_in_t, l, d_conv, tm, 512)
        qn, ckv_p, ckv_s, kpe_p, kpe_s, kpad = _latent_call(x_bf, w_in_t, q_norm_g3, kv_norm_g3, tabs_k, l, splits,
                                                             rq, rkv, tm, tp)
        gates = _gates_call(x_bf, w_in_t, l, splits[3], tm, 2048)
        ga_p = _conv_prompt_call(u, gates, conv_w, conv_b3, conv_ln_g3, conv_ln_b3, w_conv_pw, l, nb, seq, 512, 128)
        ga_s = _conv_sample_call(u, gates, state_conv, conv_w, conv_b3, conv_ln_g3, conv_ln_b3, w_conv_pw,
                                 l, tp, db, ds, ds)
        q = _q_call(qn, wq, tabs_q, l, tm, scale)
        k, vt = _kv_call(ckv_p, kpad, w_kv_b, l, tp, tm)
        mixed_p = _attn_prompt_call(q, k, vt, ga_p, gates, nb, seq, tq, 2 * tq)
        mixed_s = _attn_sample_call(q, cache_kv_latent, cache_pe_t, ckv_s, kpe_s, w_kv_b, ga_s, gates, l, tp, db, ds)
        x1, gate, eidx = _wo_call(mixed_p, mixed_s, x, w_o_bf, ln1_g3, ln1_b3, wr_hi, wr_lo, br, l, tm, alpha)
        pos_rows, meta = _dispatch_call(eidx)
        pos0, pos1 = pos_rows[0], pos_rows[1]
        tile_meta = (meta[0, :n_tiles], meta[1, :n_tiles], meta[2, :1], meta[3, :n_tiles], meta[4, :n_tiles])
        xs = _scatter_call(pos0, pos1, x1, spare, tm)
        y = _experts_call(tile_meta, xs, w_gate, w_up, w_down, l)
        if l + 1 < depth:
            x, x_bf = _combine_call(pos0, pos1, y, x1, gate, ln2_g3, ln2_b3, l, tm, alpha)
        else:
            y_prompt, y_sample = _combine_call(pos0, pos1, y, x1, gate, ln2_g3, ln2_b3, l, tm, alpha, tp)
        spare = y
        outs[0].append(ckv_p.reshape(nb, seq, rkv))
        outs[1].append(kpe_p.reshape(nb, seq, ROPE_DIM))
        outs[2].append(u[:tp].reshape(nb, seq, d_conv)[:, seq - keep:])
        outs[3].append(ckv_s.reshape(db, ds, rkv))
        outs[4].append(kpe_s.reshape(db, ds, ROPE_DIM))
        outs[5].append(u[tp:].reshape(db, ds, d_conv)[:, ds - keep:])

    return (y_prompt.reshape(nb, seq, d), y_sample.reshape(db, ds, d)) + tuple(jnp.stack(o) for o in outs)
```
